```python
import jax, jax.numpy as jnp
from jax import lax
import numpy as np

D_MODEL = 1024
BATCH = 8
SEQ = 8192
DEPTH = 4

CTX_LEN = 256
GRID_W = 64
N_MIX = 2
EPS = 1e-6

MLA_HEADS = 8
MLA_Q_LORA = 256
MLA_KV_LORA = 128
MLA_NOPE = 128
MLA_ROPE = 64
MLA_V = 128
MLA_QK = MLA_NOPE + MLA_ROPE
ROPE_BASE = 10000.0
Q_BLOCK = 128

RG_WIDTH = D_MODEL
RG_BLOCKS = 4
RG_BLOCK = RG_WIDTH // RG_BLOCKS
RG_CONV = 4
RG_CONV_LEFT = 2
RG_C = 8.0

FFN_HIDDEN = 2816
N_EXPERTS = 8
TOP_K = 2
EXPERT_HIDDEN = 3584

kernel_name = 'hybrid_mla_rglru_moe_dit'


def rms_norm(x, g):
    xf = x.astype(jnp.float32)
    y = xf * lax.rsqrt(jnp.mean(xf * xf, axis=-1, keepdims=True) + EPS)
    return (y * g.astype(jnp.float32)).astype(x.dtype)


def modulate(h, shift, scale):
    return h * (1 + scale) + shift


def ada_mod(cvec, w, b):
    return jnp.split(jax.nn.silu(cvec) @ w + b, 6, axis=-1)


def axial_rope(n_tok):
    rows = n_tok // GRID_W
    row = jnp.repeat(jnp.arange(rows, dtype=jnp.float32), GRID_W)
    col = jnp.tile(jnp.arange(GRID_W, dtype=jnp.float32), rows)
    n_freq = MLA_ROPE // 4
    inv = ROPE_BASE ** (-jnp.arange(n_freq, dtype=jnp.float32) / n_freq)
    ang = jnp.concatenate([row[:, None] * inv, col[:, None] * inv], axis=-1)
    return jnp.cos(ang), jnp.sin(ang)


def apply_rope(x, cos, sin):
    half = MLA_ROPE // 2
    c = cos[None, :, None, :].astype(x.dtype)
    s = sin[None, :, None, :].astype(x.dtype)
    x1, x2 = x[..., :half], x[..., half:]
    return jnp.concatenate([x1 * c - x2 * s, x2 * c + x1 * s], axis=-1)


def mla_q(h, w_dq, g_q, w_uq, g_qn, rope):
    b, n, _ = h.shape
    q = (rms_norm(h @ w_dq, g_q) @ w_uq).reshape(b, n, MLA_HEADS, MLA_QK)
    q = rms_norm(q, g_qn)
    if rope is not None:
        q = jnp.concatenate([q[..., :MLA_NOPE], apply_rope(q[..., MLA_NOPE:], *rope)], axis=-1)
    return q


def mla_kv(h, w_dkv, g_kv, w_ukv, g_kn, rope):
    b, n, _ = h.shape
    kv_a = h @ w_dkv
    c_kv = rms_norm(kv_a[..., :MLA_KV_LORA], g_kv)
    k_pe = kv_a[..., MLA_KV_LORA:]
    kv = (c_kv @ w_ukv).reshape(b, n, MLA_HEADS, MLA_NOPE + MLA_V)
    k = jnp.concatenate([kv[..., :MLA_NOPE],
                         jnp.broadcast_to(k_pe[:, :, None, :], (b, n, MLA_HEADS, MLA_ROPE))], axis=-1)
    k = rms_norm(k, g_kn)
    if rope is not None:
        k = jnp.concatenate([k[..., :MLA_NOPE], apply_rope(k[..., MLA_NOPE:], *rope)], axis=-1)
    return k, kv[..., MLA_NOPE:]


def attend(q, k, v):
    s = jnp.einsum('bqhd,bkhd->bhqk', q, k, preferred_element_type=jnp.float32) * (MLA_QK ** -0.5)
    p = jax.nn.softmax(s, axis=-1).astype(v.dtype)
    return jnp.einsum('bhqk,bkhd->bqhd', p, v)


def mla_mixer(hl, hc, w_dq, g_q, w_uq, w_dkv, g_kv, w_ukv, g_qn, g_kn, w_o, need_ctx):
    b, n, _ = hl.shape
    rope = axial_rope(n)
    ql = mla_q(hl, w_dq, g_q, w_uq, g_qn, rope)
    kl, vl = mla_kv(hl, w_dkv, g_kv, w_ukv, g_kn, rope)
    kc, vc = mla_kv(hc, w_dkv, g_kv, w_ukv, g_kn, None)
    k_all = jnp.concatenate([kl, kc], axis=1)
    v_all = jnp.concatenate([vl, vc], axis=1)
    nb = n // Q_BLOCK
    qb = ql.reshape(b, nb, Q_BLOCK, MLA_HEADS, MLA_QK).swapaxes(0, 1)
    ol = lax.map(lambda qi: attend(qi, k_all, v_all), qb)
    ol = ol.swapaxes(0, 1).reshape(b, n, MLA_HEADS * MLA_V) @ w_o
    oc = None
    if need_ctx:
        qc = mla_q(hc, w_dq, g_q, w_uq, g_qn, None)
        oc = attend(qc, kc, vc).reshape(b, hc.shape[1], MLA_HEADS * MLA_V) @ w_o
    return ol, oc


def centred_conv(u, w, b):
    n = u.shape[1]
    up = jnp.pad(u, ((0, 0), (RG_CONV_LEFT, RG_CONV - 1 - RG_CONV_LEFT), (0, 0)))
    out = b
    for k in range(RG_CONV):
        out = out + up[:, k:k + n] * w[k]
    return out


def block_diag(u, w, b):
    ub = u.reshape(u.shape[:-1] + (RG_BLOCKS, RG_BLOCK))
    return jnp.einsum('bnhi,hij->bnhj', ub, w).reshape(u.shape) + b


def rg_coeffs(u, w_a, b_a, w_x, b_x, lam):
    f32 = jnp.float32
    uf = u.astype(f32)
    r = jax.nn.sigmoid(block_diag(uf, w_a.astype(f32), b_a.astype(f32)))
    i = jax.nn.sigmoid(block_diag(uf, w_x.astype(f32), b_x.astype(f32)))
    log_a = -RG_C * r * jax.nn.softplus(-lam.astype(f32))
    return jnp.exp(log_a), jnp.sqrt(-jnp.expm1(2.0 * log_a)) * (i * uf)


def affine_combine(left, right):
    a1, b1 = left
    a2, b2 = right
    return a1 * a2, a2 * b1 + b2


def linear_scan(a, b, h0, reverse):
    if reverse:
        a, b = jnp.flip(a, 1), jnp.flip(b, 1)
    b = b.at[:, 0].add(a[:, 0] * h0)
    _, h = lax.associative_scan(affine_combine, (a, b), axis=1)
    return jnp.flip(h, 1) if reverse else h


def rg_direction(uc, ul, w_a, b_a, w_x, b_x, lam, reverse):
    ac, bc = rg_coeffs(uc, w_a, b_a, w_x, b_x, lam)
    hc = linear_scan(ac, bc, jnp.zeros((uc.shape[0], RG_WIDTH), jnp.float32), reverse)
    h_end = hc[:, 0] if reverse else hc[:, -1]
    al, bl = rg_coeffs(ul, w_a, b_a, w_x, b_x, lam)
    return hc, linear_scan(al, bl, h_end, reverse)


def rglru_mixer(hl, hc, w_in, w_gate, conv_w, conv_b, w_a, b_a, w_x, b_x, lam, w_out, need_ctx):
    ul = centred_conv(hl @ w_in, conv_w, conv_b)
    uc = centred_conv(hc @ w_in, conv_w, conv_b)
    hc_f, hl_f = rg_direction(uc, ul, w_a[0], b_a[0], w_x[0], b_x[0], lam[0], False)
    hc_b, hl_b = rg_direction(uc, ul, w_a[1], b_a[1], w_x[1], b_x[1], lam[1], True)
    ol = ((hl_f + hl_b).astype(hl.dtype) * jax.nn.gelu(hl @ w_gate)) @ w_out
    oc = None
    if need_ctx:
        oc = ((hc_f + hc_b).astype(hc.dtype) * jax.nn.gelu(hc @ w_gate)) @ w_out
    return ol, oc


def swiglu(h, w1, w3, w2):
    return (jax.nn.silu(h @ w1) * (h @ w3)) @ w2


def moe_swiglu(h, w_router, w1, w3, w2):
    logits = jnp.einsum('bnd,de->bne', h, w_router, preferred_element_type=jnp.float32)
    top_val, top_idx = lax.top_k(logits, TOP_K)
    gates = jax.nn.softmax(top_val, axis=-1)
    combine = jnp.einsum('bnk,bnke->bne', gates,
                         jax.nn.one_hot(top_idx, N_EXPERTS, dtype=jnp.float32)).astype(h.dtype)
    out = jnp.zeros_like(h)
    for e in range(N_EXPERTS):
        out = out + combine[..., e:e + 1] * swiglu(h, w1[e], w3[e], w2[e])
    return out


def setup_inputs(seed: int = 0) -> dict:
    key = jax.random.key(seed)
    ks = iter(jax.random.split(key, 40))
    f32 = jnp.float32
    n_a = (DEPTH + 1) // 2
    n_r = DEPTH // 2

    def nrm(shape, fan_in, gain=1.0):
        return gain * fan_in ** -0.5 * jax.random.normal(next(ks), shape, f32)

    def gain_vec(shape):
        return 1.0 + 0.05 * jax.random.normal(next(ks), shape, f32)

    def small(shape):
        return 0.02 * jax.random.normal(next(ks), shape, f32)

    x = jax.random.normal(next(ks), (BATCH, SEQ, D_MODEL), f32)
    c = jax.random.normal(next(ks), (BATCH, D_MODEL), f32)
    ctx = jax.random.normal(next(ks), (BATCH, CTX_LEN, D_MODEL), f32)
    c_ctx = jax.random.normal(next(ks), (D_MODEL,), f32)
    ada_w = nrm((DEPTH, D_MODEL, 6 * D_MODEL), D_MODEL, 0.5)
    ada_b = small((DEPTH, 6 * D_MODEL))
    norm1_g = gain_vec((DEPTH, D_MODEL))
    norm2_g = gain_vec((DEPTH, D_MODEL))
    mla_w_dq = nrm((n_a, D_MODEL, MLA_Q_LORA), D_MODEL)
    mla_g_q = gain_vec((n_a, MLA_Q_LORA))
    mla_w_uq = nrm((n_a, MLA_Q_LORA, MLA_HEADS * MLA_QK), MLA_Q_LORA)
    mla_w_dkv = nrm((n_a, D_MODEL, MLA_KV_LORA + MLA_ROPE), D_MODEL)
    mla_g_kv = gain_vec((n_a, MLA_KV_LORA))
    mla_w_ukv = nrm((n_a, MLA_KV_LORA, MLA_HEADS * (MLA_NOPE + MLA_V)), MLA_KV_LORA)
    mla_g_qn = gain_vec((n_a, MLA_QK))
    mla_g_kn = gain_vec((n_a, MLA_QK))
    mla_w_o = nrm((n_a, MLA_HEADS * MLA_V, D_MODEL), MLA_HEADS * MLA_V)
    rg_w_in = nrm((n_r, D_MODEL, RG_WIDTH), D_MODEL)
    rg_w_gate = nrm((n_r, D_MODEL, RG_WIDTH), D_MODEL)
    rg_conv_w = nrm((n_r, RG_CONV, RG_WIDTH), RG_CONV)
    rg_conv_b = small((n_r, RG_WIDTH))
    rg_w_a = nrm((n_r, 2, RG_BLOCKS, RG_BLOCK, RG_BLOCK), RG_BLOCK)
    rg_b_a = small((n_r, 2, RG_WIDTH))
    rg_w_x = nrm((n_r, 2, RG_BLOCKS, RG_BLOCK, RG_BLOCK), RG_BLOCK)
    rg_b_x = small((n_r, 2, RG_WIDTH))
    a0 = jax.random.uniform(next(ks), (n_r, 2, RG_WIDTH), f32, 0.9, 0.999)
    rg_lam = jnp.log(a0) - jnp.log1p(-a0)
    rg_w_out = nrm((n_r, RG_WIDTH, D_MODEL), RG_WIDTH)
    ffn_w1 = nrm((n_a, D_MODEL, FFN_HIDDEN), D_MODEL)
    ffn_w3 = nrm((n_a, D_MODEL, FFN_HIDDEN), D_MODEL)
    ffn_w2 = nrm((n_a, FFN_HIDDEN, D_MODEL), FFN_HIDDEN)
    moe_w_router = nrm((n_r, D_MODEL, N_EXPERTS), D_MODEL)
    moe_w1 = nrm((n_r, N_EXPERTS, D_MODEL, EXPERT_HIDDEN), D_MODEL)
    moe_w3 = nrm((n_r, N_EXPERTS, D_MODEL, EXPERT_HIDDEN), D_MODEL)
    moe_w2 = nrm((n_r, N_EXPERTS, EXPERT_HIDDEN, D_MODEL), EXPERT_HIDDEN)
    return {'x': x, 'c': c, 'ctx': ctx, 'c_ctx': c_ctx,
            'ada_w': ada_w, 'ada_b': ada_b, 'norm1_g': norm1_g, 'norm2_g': norm2_g,
            'mla_w_dq': mla_w_dq, 'mla_g_q': mla_g_q, 'mla_w_uq': mla_w_uq, 'mla_w_dkv': mla_w_dkv,
            'mla_g_kv': mla_g_kv, 'mla_w_ukv': mla_w_ukv, 'mla_g_qn': mla_g_qn, 'mla_g_kn': mla_g_kn,
            'mla_w_o': mla_w_o,
            'rg_w_in': rg_w_in, 'rg_w_gate': rg_w_gate, 'rg_conv_w': rg_conv_w, 'rg_conv_b': rg_conv_b,
            'rg_w_a': rg_w_a, 'rg_b_a': rg_b_a, 'rg_w_x': rg_w_x, 'rg_b_x': rg_b_x, 'rg_lam': rg_lam,
            'rg_w_out': rg_w_out,
            'ffn_w1': ffn_w1, 'ffn_w3': ffn_w3, 'ffn_w2': ffn_w2,
            'moe_w_router': moe_w_router, 'moe_w1': moe_w1, 'moe_w3': moe_w3, 'moe_w2': moe_w2}


def reference(x, c, ctx, c_ctx, ada_w, ada_b, norm1_g, norm2_g,
              mla_w_dq, mla_g_q, mla_w_uq, mla_w_dkv, mla_g_kv, mla_w_ukv, mla_g_qn, mla_g_kn, mla_w_o,
              rg_w_in, rg_w_gate, rg_conv_w, rg_conv_b, rg_w_a, rg_b_a, rg_w_x, rg_b_x, rg_lam, rg_w_out,
              ffn_w1, ffn_w3, ffn_w2, moe_w_router, moe_w1, moe_w3, moe_w2):
    for i in range(DEPTH):
        need_ctx = i < DEPTH - 1
        j = i // N_MIX
        sh1_l, sc1_l, g1_l, sh2_l, sc2_l, g2_l = [m[:, None, :] for m in ada_mod(c, ada_w[i], ada_b[i])]
        sh1_c, sc1_c, g1_c, sh2_c, sc2_c, g2_c = ada_mod(c_ctx, ada_w[i], ada_b[i])

        hl = modulate(rms_norm(x, norm1_g[i]), sh1_l, sc1_l)
        hc = modulate(rms_norm(ctx, norm1_g[i]), sh1_c, sc1_c)
        if i % N_MIX == 0:
            ol, oc = mla_mixer(hl, hc, mla_w_dq[j], mla_g_q[j], mla_w_uq[j], mla_w_dkv[j], mla_g_kv[j],
                               mla_w_ukv[j], mla_g_qn[j], mla_g_kn[j], mla_w_o[j], need_ctx)
        else:
            ol, oc = rglru_mixer(hl, hc, rg_w_in[j], rg_w_gate[j], rg_conv_w[j], rg_conv_b[j],
                                 rg_w_a[j], rg_b_a[j], rg_w_x[j], rg_b_x[j], rg_lam[j], rg_w_out[j], need_ctx)
        x = x + g1_l * ol

        if i % 2 == 0:
            ffn = lambda h: swiglu(h, ffn_w1[j], ffn_w3[j], ffn_w2[j])
        else:
            ffn = lambda h: moe_swiglu(h, moe_w_router[j], moe_w1[j], moe_w3[j], moe_w2[j])
        x = x + g2_l * ffn(modulate(rms_norm(x, norm2_g[i]), sh2_l, sc2_l))

        if need_ctx:
            ctx = ctx + g1_c * oc
            ctx = ctx + g2_c * ffn(modulate(rms_norm(ctx, norm2_g[i]), sh2_c, sc2_c))
    return x
```

```python
import functools
import math

import jax
import jax.numpy as jnp
from jax import lax
from jax.experimental import pallas as pl
from jax.experimental.pallas import tpu as pltpu

F32 = jnp.float32
BF16 = jnp.bfloat16

EPS = 1e-6
GRID_W = 64
N_HEADS = 8
D_NOPE = 128
D_ROPE = 64
D_V = 128
D_QK = D_NOPE + D_ROPE
D_HEAD_PAD = 256
Q_EXT = 384
ROPE_BASE = 10000.0
RG_BLOCKS = 4
RG_C = 8.0
TOP_K = 2
LANES = 128
SUBLANES = 8
VMEM_LIMIT = 56 * 1024 * 1024
LOG2E = 1.4426950408889634


def _cparams(sem):
    return pltpu.CompilerParams(dimension_semantics=sem, vmem_limit_bytes=VMEM_LIMIT)


def _resident(shape):
    nd = len(shape)
    return pl.BlockSpec(shape, lambda *_: (0,) * nd, pipeline_mode=pl.Buffered(1))


def _dot(a, b):
    return jnp.dot(a, b, preferred_element_type=F32)


def _rms(xf, g):
    return xf * lax.rsqrt(jnp.mean(xf * xf, axis=-1, keepdims=True) + EPS) * g


def _prenorm(xf, g, shift, scale):
    return _rms(xf, g) * (1.0 + scale) + shift


def _ada_kernel(c_ref, w_ref, b_ref, o_ref):
    s = jax.nn.silu(c_ref[...])
    o_ref[0] = jnp.dot(s, w_ref[0], precision=lax.Precision.HIGHEST,
                       preferred_element_type=F32) + b_ref[0]


def _ada_all(cvec, ada_w, ada_b):
    depth, d, n6 = ada_w.shape
    rows = cvec.shape[0]
    tn = 1536
    return pl.pallas_call(
        _ada_kernel,
        grid=(depth, n6 // tn),
        in_specs=[pl.BlockSpec((rows, d), lambda l, j: (0, 0)),
                  pl.BlockSpec((1, d, tn), lambda l, j: (l, 0, j)),
                  pl.BlockSpec((1, 1, tn), lambda l, j: (l, 0, j))],
        out_specs=pl.BlockSpec((1, rows, tn), lambda l, j: (l, 0, j)),
        out_shape=jax.ShapeDtypeStruct((depth, rows, n6), F32),
        compiler_params=_cparams(("arbitrary", "arbitrary")),
        name="ada_mod",
    )(cvec, ada_w, ada_b.reshape(depth, 1, n6))


def _mla_pre_kernel(x_ref, sh_ref, sc_ref, ng_ref, cos_ref, sin_ref,
                    wdq_ref, gq_ref, wuq_ref, gqn_ref,
                    wdkv_ref, gkv_ref, wuk_ref, wuvt_ref, gkn_ref,
                    q_ref, k_ref, vt_ref, *, q_scale):
    x = x_ref[0]
    h = _prenorm(x, ng_ref[...], sh_ref[0], sc_ref[0]).astype(BF16)
    cosv = cos_ref[...]
    sinv = sin_ref[...]

    qn = _rms(_dot(h, wdq_ref[...]), gq_ref[...]).astype(BF16)
    qall = _dot(qn, wuq_ref[...])
    g_n = gqn_ref[:, 0:LANES]
    g_a = gqn_ref[:, LANES:2 * LANES]
    g_b = gqn_ref[:, 2 * LANES:3 * LANES]
    for hh in range(N_HEADS):
        base = hh * Q_EXT
        nope = qall[:, base:base + LANES]
        ra = qall[:, base + LANES:base + 2 * LANES]
        rb = qall[:, base + 2 * LANES:base + 3 * LANES]
        ss = jnp.sum(nope * nope, axis=-1, keepdims=True) + jnp.sum(ra * ra, axis=-1, keepdims=True)
        inv = lax.rsqrt(ss * (1.0 / D_QK) + EPS) * q_scale
        q_ref[0, hh, :, 0:LANES] = (nope * g_n * inv).astype(BF16)
        q_ref[0, hh, :, LANES:2 * LANES] = ((ra * g_a * cosv + rb * g_b * sinv) * inv).astype(BF16)

    kva = _dot(h, wdkv_ref[...])
    ckv = _rms(kva[:, 0:LANES], gkv_ref[...]).astype(BF16)
    pa = kva[:, LANES:2 * LANES]
    pb = kva[:, 2 * LANES:3 * LANES]
    pe_ss = jnp.sum(pa * pa, axis=-1, keepdims=True)
    k_n = gkn_ref[:, 0:LANES]
    k_a = gkn_ref[:, LANES:2 * LANES]
    k_b = gkn_ref[:, 2 * LANES:3 * LANES]
    rope = pa * k_a * cosv + pb * k_b * sinv
    knope = _dot(ckv, wuk_ref[...])
    vt_all = lax.dot_general(wuvt_ref[...], ckv, (((1,), (1,)), ((), ())),
                             preferred_element_type=F32)
    for hh in range(N_HEADS):
        kn = knope[:, hh * LANES:(hh + 1) * LANES]
        ss = jnp.sum(kn * kn, axis=-1, keepdims=True) + pe_ss
        inv = lax.rsqrt(ss * (1.0 / D_QK) + EPS)
        k_ref[0, hh, :, 0:LANES] = (kn * k_n * inv).astype(BF16)
        k_ref[0, hh, :, LANES:2 * LANES] = (rope * inv).astype(BF16)
        vt_ref[0, hh, 0] = vt_all[hh * D_V:(hh + 1) * D_V, :].astype(BF16)


def _mla_pre(x, shift, scale, norm_g, cos_t, sin_t, w, tm):
    b, n, d = x.shape
    nt = n // tm
    q_scale = (D_QK ** -0.5) * LOG2E
    tok = lambda bi, i: (bi, i, 0)
    vec = lambda bi, i: (bi, 0, 0)
    out_shapes = (jax.ShapeDtypeStruct((b, N_HEADS, n, D_HEAD_PAD), BF16),
                  jax.ShapeDtypeStruct((b, N_HEADS, n, D_HEAD_PAD), BF16),
                  jax.ShapeDtypeStruct((b, N_HEADS, nt, D_V, tm), BF16))
    weights = (w["w_dq"], w["g_q"], w["w_uq"], w["g_qn"], w["w_dkv"], w["g_kv"], w["w_uk"], w["w_uvt"], w["g_kn"])
    return pl.pallas_call(
        functools.partial(_mla_pre_kernel, q_scale=q_scale),
        grid=(b, nt),
        in_specs=[pl.BlockSpec((1, tm, d), tok),
                  pl.BlockSpec((1, 1, d), vec), pl.BlockSpec((1, 1, d), vec),
                  _resident(norm_g.shape),
                  pl.BlockSpec((tm, LANES), lambda bi, i: (i, 0)),
                  pl.BlockSpec((tm, LANES), lambda bi, i: (i, 0))]
                 + [_resident(a.shape) for a in weights],
        out_specs=(pl.BlockSpec((1, N_HEADS, tm, D_HEAD_PAD), lambda bi, i: (bi, 0, i, 0)),
                   pl.BlockSpec((1, N_HEADS, tm, D_HEAD_PAD), lambda bi, i: (bi, 0, i, 0)),
                   pl.BlockSpec((1, N_HEADS, 1, D_V, tm), lambda bi, i: (bi, 0, i, 0, 0))),
        out_shape=out_shapes,
        compiler_params=_cparams(("arbitrary", "arbitrary")),
        name="mla_pre",
    )(x, shift, scale, norm_g, cos_t, sin_t, *weights)


def _attn_kernel(q_ref, *refs, n_src):
    srcs = [(refs[2 * s], refs[2 * s + 1]) for s in range(n_src)]
    o_ref = refs[2 * n_src]
    m_ref, l_ref, acc_ref = refs[2 * n_src + 1:]

    q_t = q_ref[0, 0].astype(F32).T.astype(BF16)
    m_ref[...] = jnp.full(m_ref.shape, -1e30, F32)
    l_ref[...] = jnp.zeros(l_ref.shape, F32)
    acc_ref[...] = jnp.zeros(acc_ref.shape, F32)

    def step(kc, vtc):
        s = _dot(kc, q_t)
        m_old = m_ref[...]
        m_new = jnp.maximum(m_old, jnp.max(s, axis=0, keepdims=True))
        alpha = jnp.exp2(m_old - m_new)
        p = jnp.exp2(s - m_new)
        l_ref[...] = alpha * l_ref[...] + jnp.sum(p, axis=0, keepdims=True)
        acc_ref[...] = alpha * acc_ref[...] + _dot(vtc, p.astype(BF16))
        m_ref[...] = m_new

    for k_ref, vt_ref in srcs:
        nchunks = k_ref.shape[2]
        if nchunks == 1:
            step(k_ref[0, 0, 0], vt_ref[0, 0, 0])
        else:
            def body(j, carry, k_ref=k_ref, vt_ref=vt_ref):
                step(k_ref[0, 0, j], vt_ref[0, 0, j])
                return carry
            lax.fori_loop(0, nchunks, body, 0)

    o = acc_ref[...] * (1.0 / l_ref[...])
    o_ref[0] = o.T.astype(BF16)


def _attention(q, srcs, tq):
    b, h, nq, dp = q.shape
    in_specs = [pl.BlockSpec((1, 1, tq, dp), lambda bi, hi, i: (bi, hi, i, 0))]
    args = [q]
    for k5, vt5 in srcs:
        in_specs.append(pl.BlockSpec((1, 1) + k5.shape[2:], lambda bi, hi, i: (bi, hi, 0, 0, 0)))
        in_specs.append(pl.BlockSpec((1, 1) + vt5.shape[2:], lambda bi, hi, i: (bi, hi, 0, 0, 0)))
        args += [k5, vt5]
    return pl.pallas_call(
        functools.partial(_attn_kernel, n_src=len(srcs)),
        grid=(b, h, nq // tq),
        in_specs=in_specs,
        out_specs=pl.BlockSpec((1, tq, D_V), lambda bi, hi, i: (bi, i, hi)),
        out_shape=jax.ShapeDtypeStruct((b, nq, h * D_V), BF16),
        scratch_shapes=[pltpu.VMEM((1, tq), F32), pltpu.VMEM((1, tq), F32), pltpu.VMEM((D_V, tq), F32)],
        compiler_params=_cparams(("arbitrary", "arbitrary", "arbitrary")),
        name="mla_attn",
    )(*args)


def _mla_ffn_kernel(x_ref, o_ref, g1_ref, sh_ref, sc_ref, g2_ref, ng_ref,
                    wo_ref, w1_ref, w3_ref, w2_ref, out_ref, *, n_chunks):
    x1 = x_ref[0] + g1_ref[0] * _dot(o_ref[0], wo_ref[...])
    h2 = _prenorm(x1, ng_ref[...], sh_ref[0], sc_ref[0]).astype(BF16)
    hid = w1_ref.shape[1]
    hc = hid // n_chunks
    y = None
    for c in range(n_chunks):
        a = _dot(h2, w1_ref[:, c * hc:(c + 1) * hc])
        g = _dot(h2, w3_ref[:, c * hc:(c + 1) * hc])
        act = (jax.nn.silu(a) * g).astype(BF16)
        yc = _dot(act, w2_ref[c * hc:(c + 1) * hc, :])
        y = yc if y is None else y + yc
    out_ref[0] = x1 + g2_ref[0] * y


def _mla_ffn(x, o, gate1, shift2, scale2, gate2, norm_g, w_o, w1, w3, w2, tm):
    b, n, d = x.shape
    hid = w1.shape[1]
    n_chunks = 2 if (hid // 2) % LANES == 0 else 1
    tok = lambda bi, i: (bi, i, 0)
    vec = lambda bi, i: (bi, 0, 0)
    return pl.pallas_call(
        functools.partial(_mla_ffn_kernel, n_chunks=n_chunks),
        grid=(b, n // tm),
        in_specs=[pl.BlockSpec((1, tm, d), tok), pl.BlockSpec((1, tm, o.shape[2]), tok),
                  pl.BlockSpec((1, 1, d), vec), pl.BlockSpec((1, 1, d), vec),
                  pl.BlockSpec((1, 1, d), vec), pl.BlockSpec((1, 1, d), vec),
                  _resident(norm_g.shape), _resident(w_o.shape),
                  _resident(w1.shape), _resident(w3.shape), _resident(w2.shape)],
        out_specs=pl.BlockSpec((1, tm, d), tok),
        out_shape=jax.ShapeDtypeStruct((b, n, d), F32),
        compiler_params=_cparams(("arbitrary", "arbitrary")),
        name="mla_ffn",
    )(x, o, gate1, shift2, scale2, gate2, norm_g, w_o, w1, w3, w2)


def _rg_in_kernel(xp_ref, xm_ref, xn_ref, sh_ref, sc_ref, ng_ref, win_ref, wg_ref, cw_ref, cb_ref,
                  u_ref, gl_ref, *, tm):
    i = pl.program_id(1)
    nt = pl.num_programs(1)
    x_ext = jnp.concatenate([xp_ref[0], xm_ref[0], xn_ref[0]], axis=0)
    h = _prenorm(x_ext, ng_ref[...], sh_ref[0], sc_ref[0]).astype(BF16)
    u_ext = _dot(h, win_ref[...])
    row = lax.broadcasted_iota(jnp.int32, (tm + 2 * SUBLANES, 1), 0)
    valid = jnp.logical_and(jnp.logical_or(row >= SUBLANES, i > 0),
                            jnp.logical_or(row < tm + SUBLANES, i < nt - 1))
    u_ext = jnp.where(valid, u_ext, 0.0)
    acc = cb_ref[...] + cw_ref[0:1, :] * u_ext[6:6 + tm]
    for k in range(1, 4):
        acc = acc + cw_ref[k:k + 1, :] * u_ext[6 + k:6 + k + tm]
    u_ref[0] = acc.astype(BF16)
    gl_ref[0] = jax.nn.gelu(_dot(h[SUBLANES:SUBLANES + tm], wg_ref[...])).astype(BF16)


def _rg_in(x, shift, scale, norm_g, w_in, w_gate, conv_w, conv_b, tm):
    b, n, d = x.shape
    c = w_in.shape[1]
    r = tm // SUBLANES
    last = n // SUBLANES - 1
    vec = lambda bi, i: (bi, 0, 0)
    tok = lambda bi, i: (bi, i, 0)
    return pl.pallas_call(
        functools.partial(_rg_in_kernel, tm=tm),
        grid=(b, n // tm),
        in_specs=[pl.BlockSpec((1, SUBLANES, d), lambda bi, i: (bi, jnp.maximum(i * r - 1, 0), 0)),
                  pl.BlockSpec((1, tm, d), tok),
                  pl.BlockSpec((1, SUBLANES, d), lambda bi, i: (bi, jnp.minimum((i + 1) * r, last), 0)),
                  pl.BlockSpec((1, 1, d), vec), pl.BlockSpec((1, 1, d), vec),
                  _resident(norm_g.shape), _resident(w_in.shape), _resident(w_gate.shape),
                  _resident(conv_w.shape), _resident(conv_b.shape)],
        out_specs=(pl.BlockSpec((1, tm, c), tok), pl.BlockSpec((1, tm, c), tok)),
        out_shape=(jax.ShapeDtypeStruct((b, n, c), BF16), jax.ShapeDtypeStruct((b, n, c), BF16)),
        compiler_params=_cparams(("arbitrary", "arbitrary")),
        name="rg_in",
    )(x, x, x, shift, scale, norm_g, w_in, w_gate, conv_w, conv_b)


def _rg_scan_kernel(u_ref, h0_ref, wa_ref, ba_ref, wx_ref, bx_ref, lam_ref,
                    hout_ref, hfin_ref, a_s, b_s, o_s, h_s, *, reverse, tn, pitch):
    @pl.when(pl.program_id(1) == 0)
    def _():
        h_s[...] = h0_ref[0]

    u = u_ref[0]
    width = u.shape[1]
    bw = width // RG_BLOCKS
    for blk in range(RG_BLOCKS):
        cs = slice(blk * bw, (blk + 1) * bw)
        ub = u[:, cs]
        r = jax.nn.sigmoid(_dot(ub, wa_ref[blk]) + ba_ref[:, cs])
        ig = jax.nn.sigmoid(_dot(ub, wx_ref[blk]) + bx_ref[:, cs])
        log_a = -RG_C * r * jax.nn.softplus(-lam_ref[:, cs])
        a = jnp.exp(log_a)
        bb = jnp.sqrt(1.0 - a * a) * (ig * ub.astype(F32))
        for half in range(bw // LANES):
            j = blk * (bw // LANES) + half
            a_s[j * pitch:j * pitch + tn, :] = a[:, half * LANES:(half + 1) * LANES]
            b_s[j * pitch:j * pitch + tn, :] = bb[:, half * LANES:(half + 1) * LANES]

    n_ct = width // LANES

    def body(g, h):
        for s in range(SUBLANES):
            t = g * SUBLANES + s
            if reverse:
                t = tn - 1 - t
            idx = pl.ds(t, n_ct, stride=pitch)
            h = a_s[idx, :] * h + b_s[idx, :]
            o_s[idx, :] = h
        return h

    h = lax.fori_loop(0, tn // SUBLANES, body, h_s[...])
    h_s[...] = h
    hfin_ref[0] = h
    for j in range(n_ct):
        hout_ref[0, :, j * LANES:(j + 1) * LANES] = o_s[j * pitch:j * pitch + tn, :].astype(BF16)


def _rg_scan(u, h0, w_a, b_a, w_x, b_x, lam, reverse, tn):
    b, n, c = u.shape
    nt = n // tn
    n_ct = c // LANES
    pitch = tn + SUBLANES
    tmap = (lambda bi, i: (bi, nt - 1 - i, 0)) if reverse else (lambda bi, i: (bi, i, 0))
    st = lambda bi, i: (bi, 0, 0)
    scr = pltpu.VMEM((n_ct * pitch, LANES), F32)
    return pl.pallas_call(
        functools.partial(_rg_scan_kernel, reverse=reverse, tn=tn, pitch=pitch),
        grid=(b, nt),
        in_specs=[pl.BlockSpec((1, tn, c), tmap), pl.BlockSpec((1, n_ct, LANES), st),
                  _resident(w_a.shape), _resident(b_a.shape), _resident(w_x.shape),
                  _resident(b_x.shape), _resident(lam.shape)],
        out_specs=(pl.BlockSpec((1, tn, c), tmap), pl.BlockSpec((1, n_ct, LANES), st)),
        out_shape=(jax.ShapeDtypeStruct((b, n, c), BF16), jax.ShapeDtypeStruct((b, n_ct, LANES), F32)),
        scratch_shapes=[scr, scr, scr, pltpu.VMEM((n_ct, LANES), F32)],
        compiler_params=_cparams(("arbitrary", "arbitrary")),
        name="rg_scan_bwd" if reverse else "rg_scan_fwd",
    )(u, h0, w_a, b_a, w_x, b_x, lam)


def _rg_out_kernel(x_ref, hf_ref, hb_ref, gl_ref, g1_ref, sh_ref, sc_ref, ng_ref, wout_ref, wr_ref,
                   x1_ref, h2_ref, comb_ref, *, n_experts):
    y = ((hf_ref[0].astype(F32) + hb_ref[0].astype(F32)) * gl_ref[0].astype(F32)).astype(BF16)
    x1 = x_ref[0] + g1_ref[0] * _dot(y, wout_ref[...])
    x1_ref[0] = x1
    h2 = _prenorm(x1, ng_ref[...], sh_ref[0], sc_ref[0])
    h2_ref[0] = h2.astype(BF16)
    logits = jnp.dot(h2, wr_ref[...], precision=lax.Precision.HIGHEST, preferred_element_type=F32)
    lane = lax.broadcasted_iota(jnp.int32, logits.shape, 1).astype(F32)
    neg = jnp.float32(-jnp.inf)
    lg = jnp.where(lane < n_experts, logits, neg)
    m1 = jnp.max(lg, axis=-1, keepdims=True)
    i1 = jnp.min(jnp.where(lg == m1, lane, float(LANES)), axis=-1, keepdims=True)
    lg2 = jnp.where(lane == i1, neg, lg)
    m2 = jnp.max(lg2, axis=-1, keepdims=True)
    i2 = jnp.min(jnp.where(lg2 == m2, lane, float(LANES)), axis=-1, keepdims=True)
    e2 = jnp.exp(m2 - m1)
    den = 1.0 + e2
    comb_ref[0] = jnp.where(lane == i1, 1.0 / den, 0.0) + jnp.where(lane == i2, e2 / den, 0.0)


def _rg_out(x, hf, hb, gl, gate1, shift2, scale2, norm_g, w_out, w_router_pad, n_experts, tm):
    b, n, d = x.shape
    c = hf.shape[2]
    tok = lambda bi, i: (bi, i, 0)
    vec = lambda bi, i: (bi, 0, 0)
    return pl.pallas_call(
        functools.partial(_rg_out_kernel, n_experts=n_experts),
        grid=(b, n // tm),
        in_specs=[pl.BlockSpec((1, tm, d), tok), pl.BlockSpec((1, tm, c), tok), pl.BlockSpec((1, tm, c), tok),
                  pl.BlockSpec((1, tm, c), tok),
                  pl.BlockSpec((1, 1, d), vec), pl.BlockSpec((1, 1, d), vec), pl.BlockSpec((1, 1, d), vec),
                  _resident(norm_g.shape), _resident(w_out.shape), _resident(w_router_pad.shape)],
        out_specs=(pl.BlockSpec((1, tm, d), tok), pl.BlockSpec((1, tm, d), tok),
                   pl.BlockSpec((1, tm, LANES), tok)),
        out_shape=(jax.ShapeDtypeStruct((b, n, d), F32), jax.ShapeDtypeStruct((b, n, d), BF16),
                   jax.ShapeDtypeStruct((b, n, LANES), F32)),
        compiler_params=_cparams(("arbitrary", "arbitrary")),
        name="rg_out_router",
    )(x, hf, hb, gl, gate1, shift2, scale2, norm_g, w_out, w_router_pad)


def _moe_kernel(h2_ref, comb_ref, x1_ref, g2_ref, w1_ref, w3_ref, w2_ref, out_ref, acc_ref):
    e = pl.program_id(2)
    c = pl.program_id(3)
    first = jnp.logical_and(e == 0, c == 0)
    last = jnp.logical_and(e == pl.num_programs(2) - 1, c == pl.num_programs(3) - 1)

    @pl.when(first)
    def _():
        acc_ref[...] = jnp.zeros(acc_ref.shape, F32)

    h2 = h2_ref[0]
    act = (jax.nn.silu(_dot(h2, w1_ref[0])) * _dot(h2, w3_ref[0])).astype(BF16)
    comb = comb_ref[0]
    lane = lax.broadcasted_iota(jnp.int32, comb.shape, 1)
    cw = jnp.sum(jnp.where(lane == e, comb, 0.0), axis=-1, keepdims=True)
    acc_ref[...] += cw * _dot(act, w2_ref[0])

    @pl.when(last)
    def _():
        out_ref[0] = x1_ref[0] + g2_ref[0] * acc_ref[...]


def _moe(h2, comb, x1, gate2, w1, w3, w2, tm, hc):
    b, n, d = x1.shape
    n_e, _, hid = w1.shape
    tok = lambda bi, i, e, c: (bi, i, 0)
    vec = lambda bi, i, e, c: (bi, 0, 0)
    return pl.pallas_call(
        _moe_kernel,
        grid=(b, n // tm, n_e, hid // hc),
        in_specs=[pl.BlockSpec((1, tm, d), tok), pl.BlockSpec((1, tm, LANES), tok), pl.BlockSpec((1, tm, d), tok),
                  pl.BlockSpec((1, 1, d), vec),
                  pl.BlockSpec((1, d, hc), lambda bi, i, e, c: (e, 0, c)),
                  pl.BlockSpec((1, d, hc), lambda bi, i, e, c: (e, 0, c)),
                  pl.BlockSpec((1, hc, d), lambda bi, i, e, c: (e, c, 0))],
        out_specs=pl.BlockSpec((1, tm, d), tok),
        out_shape=jax.ShapeDtypeStruct((b, n, d), F32),
        scratch_shapes=[pltpu.VMEM((tm, d), F32)],
        compiler_params=_cparams(("arbitrary", "arbitrary", "arbitrary", "arbitrary")),
        name="moe_ffn",
    )(h2, comb, x1, gate2, w1, w3, w2)


def _prep_mla_weights(w_dq, g_q, w_uq, w_dkv, g_kv, w_ukv, g_qn, g_kn):
    ql = w_uq.shape[0]
    half = D_ROPE // 2
    pad = LANES - D_ROPE

    def ext_cols(w_rope):
        z = jnp.zeros(w_rope.shape[:-1] + (pad,), w_rope.dtype)
        r1, r2 = w_rope[..., :half], w_rope[..., half:]
        return jnp.concatenate([w_rope, z], -1), jnp.concatenate([-r2, r1, z], -1)

    def ext_gain(g):
        z = jnp.zeros((pad,), g.dtype)
        gr1, gr2 = g[D_NOPE:D_NOPE + half], g[D_NOPE + half:]
        return jnp.concatenate([g[:D_NOPE], gr1, gr2, z, gr2, gr1, z])[None, :].astype(F32)

    wq = w_uq.reshape(ql, N_HEADS, D_QK)
    qa, qb = ext_cols(wq[:, :, D_NOPE:])
    w_uq_ext = jnp.concatenate([wq[:, :, :D_NOPE], qa, qb], -1).reshape(ql, N_HEADS * Q_EXT)
    kl = w_ukv.shape[0]
    ka, kb = ext_cols(w_dkv[:, kl:])
    w_dkv_ext = jnp.concatenate([w_dkv[:, :kl], ka, kb], -1)
    wkv = w_ukv.reshape(kl, N_HEADS, D_NOPE + D_V)
    w_uk = wkv[:, :, :D_NOPE].reshape(kl, N_HEADS * D_NOPE)
    w_uvt = wkv[:, :, D_NOPE:].reshape(kl, N_HEADS * D_V).T
    return dict(w_dq=w_dq.astype(BF16), g_q=g_q[None, :].astype(F32), w_uq=w_uq_ext.astype(BF16),
                g_qn=ext_gain(g_qn), w_dkv=w_dkv_ext.astype(BF16), g_kv=g_kv[None, :].astype(F32),
                w_uk=w_uk.astype(BF16), w_uvt=w_uvt.astype(BF16), g_kn=ext_gain(g_kn))


def _rope_tables(n_tok):
    rows = n_tok // GRID_W
    row = jnp.repeat(jnp.arange(rows, dtype=F32), GRID_W)
    col = jnp.tile(jnp.arange(GRID_W, dtype=F32), rows)
    n_freq = D_ROPE // 4
    inv = ROPE_BASE ** (-jnp.arange(n_freq, dtype=F32) / n_freq)
    ang = jnp.concatenate([row[:, None] * inv, col[:, None] * inv], axis=-1)
    z = jnp.zeros((n_tok, LANES - D_ROPE), F32)
    cos, sin = jnp.cos(ang), jnp.sin(ang)
    return jnp.concatenate([cos, cos, z], -1), jnp.concatenate([sin, sin, z], -1)


def _no_rope_tables(n_tok):
    one = jnp.ones((n_tok, D_ROPE), F32)
    z = jnp.zeros((n_tok, LANES - D_ROPE), F32)
    return jnp.concatenate([one, z], -1), jnp.zeros((n_tok, LANES), F32)


def _tile(n, target):
    t = min(n, target)
    assert n % t == 0, (n, t)
    return t


def _mla_layer(x, ctx, mods_l, mods_c, norm1_g, norm2_g, w, w_o, ffn, need_ctx):
    b, n, d = x.shape
    nc = ctx.shape[1]
    tm = _tile(n, 512)
    tmc = _tile(nc, 512)
    sh1, sc1, g1, sh2, sc2, g2 = mods_l
    csh1, csc1, cg1, csh2, csc2, cg2 = mods_c
    cos_l, sin_l = _rope_tables(n)
    cos_c, sin_c = _no_rope_tables(nc)
    ql, kl, vtl = _mla_pre(x, sh1, sc1, norm1_g, cos_l, sin_l, w, tm)
    qc, kc, vtc = _mla_pre(ctx, csh1, csc1, norm1_g, cos_c, sin_c, w, tmc)
    k5 = lambda k, t: k.reshape(k.shape[0], k.shape[1], k.shape[2] // t, t, k.shape[3])
    src_l = (k5(kl, tm), vtl)
    src_c = (k5(kc, tmc), vtc)
    ol = _attention(ql, [src_l, src_c], _tile(n, 1024))
    x_new = _mla_ffn(x, ol, g1, sh2, sc2, g2, norm2_g, w_o, *ffn, tm)
    ctx_new = ctx
    if need_ctx:
        oc = _attention(qc, [src_c], _tile(nc, 1024))
        ctx_new = _mla_ffn(ctx, oc, cg1, csh2, csc2, cg2, norm2_g, w_o, *ffn, tmc)
    return x_new, ctx_new


def _rg_layer(x, ctx, mods_l, mods_c, norm1_g, norm2_g, rg, moe, need_ctx):
    b, n, d = x.shape
    nc = ctx.shape[1]
    tm = _tile(n, 512)
    tmc = _tile(nc, 512)
    tn = _tile(n, 256)
    tnc = _tile(nc, 256)
    sh1, sc1, g1, sh2, sc2, g2 = mods_l
    csh1, csc1, cg1, csh2, csc2, cg2 = mods_c
    w_in, w_gate, conv_w, conv_b, w_a, b_a, w_x, b_x, lam, w_out = rg
    w_router_pad, mw1, mw3, mw2, n_experts = moe
    ul, gll = _rg_in(x, sh1, sc1, norm1_g, w_in, w_gate, conv_w, conv_b, tm)
    uc, glc = _rg_in(ctx, csh1, csc1, norm1_g, w_in, w_gate, conv_w, conv_b, tmc)
    zero = jnp.zeros((b, w_in.shape[1] // LANES, LANES), F32)
    hl, hc = [], []
    for dr in range(2):
        args = (w_a[dr], b_a[dr], w_x[dr], b_x[dr], lam[dr])
        hcd, h_end = _rg_scan(uc, zero, *args, reverse=bool(dr), tn=tnc)
        hld, _ = _rg_scan(ul, h_end, *args, reverse=bool(dr), tn=tn)
        hl.append(hld)
        hc.append(hcd)
    hid = mw1.shape[2]
    hcn = hid // 4 if (hid // 4) % LANES == 0 else hid
    x1, h2, comb = _rg_out(x, hl[0], hl[1], gll, g1, sh2, sc2, norm2_g, w_out, w_router_pad, n_experts, tm)
    x_new = _moe(h2, comb, x1, g2, mw1, mw3, mw2, tm, hcn)
    ctx_new = ctx
    if need_ctx:
        c1, ch2, ccomb = _rg_out(ctx, hc[0], hc[1], glc, cg1, csh2, csc2, norm2_g, w_out, w_router_pad,
                                 n_experts, tmc)
        ctx_new = _moe(ch2, ccomb, c1, cg2, mw1, mw3, mw2, tmc, hcn)
    return x_new, ctx_new


def kernel(x, c, ctx, c_ctx, ada_w, ada_b, norm1_g, norm2_g, mla_w_dq, mla_g_q, mla_w_uq, mla_w_dkv, mla_g_kv, mla_w_ukv, mla_g_qn, mla_g_kn, mla_w_o, rg_w_in, rg_w_gate, rg_conv_w, rg_conv_b, rg_w_a, rg_b_a, rg_w_x, rg_b_x, rg_lam, rg_w_out, ffn_w1, ffn_w3, ffn_w2, moe_w_router, moe_w1, moe_w3, moe_w2):
    b, n, d = x.shape
    depth = ada_w.shape[0]
    n_experts = moe_w_router.shape[2]

    rows = 2 * SUBLANES
    cvec = jnp.concatenate([c, c_ctx[None, :], jnp.zeros((rows - b - 1, d), F32)], axis=0)
    mods = _ada_all(cvec, ada_w, ada_b)

    for i in range(depth):
        need_ctx = i < depth - 1
        j = i // 2
        chunks = [mods[i, :, k * d:(k + 1) * d] for k in range(6)]
        mods_l = [m[:b, None, :] for m in chunks]
        mods_c = [jnp.broadcast_to(m[b:b + 1, None, :], (b, 1, d)) for m in chunks]
        n1 = norm1_g[i][None, :]
        n2 = norm2_g[i][None, :]
        if i % 2 == 0:
            w = _prep_mla_weights(mla_w_dq[j], mla_g_q[j], mla_w_uq[j], mla_w_dkv[j], mla_g_kv[j],
                                  mla_w_ukv[j], mla_g_qn[j], mla_g_kn[j])
            ffn = (ffn_w1[j].astype(BF16), ffn_w3[j].astype(BF16), ffn_w2[j].astype(BF16))
            x, ctx = _mla_layer(x, ctx, mods_l, mods_c, n1, n2, w, mla_w_o[j].astype(BF16), ffn, need_ctx)
        else:
            rg = (rg_w_in[j].astype(BF16), rg_w_gate[j].astype(BF16), rg_conv_w[j], rg_conv_b[j][None, :],
                  rg_w_a[j].astype(BF16), rg_b_a[j][:, None, :], rg_w_x[j].astype(BF16), rg_b_x[j][:, None, :],
                  rg_lam[j][:, None, :], rg_w_out[j].astype(BF16))
            w_router_pad = jnp.concatenate(
                [moe_w_router[j], jnp.zeros((d, LANES - n_experts), F32)], axis=1)
            moe = (w_router_pad, moe_w1[j].astype(BF16), moe_w3[j].astype(BF16), moe_w2[j].astype(BF16), n_experts)
            x, ctx = _rg_layer(x, ctx, mods_l, mods_c, n1, n2, rg, moe, need_ctx)
    return x
```

```python
import functools
import math

import jax
import jax.numpy as jnp
from jax import lax
from jax.experimental import pallas as pl
from jax.experimental.pallas import tpu as pltpu

F32 = jnp.float32
BF16 = jnp.bfloat16

EPS = 1e-6
GRID_W = 64
N_HEADS = 8
D_NOPE = 128
D_ROPE = 64
D_V = 128
D_QK = D_NOPE + D_ROPE
D_HEAD_PAD = 256
Q_EXT = 384
ROPE_BASE = 10000.0
RG_BLOCKS = 4
RG_C = 8.0
TOP_K = 2
LANES = 128
SUBLANES = 8
VMEM_LIMIT = 56 * 1024 * 1024
LOG2E = 1.4426950408889634


def _cparams(sem):
    return pltpu.CompilerParams(dimension_semantics=sem, vmem_limit_bytes=VMEM_LIMIT)


def _resident(shape):
    nd = len(shape)
    return pl.BlockSpec(shape, lambda *_: (0,) * nd, pipeline_mode=pl.Buffered(1))


def _dot(a, b):
    return jnp.dot(a, b, preferred_element_type=F32)


def _rms(xf, g):
    return xf * lax.rsqrt(jnp.mean(xf * xf, axis=-1, keepdims=True) + EPS) * g


def _prenorm(xf, g, shift, scale):
    return _rms(xf, g) * (1.0 + scale) + shift


def _ada_kernel(c_ref, w_ref, b_ref, o_ref):
    s = jax.nn.silu(c_ref[...])
    o_ref[0] = jnp.dot(s, w_ref[0], precision=lax.Precision.HIGHEST,
                       preferred_element_type=F32) + b_ref[0]


def _ada_all(cvec, ada_w, ada_b):
    depth, d, n6 = ada_w.shape
    rows = cvec.shape[0]
    tn = 1536
    return pl.pallas_call(
        _ada_kernel,
        grid=(depth, n6 // tn),
        in_specs=[pl.BlockSpec((rows, d), lambda l, j: (0, 0)),
                  pl.BlockSpec((1, d, tn), lambda l, j: (l, 0, j)),
                  pl.BlockSpec((1, 1, tn), lambda l, j: (l, 0, j))],
        out_specs=pl.BlockSpec((1, rows, tn), lambda l, j: (l, 0, j)),
        out_shape=jax.ShapeDtypeStruct((depth, rows, n6), F32),
        compiler_params=_cparams(("arbitrary", "arbitrary")),
        name="ada_mod",
    )(cvec, ada_w, ada_b.reshape(depth, 1, n6))


def _mla_pre_kernel(x_ref, sh_ref, sc_ref, ng_ref, cos_ref, sin_ref,
                    wdq_ref, gq_ref, wuq_ref, gqn_ref,
                    wdkv_ref, gkv_ref, wuk_ref, wuvt_ref, gkn_ref,
                    q_ref, k_ref, vt_ref, *, q_scale):
    x = x_ref[0]
    h = _prenorm(x, ng_ref[...], sh_ref[0], sc_ref[0]).astype(BF16)
    cosv = cos_ref[...]
    sinv = sin_ref[...]

    qn = _rms(_dot(h, wdq_ref[...]), gq_ref[...]).astype(BF16)
    qall = _dot(qn, wuq_ref[...])
    g_n = gqn_ref[:, 0:LANES]
    g_a = gqn_ref[:, LANES:2 * LANES]
    g_b = gqn_ref[:, 2 * LANES:3 * LANES]
    for hh in range(N_HEADS):
        base = hh * Q_EXT
        nope = qall[:, base:base + LANES]
        ra = qall[:, base + LANES:base + 2 * LANES]
        rb = qall[:, base + 2 * LANES:base + 3 * LANES]
        ss = jnp.sum(nope * nope, axis=-1, keepdims=True) + jnp.sum(ra * ra, axis=-1, keepdims=True)
        inv = lax.rsqrt(ss * (1.0 / D_QK) + EPS) * q_scale
        q_ref[0, hh, :, 0:LANES] = (nope * g_n * inv).astype(BF16)
        q_ref[0, hh, :, LANES:2 * LANES] = ((ra * g_a * cosv + rb * g_b * sinv) * inv).astype(BF16)

    kva = _dot(h, wdkv_ref[...])
    ckv = _rms(kva[:, 0:LANES], gkv_ref[...]).astype(BF16)
    pa = kva[:, LANES:2 * LANES]
    pb = kva[:, 2 * LANES:3 * LANES]
    pe_ss = jnp.sum(pa * pa, axis=-1, keepdims=True)
    k_n = gkn_ref[:, 0:LANES]
    k_a = gkn_ref[:, LANES:2 * LANES]
    k_b = gkn_ref[:, 2 * LANES:3 * LANES]
    rope = pa * k_a * cosv + pb * k_b * sinv
    knope = _dot(ckv, wuk_ref[...])
    vt_all = lax.dot_general(wuvt_ref[...], ckv, (((1,), (1,)), ((), ())),
                             preferred_element_type=F32)
    for hh in range(N_HEADS):
        kn = knope[:, hh * LANES:(hh + 1) * LANES]
        ss = jnp.sum(kn * kn, axis=-1, keepdims=True) + pe_ss
        inv = lax.rsqrt(ss * (1.0 / D_QK) + EPS)
        k_ref[0, hh, :, 0:LANES] = (kn * k_n * inv).astype(BF16)
        k_ref[0, hh, :, LANES:2 * LANES] = (rope * inv).astype(BF16)
        vt_ref[0, hh, 0] = vt_all[hh * D_V:(hh + 1) * D_V, :].astype(BF16)


def _mla_pre(x, shift, scale, norm_g, cos_t, sin_t, w, tm):
    b, n, d = x.shape
    nt = n // tm
    q_scale = (D_QK ** -0.5) * LOG2E
    tok = lambda bi, i: (bi, i, 0)
    vec = lambda bi, i: (bi, 0, 0)
    out_shapes = (jax.ShapeDtypeStruct((b, N_HEADS, n, D_HEAD_PAD), BF16),
                  jax.ShapeDtypeStruct((b, N_HEADS, n, D_HEAD_PAD), BF16),
                  jax.ShapeDtypeStruct((b, N_HEADS, nt, D_V, tm), BF16))
    weights = (w["w_dq"], w["g_q"], w["w_uq"], w["g_qn"], w["w_dkv"], w["g_kv"], w["w_uk"], w["w_uvt"], w["g_kn"])
    return pl.pallas_call(
        functools.partial(_mla_pre_kernel, q_scale=q_scale),
        grid=(b, nt),
        in_specs=[pl.BlockSpec((1, tm, d), tok),
                  pl.BlockSpec((1, 1, d), vec), pl.BlockSpec((1, 1, d), vec),
                  _resident(norm_g.shape),
                  pl.BlockSpec((tm, LANES), lambda bi, i: (i, 0)),
                  pl.BlockSpec((tm, LANES), lambda bi, i: (i, 0))]
                 + [_resident(a.shape) for a in weights],
        out_specs=(pl.BlockSpec((1, N_HEADS, tm, D_HEAD_PAD), lambda bi, i: (bi, 0, i, 0)),
                   pl.BlockSpec((1, N_HEADS, tm, D_HEAD_PAD), lambda bi, i: (bi, 0, i, 0)),
                   pl.BlockSpec((1, N_HEADS, 1, D_V, tm), lambda bi, i: (bi, 0, i, 0, 0))),
        out_shape=out_shapes,
        compiler_params=_cparams(("arbitrary", "arbitrary")),
        name="mla_pre",
    )(x, shift, scale, norm_g, cos_t, sin_t, *weights)


def _attn_kernel(q_ref, *refs, n_src):
    srcs = [(refs[2 * s], refs[2 * s + 1]) for s in range(n_src)]
    o_ref = refs[2 * n_src]
    l_ref, acc_ref = refs[2 * n_src + 1:2 * n_src + 3]
    s_bufs = refs[2 * n_src + 3:]

    q_t = q_ref[0, 0].astype(F32).T.astype(BF16)
    l_ref[...] = jnp.zeros(l_ref.shape, F32)
    acc_ref[...] = jnp.zeros(acc_ref.shape, F32)

    def produce(kc, s_ref, m_prev):
        s = _dot(kc, q_t)
        s_ref[...] = s
        return jnp.maximum(m_prev, jnp.max(s, axis=0, keepdims=True))

    def consume(s_ref, vtc, m_cur, m_prev):
        alpha = jnp.exp2(m_prev - m_cur)
        p = jnp.exp2(s_ref[...] - m_cur)
        l_ref[...] = alpha * l_ref[...] + jnp.sum(p, axis=0, keepdims=True)
        acc_ref[...] = alpha * acc_ref[...] + _dot(vtc, p.astype(BF16))

    m_init = jnp.full(l_ref.shape, -1e30, F32)
    k_ref, vt_ref = srcs[0]
    n = k_ref.shape[2]
    if n == 1:
        m_cur = produce(k_ref[0, 0, 0], s_bufs[0], m_init)
        m_prev = m_init
        pending = (s_bufs[0], vt_ref, 0)
    else:
        s_a, s_b = s_bufs[0], s_bufs[1]
        m0 = produce(k_ref[0, 0, 0], s_a, m_init)

        def body(jj, carry):
            m_prev, m_cur = carry
            j = 2 * jj
            m_1 = produce(k_ref[0, 0, j + 1], s_b, m_cur)
            consume(s_a, vt_ref[0, 0, j], m_cur, m_prev)
            m_2 = produce(k_ref[0, 0, j + 2], s_a, m_1)
            consume(s_b, vt_ref[0, 0, j + 1], m_1, m_cur)
            return m_1, m_2

        m_prev, m_cur = lax.fori_loop(0, n // 2 - 1, body, (m_init, m0))
        m_1 = produce(k_ref[0, 0, n - 1], s_b, m_cur)
        consume(s_a, vt_ref[0, 0, n - 2], m_cur, m_prev)
        m_prev, m_cur = m_cur, m_1
        pending = (s_b, vt_ref, n - 1)

    for kx_ref, vtx_ref in srcs[1:]:
        m_x = produce(kx_ref[0, 0, 0], s_bufs[-1], m_cur)
        consume(pending[0], pending[1][0, 0, pending[2]], m_cur, m_prev)
        m_prev, m_cur = m_cur, m_x
        pending = (s_bufs[-1], vtx_ref, 0)
    consume(pending[0], pending[1][0, 0, pending[2]], m_cur, m_prev)

    o = acc_ref[...] * (1.0 / l_ref[...])
    o_ref[0] = o.T.astype(BF16)


def _attention(q, srcs, tq):
    b, h, nq, dp = q.shape
    in_specs = [pl.BlockSpec((1, 1, tq, dp), lambda bi, hi, i: (bi, hi, i, 0))]
    args = [q]
    for k5, vt5 in srcs:
        in_specs.append(pl.BlockSpec((1, 1) + k5.shape[2:], lambda bi, hi, i: (bi, hi, 0, 0, 0)))
        in_specs.append(pl.BlockSpec((1, 1) + vt5.shape[2:], lambda bi, hi, i: (bi, hi, 0, 0, 0)))
        args += [k5, vt5]
    n0, tk0 = srcs[0][0].shape[2:4]
    assert n0 == 1 or n0 % 2 == 0, n0
    assert all(k5.shape[2] == 1 for k5, _ in srcs[1:])
    s_shapes = [pltpu.VMEM((tk0, tq), F32)] * (1 if n0 == 1 else 2)
    s_shapes += [pltpu.VMEM((k5.shape[3], tq), F32) for k5, _ in srcs[1:2]]
    return pl.pallas_call(
        functools.partial(_attn_kernel, n_src=len(srcs)),
        grid=(b, h, nq // tq),
        in_specs=in_specs,
        out_specs=pl.BlockSpec((1, tq, D_V), lambda bi, hi, i: (bi, i, hi)),
        out_shape=jax.ShapeDtypeStruct((b, nq, h * D_V), BF16),
        scratch_shapes=[pltpu.VMEM((1, tq), F32), pltpu.VMEM((D_V, tq), F32)] + s_shapes,
        compiler_params=_cparams(("arbitrary", "arbitrary", "arbitrary")),
        name="mla_attn",
    )(*args)


def _mla_ffn_kernel(x_ref, o_ref, g1_ref, sh_ref, sc_ref, g2_ref, ng_ref,
                    wo_ref, w1_ref, w3_ref, w2_ref, out_ref, *, n_chunks):
    x1 = x_ref[0] + g1_ref[0] * _dot(o_ref[0], wo_ref[...])
    h2 = _prenorm(x1, ng_ref[...], sh_ref[0], sc_ref[0]).astype(BF16)
    hid = w1_ref.shape[1]
    hc = hid // n_chunks
    y = None
    for c in range(n_chunks):
        a = _dot(h2, w1_ref[:, c * hc:(c + 1) * hc])
        g = _dot(h2, w3_ref[:, c * hc:(c + 1) * hc])
        act = (jax.nn.silu(a) * g).astype(BF16)
        yc = _dot(act, w2_ref[c * hc:(c + 1) * hc, :])
        y = yc if y is None else y + yc
    out_ref[0] = x1 + g2_ref[0] * y


def _mla_ffn(x, o, gate1, shift2, scale2, gate2, norm_g, w_o, w1, w3, w2, tm):
    b, n, d = x.shape
    hid = w1.shape[1]
    n_chunks = 2 if (hid // 2) % LANES == 0 else 1
    tok = lambda bi, i: (bi, i, 0)
    vec = lambda bi, i: (bi, 0, 0)
    return pl.pallas_call(
        functools.partial(_mla_ffn_kernel, n_chunks=n_chunks),
        grid=(b, n // tm),
        in_specs=[pl.BlockSpec((1, tm, d), tok), pl.BlockSpec((1, tm, o.shape[2]), tok),
                  pl.BlockSpec((1, 1, d), vec), pl.BlockSpec((1, 1, d), vec),
                  pl.BlockSpec((1, 1, d), vec), pl.BlockSpec((1, 1, d), vec),
                  _resident(norm_g.shape), _resident(w_o.shape),
                  _resident(w1.shape), _resident(w3.shape), _resident(w2.shape)],
        out_specs=pl.BlockSpec((1, tm, d), tok),
        out_shape=jax.ShapeDtypeStruct((b, n, d), F32),
        compiler_params=_cparams(("arbitrary", "arbitrary")),
        name="mla_ffn",
    )(x, o, gate1, shift2, scale2, gate2, norm_g, w_o, w1, w3, w2)


def _rg_in_kernel(xp_ref, xm_ref, xn_ref, sh_ref, sc_ref, ng_ref, win_ref, wg_ref, cw_ref, cb_ref,
                  u_ref, gl_ref, *, tm):
    i = pl.program_id(1)
    nt = pl.num_programs(1)
    x_ext = jnp.concatenate([xp_ref[0], xm_ref[0], xn_ref[0]], axis=0)
    h = _prenorm(x_ext, ng_ref[...], sh_ref[0], sc_ref[0]).astype(BF16)
    u_ext = _dot(h, win_ref[...])
    row = lax.broadcasted_iota(jnp.int32, (tm + 2 * SUBLANES, 1), 0)
    valid = jnp.logical_and(jnp.logical_or(row >= SUBLANES, i > 0),
                            jnp.logical_or(row < tm + SUBLANES, i < nt - 1))
    u_ext = jnp.where(valid, u_ext, 0.0)
    acc = cb_ref[...] + cw_ref[0:1, :] * u_ext[6:6 + tm]
    for k in range(1, 4):
        acc = acc + cw_ref[k:k + 1, :] * u_ext[6 + k:6 + k + tm]
    u_ref[0] = acc.astype(BF16)
    gl_ref[0] = jax.nn.gelu(_dot(h[SUBLANES:SUBLANES + tm], wg_ref[...])).astype(BF16)


def _rg_in(x, shift, scale, norm_g, w_in, w_gate, conv_w, conv_b, tm):
    b, n, d = x.shape
    c = w_in.shape[1]
    r = tm // SUBLANES
    last = n // SUBLANES - 1
    vec = lambda bi, i: (bi, 0, 0)
    tok = lambda bi, i: (bi, i, 0)
    return pl.pallas_call(
        functools.partial(_rg_in_kernel, tm=tm),
        grid=(b, n // tm),
        in_specs=[pl.BlockSpec((1, SUBLANES, d), lambda bi, i: (bi, jnp.maximum(i * r - 1, 0), 0)),
                  pl.BlockSpec((1, tm, d), tok),
                  pl.BlockSpec((1, SUBLANES, d), lambda bi, i: (bi, jnp.minimum((i + 1) * r, last), 0)),
                  pl.BlockSpec((1, 1, d), vec), pl.BlockSpec((1, 1, d), vec),
                  _resident(norm_g.shape), _resident(w_in.shape), _resident(w_gate.shape),
                  _resident(conv_w.shape), _resident(conv_b.shape)],
        out_specs=(pl.BlockSpec((1, tm, c), tok), pl.BlockSpec((1, tm, c), tok)),
        out_shape=(jax.ShapeDtypeStruct((b, n, c), BF16), jax.ShapeDtypeStruct((b, n, c), BF16)),
        compiler_params=_cparams(("arbitrary", "arbitrary")),
        name="rg_in",
    )(x, x, x, shift, scale, norm_g, w_in, w_gate, conv_w, conv_b)


def _rg_scan_kernel(u_ref, h0_ref, wa_ref, ba_ref, wx_ref, bx_ref, lam_ref,
                    hout_ref, hfin_ref, a_s, b_s, o_s, h_s, *, reverse, tn, pitch):
    @pl.when(pl.program_id(1) == 0)
    def _():
        h_s[...] = h0_ref[0]

    u = u_ref[0]
    width = u.shape[1]
    bw = width // RG_BLOCKS
    for blk in range(RG_BLOCKS):
        cs = slice(blk * bw, (blk + 1) * bw)
        ub = u[:, cs]
        r = jax.nn.sigmoid(_dot(ub, wa_ref[blk]) + ba_ref[:, cs])
        ig = jax.nn.sigmoid(_dot(ub, wx_ref[blk]) + bx_ref[:, cs])
        log_a = -RG_C * r * jax.nn.softplus(-lam_ref[:, cs])
        a = jnp.exp(log_a)
        bb = jnp.sqrt(1.0 - a * a) * (ig * ub.astype(F32))
        for half in range(bw // LANES):
            j = blk * (bw // LANES) + half
            a_s[j * pitch:j * pitch + tn, :] = a[:, half * LANES:(half + 1) * LANES]
            b_s[j * pitch:j * pitch + tn, :] = bb[:, half * LANES:(half + 1) * LANES]

    n_ct = width // LANES

    def body(g, h):
        for s in range(SUBLANES):
            t = g * SUBLANES + s
            if reverse:
                t = tn - 1 - t
            idx = pl.ds(t, n_ct, stride=pitch)
            h = a_s[idx, :] * h + b_s[idx, :]
            o_s[idx, :] = h
        return h

    h = lax.fori_loop(0, tn // SUBLANES, body, h_s[...])
    h_s[...] = h
    hfin_ref[0] = h
    for j in range(n_ct):
        hout_ref[0, :, j * LANES:(j + 1) * LANES] = o_s[j * pitch:j * pitch + tn, :].astype(BF16)


def _rg_scan(u, h0, w_a, b_a, w_x, b_x, lam, reverse, tn):
    b, n, c = u.shape
    nt = n // tn
    n_ct = c // LANES
    pitch = tn + SUBLANES
    tmap = (lambda bi, i: (bi, nt - 1 - i, 0)) if reverse else (lambda bi, i: (bi, i, 0))
    st = lambda bi, i: (bi, 0, 0)
    scr = pltpu.VMEM((n_ct * pitch, LANES), F32)
    return pl.pallas_call(
        functools.partial(_rg_scan_kernel, reverse=reverse, tn=tn, pitch=pitch),
        grid=(b, nt),
        in_specs=[pl.BlockSpec((1, tn, c), tmap), pl.BlockSpec((1, n_ct, LANES), st),
                  _resident(w_a.shape), _resident(b_a.shape), _resident(w_x.shape),
                  _resident(b_x.shape), _resident(lam.shape)],
        out_specs=(pl.BlockSpec((1, tn, c), tmap), pl.BlockSpec((1, n_ct, LANES), st)),
        out_shape=(jax.ShapeDtypeStruct((b, n, c), BF16), jax.ShapeDtypeStruct((b, n_ct, LANES), F32)),
        scratch_shapes=[scr, scr, scr, pltpu.VMEM((n_ct, LANES), F32)],
        compiler_params=_cparams(("arbitrary", "arbitrary")),
        name="rg_scan_bwd" if reverse else "rg_scan_fwd",
    )(u, h0, w_a, b_a, w_x, b_x, lam)


def _rg_out_kernel(x_ref, hf_ref, hb_ref, gl_ref, g1_ref, sh_ref, sc_ref, ng_ref, wout_ref, wr_ref,
                   x1_ref, h2_ref, route_ref, *, n_experts):
    y = ((hf_ref[0].astype(F32) + hb_ref[0].astype(F32)) * gl_ref[0].astype(F32)).astype(BF16)
    x1 = x_ref[0] + g1_ref[0] * _dot(y, wout_ref[...])
    x1_ref[0] = x1
    h2 = _prenorm(x1, ng_ref[...], sh_ref[0], sc_ref[0])
    h2_ref[0] = h2
    logits = jnp.dot(h2, wr_ref[...], precision=lax.Precision.HIGHEST, preferred_element_type=F32)
    lane = lax.broadcasted_iota(jnp.int32, logits.shape, 1).astype(F32)
    neg = jnp.float32(-jnp.inf)
    lg = jnp.where(lane < n_experts, logits, neg)
    m1 = jnp.max(lg, axis=-1, keepdims=True)
    i1 = jnp.min(jnp.where(lg == m1, lane, float(LANES)), axis=-1, keepdims=True)
    lg2 = jnp.where(lane == i1, neg, lg)
    m2 = jnp.max(lg2, axis=-1, keepdims=True)
    i2 = jnp.min(jnp.where(lg2 == m2, lane, float(LANES)), axis=-1, keepdims=True)
    e2 = jnp.exp(m2 - m1)
    den = 1.0 + e2
    route_ref[0] = (jnp.where(lane == 0.0, i1, 0.0) + jnp.where(lane == 1.0, i2, 0.0)
                    + jnp.where(lane == 2.0, 1.0 / den, 0.0) + jnp.where(lane == 3.0, e2 / den, 0.0))


def _rg_out(x, hf, hb, gl, gate1, shift2, scale2, norm_g, w_out, w_router_pad, n_experts, tm):
    b, n, d = x.shape
    c = hf.shape[2]
    tok = lambda bi, i: (bi, i, 0)
    vec = lambda bi, i: (bi, 0, 0)
    return pl.pallas_call(
        functools.partial(_rg_out_kernel, n_experts=n_experts),
        grid=(b, n // tm),
        in_specs=[pl.BlockSpec((1, tm, d), tok), pl.BlockSpec((1, tm, c), tok), pl.BlockSpec((1, tm, c), tok),
                  pl.BlockSpec((1, tm, c), tok),
                  pl.BlockSpec((1, 1, d), vec), pl.BlockSpec((1, 1, d), vec), pl.BlockSpec((1, 1, d), vec),
                  _resident(norm_g.shape), _resident(w_out.shape), _resident(w_router_pad.shape)],
        out_specs=(pl.BlockSpec((1, tm, d), tok), pl.BlockSpec((1, tm, d), tok),
                   pl.BlockSpec((1, tm, LANES), tok)),
        out_shape=(jax.ShapeDtypeStruct((b, n, d), F32), jax.ShapeDtypeStruct((b, n, d), F32),
                   jax.ShapeDtypeStruct((b, n, LANES), F32)),
        compiler_params=_cparams(("arbitrary", "arbitrary")),
        name="rg_out_router",
    )(x, hf, hb, gl, gate1, shift2, scale2, norm_g, w_out, w_router_pad)


def _route_tables(e_idx, tm, n_experts):
    t = e_idx.shape[0]
    n_slots = TOP_K * t
    n_tiles = n_slots // tm + n_experts
    e_flat = e_idx.T.reshape(n_slots)
    order = jnp.argsort(e_flat, stable=True).astype(jnp.int32)
    counts = jnp.sum((e_flat[:, None] == jnp.arange(n_experts, dtype=jnp.int32)[None, :]).astype(jnp.int32), axis=0)
    padded = (counts + tm - 1) // tm * tm
    ends = jnp.cumsum(padded)
    off = ends - padded
    uoff = jnp.cumsum(counts) - counts
    n_used = (ends[-1] // tm).astype(jnp.int32)
    tile_ids = jnp.arange(n_tiles, dtype=jnp.int32)
    tile_e = jnp.sum((tile_ids[:, None] * tm >= ends[None, :]).astype(jnp.int32), axis=1)
    last_e = jnp.max(jnp.where(tile_ids < n_used, tile_e, 0))
    tile_e = jnp.where(tile_ids < n_used, tile_e, last_e)
    p = jnp.arange(n_tiles * tm, dtype=jnp.int32)
    tile_p = p // tm
    e_p = tile_e[tile_p]
    rank = p - off[e_p]
    valid = jnp.logical_and(rank < counts[e_p], tile_p < n_used)
    src = order[jnp.clip(uoff[e_p] + rank, 0, n_slots - 1)]
    tok = jnp.where(valid, src % t, 0)
    dst = jnp.where(valid, src, n_slots + (tile_p % 2) * tm + p % tm)
    return (tile_e, n_used.reshape(1), tok.reshape(n_tiles, 1, tm), dst.reshape(n_tiles, 1, tm))


def _moe_group_kernel(te_ref, nu_ref, tokc_ref, tokn_ref, dst_ref, h2_hbm, w1_ref, w3_ref, w2_ref, y_hbm,
                      xbuf, ybuf, gsem, ssem, *, tm, n_chunks):
    i = pl.program_id(0)
    n_used = nu_ref[0]
    slot = lax.rem(i, 2)

    def gather(tok_ref, s):
        def body(r, carry):
            pltpu.make_async_copy(h2_hbm.at[pl.ds(tok_ref[0, 0, r], 1)], xbuf.at[s, pl.ds(r, 1)],
                                  gsem.at[s]).start()
            return carry
        lax.fori_loop(0, tm, body, 0, unroll=8)

    def wait_gather(s):
        pltpu.make_async_copy(h2_hbm.at[pl.ds(0, tm)], xbuf.at[s], gsem.at[s]).wait()

    def wait_scatter(s):
        pltpu.make_async_copy(ybuf.at[s], y_hbm.at[pl.ds(0, tm)], ssem.at[s]).wait()

    @pl.when(i == 0)
    def _():
        ybuf[...] = jnp.zeros(ybuf.shape, F32)
        tail = y_hbm.shape[0] - 2 * tm
        for s in range(2):
            cp = pltpu.make_async_copy(ybuf.at[s], y_hbm.at[pl.ds(tail + s * tm, tm)], ssem.at[s])
            cp.start()
            cp.wait()

    @pl.when(jnp.logical_and(i == 0, n_used > 0))
    def _():
        gather(tokc_ref, 0)

    @pl.when(i + 1 < n_used)
    def _():
        gather(tokn_ref, 1 - slot)

    @pl.when(i < n_used)
    def _():
        wait_gather(slot)

        @pl.when(i >= 2)
        def _():
            wait_scatter(slot)

        x = xbuf[slot].astype(BF16)
        hid = w1_ref.shape[2]
        hc = hid // n_chunks
        y = None
        for c in range(n_chunks):
            a = _dot(x, w1_ref[0, :, c * hc:(c + 1) * hc])
            g = _dot(x, w3_ref[0, :, c * hc:(c + 1) * hc])
            act = (jax.nn.silu(a) * g).astype(BF16)
            yc = _dot(act, w2_ref[0, c * hc:(c + 1) * hc, :])
            y = yc if y is None else y + yc
        ybuf[slot] = y

        def body(r, carry):
            pltpu.make_async_copy(ybuf.at[slot, pl.ds(r, 1)], y_hbm.at[pl.ds(dst_ref[0, 0, r], 1)],
                                  ssem.at[slot]).start()
            return carry
        lax.fori_loop(0, tm, body, 0, unroll=8)

    @pl.when(i == pl.num_programs(0) - 1)
    def _():
        @pl.when(n_used >= 1)
        def _():
            wait_scatter(lax.rem(n_used - 1, 2))

        @pl.when(n_used >= 2)
        def _():
            wait_scatter(lax.rem(n_used, 2))


def _moe_group(h2, e_idx, w1, w3, w2, tm):
    t, d = h2.shape
    n_e, _, hid = w1.shape
    tile_e, n_used, tok, dst = _route_tables(e_idx, tm, n_e)
    n_tiles = tile_e.shape[0]
    n_chunks = 4 if (hid // 4) % LANES == 0 else 1
    smem_blk = lambda f: pl.BlockSpec((1, 1, tm), f, memory_space=pltpu.SMEM)
    wspec = lambda shape: pl.BlockSpec((1,) + shape, lambda i, te, nu: (te[i], 0, 0), pipeline_mode=pl.Buffered(1))
    grid_spec = pltpu.PrefetchScalarGridSpec(
        num_scalar_prefetch=2,
        grid=(n_tiles,),
        in_specs=[smem_blk(lambda i, te, nu: (i, 0, 0)),
                  smem_blk(lambda i, te, nu: (jnp.minimum(i + 1, n_tiles - 1), 0, 0)),
                  smem_blk(lambda i, te, nu: (i, 0, 0)),
                  pl.BlockSpec(memory_space=pl.ANY),
                  wspec((d, hid)), wspec((d, hid)), wspec((hid, d))],
        out_specs=pl.BlockSpec(memory_space=pl.ANY),
        scratch_shapes=[pltpu.VMEM((2, tm, d), F32), pltpu.VMEM((2, tm, d), F32),
                        pltpu.SemaphoreType.DMA((2,)), pltpu.SemaphoreType.DMA((2,))],
    )
    return pl.pallas_call(
        functools.partial(_moe_group_kernel, tm=tm, n_chunks=n_chunks),
        grid_spec=grid_spec,
        out_shape=jax.ShapeDtypeStruct((TOP_K * t + 2 * tm, d), F32),
        compiler_params=_cparams(("arbitrary",)),
        name="moe_group",
    )(tile_e, n_used, tok, tok, dst, h2, w1, w3, w2)


def _moe_combine_kernel(x1_ref, y0_ref, y1_ref, route_ref, g2_ref, out_ref):
    r = route_ref[0]
    out_ref[0] = x1_ref[0] + g2_ref[0] * (r[:, 2:3] * y0_ref[...] + r[:, 3:4] * y1_ref[...])


def _moe_combine(x1, y, route, gate2, tm):
    b, n, d = x1.shape
    nt = n // tm
    t_blocks = b * nt
    tok = lambda bi, i: (bi, i, 0)
    return pl.pallas_call(
        _moe_combine_kernel,
        grid=(b, nt),
        in_specs=[pl.BlockSpec((1, tm, d), tok),
                  pl.BlockSpec((tm, d), lambda bi, i: (bi * nt + i, 0)),
                  pl.BlockSpec((tm, d), lambda bi, i: (t_blocks + bi * nt + i, 0)),
                  pl.BlockSpec((1, tm, LANES), tok),
                  pl.BlockSpec((1, 1, d), lambda bi, i: (bi, 0, 0))],
        out_specs=pl.BlockSpec((1, tm, d), tok),
        out_shape=jax.ShapeDtypeStruct((b, n, d), F32),
        compiler_params=_cparams(("arbitrary", "arbitrary")),
        name="moe_combine",
    )(x1, y, y, route, gate2)


def _moe(h2, route, x1, gate2, w1, w3, w2, tm):
    b, n, d = x1.shape
    e_idx = route.reshape(b * n, LANES)[:, :TOP_K].astype(jnp.int32)
    y = _moe_group(h2.reshape(b * n, d), e_idx, w1, w3, w2, tm)
    return _moe_combine(x1, y, route, gate2, tm)


def _prep_mla_weights(w_dq, g_q, w_uq, w_dkv, g_kv, w_ukv, g_qn, g_kn):
    ql = w_uq.shape[0]
    half = D_ROPE // 2
    pad = LANES - D_ROPE

    def ext_cols(w_rope):
        z = jnp.zeros(w_rope.shape[:-1] + (pad,), w_rope.dtype)
        r1, r2 = w_rope[..., :half], w_rope[..., half:]
        return jnp.concatenate([w_rope, z], -1), jnp.concatenate([-r2, r1, z], -1)

    def ext_gain(g):
        z = jnp.zeros((pad,), g.dtype)
        gr1, gr2 = g[D_NOPE:D_NOPE + half], g[D_NOPE + half:]
        return jnp.concatenate([g[:D_NOPE], gr1, gr2, z, gr2, gr1, z])[None, :].astype(F32)

    wq = w_uq.reshape(ql, N_HEADS, D_QK)
    qa, qb = ext_cols(wq[:, :, D_NOPE:])
    w_uq_ext = jnp.concatenate([wq[:, :, :D_NOPE], qa, qb], -1).reshape(ql, N_HEADS * Q_EXT)
    kl = w_ukv.shape[0]
    ka, kb = ext_cols(w_dkv[:, kl:])
    w_dkv_ext = jnp.concatenate([w_dkv[:, :kl], ka, kb], -1)
    wkv = w_ukv.reshape(kl, N_HEADS, D_NOPE + D_V)
    w_uk = wkv[:, :, :D_NOPE].reshape(kl, N_HEADS * D_NOPE)
    w_uvt = wkv[:, :, D_NOPE:].reshape(kl, N_HEADS * D_V).T
    return dict(w_dq=w_dq.astype(BF16), g_q=g_q[None, :].astype(F32), w_uq=w_uq_ext.astype(BF16),
                g_qn=ext_gain(g_qn), w_dkv=w_dkv_ext.astype(BF16), g_kv=g_kv[None, :].astype(F32),
                w_uk=w_uk.astype(BF16), w_uvt=w_uvt.astype(BF16), g_kn=ext_gain(g_kn))


def _rope_tables(n_tok):
    rows = n_tok // GRID_W
    row = jnp.repeat(jnp.arange(rows, dtype=F32), GRID_W)
    col = jnp.tile(jnp.arange(GRID_W, dtype=F32), rows)
    n_freq = D_ROPE // 4
    inv = ROPE_BASE ** (-jnp.arange(n_freq, dtype=F32) / n_freq)
    ang = jnp.concatenate([row[:, None] * inv, col[:, None] * inv], axis=-1)
    z = jnp.zeros((n_tok, LANES - D_ROPE), F32)
    cos, sin = jnp.cos(ang), jnp.sin(ang)
    return jnp.concatenate([cos, cos, z], -1), jnp.concatenate([sin, sin, z], -1)


def _no_rope_tables(n_tok):
    one = jnp.ones((n_tok, D_ROPE), F32)
    z = jnp.zeros((n_tok, LANES - D_ROPE), F32)
    return jnp.concatenate([one, z], -1), jnp.zeros((n_tok, LANES), F32)


def _tile(n, target):
    t = min(n, target)
    assert n % t == 0, (n, t)
    return t


def _mla_layer(x, ctx, mods_l, mods_c, norm1_g, norm2_g, w, w_o, ffn, need_ctx):
    b, n, d = x.shape
    nc = ctx.shape[1]
    tm = _tile(n, 512)
    tmc = _tile(nc, 512)
    sh1, sc1, g1, sh2, sc2, g2 = mods_l
    csh1, csc1, cg1, csh2, csc2, cg2 = mods_c
    cos_l, sin_l = _rope_tables(n)
    cos_c, sin_c = _no_rope_tables(nc)
    ql, kl, vtl = _mla_pre(x, sh1, sc1, norm1_g, cos_l, sin_l, w, tm)
    qc, kc, vtc = _mla_pre(ctx, csh1, csc1, norm1_g, cos_c, sin_c, w, tmc)
    k5 = lambda k, t: k.reshape(k.shape[0], k.shape[1], k.shape[2] // t, t, k.shape[3])
    src_l = (k5(kl, tm), vtl)
    src_c = (k5(kc, tmc), vtc)
    ol = _attention(ql, [src_l, src_c], _tile(n, 1024))
    x_new = _mla_ffn(x, ol, g1, sh2, sc2, g2, norm2_g, w_o, *ffn, tm)
    ctx_new = ctx
    if need_ctx:
        oc = _attention(qc, [src_c], _tile(nc, 1024))
        ctx_new = _mla_ffn(ctx, oc, cg1, csh2, csc2, cg2, norm2_g, w_o, *ffn, tmc)
    return x_new, ctx_new


def _rg_layer(x, ctx, mods_l, mods_c, norm1_g, norm2_g, rg, moe, need_ctx):
    b, n, d = x.shape
    nc = ctx.shape[1]
    tm = _tile(n, 512)
    tmc = _tile(nc, 512)
    tn = _tile(n, 256)
    tnc = _tile(nc, 256)
    sh1, sc1, g1, sh2, sc2, g2 = mods_l
    csh1, csc1, cg1, csh2, csc2, cg2 = mods_c
    w_in, w_gate, conv_w, conv_b, w_a, b_a, w_x, b_x, lam, w_out = rg
    w_router_pad, mw1, mw3, mw2, n_experts = moe
    ul, gll = _rg_in(x, sh1, sc1, norm1_g, w_in, w_gate, conv_w, conv_b, tm)
    uc, glc = _rg_in(ctx, csh1, csc1, norm1_g, w_in, w_gate, conv_w, conv_b, tmc)
    zero = jnp.zeros((b, w_in.shape[1] // LANES, LANES), F32)
    hl, hc = [], []
    for dr in range(2):
        args = (w_a[dr], b_a[dr], w_x[dr], b_x[dr], lam[dr])
        hcd, h_end = _rg_scan(uc, zero, *args, reverse=bool(dr), tn=tnc)
        hld, _ = _rg_scan(ul, h_end, *args, reverse=bool(dr), tn=tn)
        hl.append(hld)
        hc.append(hcd)
    x1, h2, route = _rg_out(x, hl[0], hl[1], gll, g1, sh2, sc2, norm2_g, w_out, w_router_pad, n_experts, tm)
    x_new = _moe(h2, route, x1, g2, mw1, mw3, mw2, tm)
    ctx_new = ctx
    if need_ctx:
        c1, ch2, croute = _rg_out(ctx, hc[0], hc[1], glc, cg1, csh2, csc2, norm2_g, w_out, w_router_pad,
                                  n_experts, tmc)
        ctx_new = _moe(ch2, croute, c1, cg2, mw1, mw3, mw2, tmc)
    return x_new, ctx_new


def kernel(x, c, ctx, c_ctx, ada_w, ada_b, norm1_g, norm2_g, mla_w_dq, mla_g_q, mla_w_uq, mla_w_dkv, mla_g_kv, mla_w_ukv, mla_g_qn, mla_g_kn, mla_w_o, rg_w_in, rg_w_gate, rg_conv_w, rg_conv_b, rg_w_a, rg_b_a, rg_w_x, rg_b_x, rg_lam, rg_w_out, ffn_w1, ffn_w3, ffn_w2, moe_w_router, moe_w1, moe_w3, moe_w2):
    b, n, d = x.shape
    depth = ada_w.shape[0]
    n_experts = moe_w_router.shape[2]

    rows = 2 * SUBLANES
    cvec = jnp.concatenate([c, c_ctx[None, :], jnp.zeros((rows - b - 1, d), F32)], axis=0)
    mods = _ada_all(cvec, ada_w, ada_b)

    for i in range(depth):
        need_ctx = i < depth - 1
        j = i // 2
        chunks = [mods[i, :, k * d:(k + 1) * d] for k in range(6)]
        mods_l = [m[:b, None, :] for m in chunks]
        mods_c = [jnp.broadcast_to(m[b:b + 1, None, :], (b, 1, d)) for m in chunks]
        n1 = norm1_g[i][None, :]
        n2 = norm2_g[i][None, :]
        if i % 2 == 0:
            w = _prep_mla_weights(mla_w_dq[j], mla_g_q[j], mla_w_uq[j], mla_w_dkv[j], mla_g_kv[j],
                                  mla_w_ukv[j], mla_g_qn[j], mla_g_kn[j])
            ffn = (ffn_w1[j].astype(BF16), ffn_w3[j].astype(BF16), ffn_w2[j].astype(BF16))
            x, ctx = _mla_layer(x, ctx, mods_l, mods_c, n1, n2, w, mla_w_o[j].astype(BF16), ffn, need_ctx)
        else:
            rg = (rg_w_in[j].astype(BF16), rg_w_gate[j].astype(BF16), rg_conv_w[j], rg_conv_b[j][None, :],
                  rg_w_a[j].astype(BF16), rg_b_a[j][:, None, :], rg_w_x[j].astype(BF16), rg_b_x[j][:, None, :],
                  rg_lam[j][:, None, :], rg_w_out[j].astype(BF16))
            w_router_pad = jnp.concatenate(
                [moe_w_router[j], jnp.zeros((d, LANES - n_experts), F32)], axis=1)
            moe = (w_router_pad, moe_w1[j].astype(BF16), moe_w3[j].astype(BF16), moe_w2[j].astype(BF16), n_experts)
            x, ctx = _rg_layer(x, ctx, mods_l, mods_c, n1, n2, rg, moe, need_ctx)
    return x
```

```python
import functools
import math

import jax
import jax.numpy as jnp
from jax import lax
from jax.experimental import pallas as pl
from jax.experimental.pallas import tpu as pltpu

F32 = jnp.float32
BF16 = jnp.bfloat16

EPS = 1e-6
GRID_W = 64
N_HEADS = 8
D_NOPE = 128
D_ROPE = 64
D_V = 128
D_QK = D_NOPE + D_ROPE
D_HEAD_PAD = 256
Q_EXT = 384
ROPE_BASE = 10000.0
RG_BLOCKS = 4
RG_C = 8.0
TOP_K = 2
LANES = 128
SUBLANES = 8
VMEM_LIMIT = 56 * 1024 * 1024
LOG2E = 1.4426950408889634


def _cparams(sem):
    return pltpu.CompilerParams(dimension_semantics=sem, vmem_limit_bytes=VMEM_LIMIT)


def _resident(shape):
    nd = len(shape)
    return pl.BlockSpec(shape, lambda *_: (0,) * nd, pipeline_mode=pl.Buffered(1))


def _dot(a, b):
    return jnp.dot(a, b, preferred_element_type=F32)


def _rms(xf, g):
    return xf * lax.rsqrt(jnp.mean(xf * xf, axis=-1, keepdims=True) + EPS) * g


def _prenorm(xf, g, shift, scale):
    return _rms(xf, g) * (1.0 + scale) + shift


def _ada_kernel(c_ref, w_ref, b_ref, o_ref):
    s = jax.nn.silu(c_ref[...])
    o_ref[0] = jnp.dot(s, w_ref[0], precision=lax.Precision.HIGHEST,
                       preferred_element_type=F32) + b_ref[0]


def _ada_all(cvec, ada_w, ada_b):
    depth, d, n6 = ada_w.shape
    rows = cvec.shape[0]
    tn = 1536
    return pl.pallas_call(
        _ada_kernel,
        grid=(depth, n6 // tn),
        in_specs=[pl.BlockSpec((rows, d), lambda l, j: (0, 0)),
                  pl.BlockSpec((1, d, tn), lambda l, j: (l, 0, j)),
                  pl.BlockSpec((1, 1, tn), lambda l, j: (l, 0, j))],
        out_specs=pl.BlockSpec((1, rows, tn), lambda l, j: (l, 0, j)),
        out_shape=jax.ShapeDtypeStruct((depth, rows, n6), F32),
        compiler_params=_cparams(("arbitrary", "arbitrary")),
        name="ada_mod",
    )(cvec, ada_w, ada_b.reshape(depth, 1, n6))


def _mla_pre_kernel(x_ref, sh_ref, sc_ref, ng_ref, cos_ref, sin_ref,
                    wdq_ref, gq_ref, wuq_ref, gqn_ref,
                    wdkv_ref, gkv_ref, wuk_ref, wuvt_ref, gkn_ref,
                    q_ref, k_ref, vt_ref, *, q_scale):
    x = x_ref[0]
    h = _prenorm(x, ng_ref[...], sh_ref[0], sc_ref[0]).astype(BF16)
    cosv = cos_ref[...]
    sinv = sin_ref[...]

    qn = _rms(_dot(h, wdq_ref[...]), gq_ref[...]).astype(BF16)
    qall = _dot(qn, wuq_ref[...])
    g_n = gqn_ref[:, 0:LANES]
    g_a = gqn_ref[:, LANES:2 * LANES]
    g_b = gqn_ref[:, 2 * LANES:3 * LANES]
    for hh in range(N_HEADS):
        base = hh * Q_EXT
        nope = qall[:, base:base + LANES]
        ra = qall[:, base + LANES:base + 2 * LANES]
        rb = qall[:, base + 2 * LANES:base + 3 * LANES]
        ss = jnp.sum(nope * nope, axis=-1, keepdims=True) + jnp.sum(ra * ra, axis=-1, keepdims=True)
        inv = lax.rsqrt(ss * (1.0 / D_QK) + EPS) * q_scale
        q_ref[0, hh, :, 0:LANES] = (nope * g_n * inv).astype(BF16)
        q_ref[0, hh, :, LANES:2 * LANES] = ((ra * g_a * cosv + rb * g_b * sinv) * inv).astype(BF16)

    kva = _dot(h, wdkv_ref[...])
    ckv = _rms(kva[:, 0:LANES], gkv_ref[...]).astype(BF16)
    pa = kva[:, LANES:2 * LANES]
    pb = kva[:, 2 * LANES:3 * LANES]
    pe_ss = jnp.sum(pa * pa, axis=-1, keepdims=True)
    k_n = gkn_ref[:, 0:LANES]
    k_a = gkn_ref[:, LANES:2 * LANES]
    k_b = gkn_ref[:, 2 * LANES:3 * LANES]
    rope = pa * k_a * cosv + pb * k_b * sinv
    knope = _dot(ckv, wuk_ref[...])
    vt_all = lax.dot_general(wuvt_ref[...], ckv, (((1,), (1,)), ((), ())),
                             preferred_element_type=F32)
    for hh in range(N_HEADS):
        kn = knope[:, hh * LANES:(hh + 1) * LANES]
        ss = jnp.sum(kn * kn, axis=-1, keepdims=True) + pe_ss
        inv = lax.rsqrt(ss * (1.0 / D_QK) + EPS)
        k_ref[0, hh, :, 0:LANES] = (kn * k_n * inv).astype(BF16)
        k_ref[0, hh, :, LANES:2 * LANES] = (rope * inv).astype(BF16)
        vt_ref[0, hh, 0] = vt_all[hh * D_V:(hh + 1) * D_V, :].astype(BF16)


def _mla_pre(x, shift, scale, norm_g, cos_t, sin_t, w, tm):
    b, n, d = x.shape
    nt = n // tm
    q_scale = (D_QK ** -0.5) * LOG2E
    tok = lambda bi, i: (bi, i, 0)
    vec = lambda bi, i: (bi, 0, 0)
    out_shapes = (jax.ShapeDtypeStruct((b, N_HEADS, n, D_HEAD_PAD), BF16),
                  jax.ShapeDtypeStruct((b, N_HEADS, n, D_HEAD_PAD), BF16),
                  jax.ShapeDtypeStruct((b, N_HEADS, nt, D_V, tm), BF16))
    weights = (w["w_dq"], w["g_q"], w["w_uq"], w["g_qn"], w["w_dkv"], w["g_kv"], w["w_uk"], w["w_uvt"], w["g_kn"])
    return pl.pallas_call(
        functools.partial(_mla_pre_kernel, q_scale=q_scale),
        grid=(b, nt),
        in_specs=[pl.BlockSpec((1, tm, d), tok),
                  pl.BlockSpec((1, 1, d), vec), pl.BlockSpec((1, 1, d), vec),
                  _resident(norm_g.shape),
                  pl.BlockSpec((tm, LANES), lambda bi, i: (i, 0)),
                  pl.BlockSpec((tm, LANES), lambda bi, i: (i, 0))]
                 + [_resident(a.shape) for a in weights],
        out_specs=(pl.BlockSpec((1, N_HEADS, tm, D_HEAD_PAD), lambda bi, i: (bi, 0, i, 0)),
                   pl.BlockSpec((1, N_HEADS, tm, D_HEAD_PAD), lambda bi, i: (bi, 0, i, 0)),
                   pl.BlockSpec((1, N_HEADS, 1, D_V, tm), lambda bi, i: (bi, 0, i, 0, 0))),
        out_shape=out_shapes,
        compiler_params=_cparams(("arbitrary", "arbitrary")),
        name="mla_pre",
    )(x, shift, scale, norm_g, cos_t, sin_t, *weights)


def _attn_kernel(q_ref, *refs, n_src):
    srcs = [(refs[2 * s], refs[2 * s + 1]) for s in range(n_src)]
    o_ref = refs[2 * n_src]
    l_ref, acc_ref = refs[2 * n_src + 1:2 * n_src + 3]
    s_bufs = refs[2 * n_src + 3:]

    q_t = q_ref[0, 0].astype(F32).T.astype(BF16)
    l_ref[...] = jnp.zeros(l_ref.shape, F32)
    acc_ref[...] = jnp.zeros(acc_ref.shape, F32)

    def produce(kc, s_ref, m_prev):
        s = _dot(kc, q_t)
        s_ref[...] = s
        return jnp.maximum(m_prev, jnp.max(s, axis=0, keepdims=True))

    def consume(s_ref, vtc, m_cur, m_prev):
        alpha = jnp.exp2(m_prev - m_cur)
        p = jnp.exp2(s_ref[...] - m_cur)
        l_ref[...] = alpha * l_ref[...] + jnp.sum(p, axis=0, keepdims=True)
        acc_ref[...] = alpha * acc_ref[...] + _dot(vtc, p.astype(BF16))

    m_init = jnp.full(l_ref.shape, -1e30, F32)
    k_ref, vt_ref = srcs[0]
    n = k_ref.shape[2]
    if n == 1:
        m_cur = produce(k_ref[0, 0, 0], s_bufs[0], m_init)
        m_prev = m_init
        pending = (s_bufs[0], vt_ref, 0)
    else:
        s_a, s_b = s_bufs[0], s_bufs[1]
        m0 = produce(k_ref[0, 0, 0], s_a, m_init)

        def body(jj, carry):
            m_prev, m_cur = carry
            j = 2 * jj
            m_1 = produce(k_ref[0, 0, j + 1], s_b, m_cur)
            consume(s_a, vt_ref[0, 0, j], m_cur, m_prev)
            m_2 = produce(k_ref[0, 0, j + 2], s_a, m_1)
            consume(s_b, vt_ref[0, 0, j + 1], m_1, m_cur)
            return m_1, m_2

        m_prev, m_cur = lax.fori_loop(0, n // 2 - 1, body, (m_init, m0))
        m_1 = produce(k_ref[0, 0, n - 1], s_b, m_cur)
        consume(s_a, vt_ref[0, 0, n - 2], m_cur, m_prev)
        m_prev, m_cur = m_cur, m_1
        pending = (s_b, vt_ref, n - 1)

    for kx_ref, vtx_ref in srcs[1:]:
        m_x = produce(kx_ref[0, 0, 0], s_bufs[-1], m_cur)
        consume(pending[0], pending[1][0, 0, pending[2]], m_cur, m_prev)
        m_prev, m_cur = m_cur, m_x
        pending = (s_bufs[-1], vtx_ref, 0)
    consume(pending[0], pending[1][0, 0, pending[2]], m_cur, m_prev)

    o = acc_ref[...] * (1.0 / l_ref[...])
    o_ref[0] = o.T.astype(BF16)


def _attention(q, srcs, tq):
    b, h, nq, dp = q.shape
    in_specs = [pl.BlockSpec((1, 1, tq, dp), lambda bi, hi, i: (bi, hi, i, 0))]
    args = [q]
    for k5, vt5 in srcs:
        in_specs.append(pl.BlockSpec((1, 1) + k5.shape[2:], lambda bi, hi, i: (bi, hi, 0, 0, 0)))
        in_specs.append(pl.BlockSpec((1, 1) + vt5.shape[2:], lambda bi, hi, i: (bi, hi, 0, 0, 0)))
        args += [k5, vt5]
    n0, tk0 = srcs[0][0].shape[2:4]
    assert n0 == 1 or n0 % 2 == 0, n0
    assert all(k5.shape[2] == 1 for k5, _ in srcs[1:])
    s_shapes = [pltpu.VMEM((tk0, tq), F32)] * (1 if n0 == 1 else 2)
    s_shapes += [pltpu.VMEM((k5.shape[3], tq), F32) for k5, _ in srcs[1:2]]
    return pl.pallas_call(
        functools.partial(_attn_kernel, n_src=len(srcs)),
        grid=(b, h, nq // tq),
        in_specs=in_specs,
        out_specs=pl.BlockSpec((1, tq, D_V), lambda bi, hi, i: (bi, i, hi)),
        out_shape=jax.ShapeDtypeStruct((b, nq, h * D_V), BF16),
        scratch_shapes=[pltpu.VMEM((1, tq), F32), pltpu.VMEM((D_V, tq), F32)] + s_shapes,
        compiler_params=_cparams(("arbitrary", "arbitrary", "arbitrary")),
        name="mla_attn",
    )(*args)


def _mla_ffn_kernel(x_ref, o_ref, g1_ref, sh_ref, sc_ref, g2_ref, ng_ref,
                    wo_ref, w1_ref, w3_ref, w2_ref, out_ref, *, n_chunks):
    x1 = x_ref[0] + g1_ref[0] * _dot(o_ref[0], wo_ref[...])
    h2 = _prenorm(x1, ng_ref[...], sh_ref[0], sc_ref[0]).astype(BF16)
    hid = w1_ref.shape[1]
    hc = hid // n_chunks
    y = None
    for c in range(n_chunks):
        a = _dot(h2, w1_ref[:, c * hc:(c + 1) * hc])
        g = _dot(h2, w3_ref[:, c * hc:(c + 1) * hc])
        act = (jax.nn.silu(a) * g).astype(BF16)
        yc = _dot(act, w2_ref[c * hc:(c + 1) * hc, :])
        y = yc if y is None else y + yc
    out_ref[0] = x1 + g2_ref[0] * y


def _mla_ffn(x, o, gate1, shift2, scale2, gate2, norm_g, w_o, w1, w3, w2, tm):
    b, n, d = x.shape
    hid = w1.shape[1]
    n_chunks = 2 if (hid // 2) % LANES == 0 else 1
    tok = lambda bi, i: (bi, i, 0)
    vec = lambda bi, i: (bi, 0, 0)
    return pl.pallas_call(
        functools.partial(_mla_ffn_kernel, n_chunks=n_chunks),
        grid=(b, n // tm),
        in_specs=[pl.BlockSpec((1, tm, d), tok), pl.BlockSpec((1, tm, o.shape[2]), tok),
                  pl.BlockSpec((1, 1, d), vec), pl.BlockSpec((1, 1, d), vec),
                  pl.BlockSpec((1, 1, d), vec), pl.BlockSpec((1, 1, d), vec),
                  _resident(norm_g.shape), _resident(w_o.shape),
                  _resident(w1.shape), _resident(w3.shape), _resident(w2.shape)],
        out_specs=pl.BlockSpec((1, tm, d), tok),
        out_shape=jax.ShapeDtypeStruct((b, n, d), F32),
        compiler_params=_cparams(("arbitrary", "arbitrary")),
        name="mla_ffn",
    )(x, o, gate1, shift2, scale2, gate2, norm_g, w_o, w1, w3, w2)


def _rg_in_kernel(xp_ref, xm_ref, xn_ref, sh_ref, sc_ref, ng_ref, win_ref, wg_ref, cw_ref, cb_ref,
                  u_ref, gl_ref, *, tm):
    i = pl.program_id(1)
    nt = pl.num_programs(1)
    x_ext = jnp.concatenate([xp_ref[0], xm_ref[0], xn_ref[0]], axis=0)
    h = _prenorm(x_ext, ng_ref[...], sh_ref[0], sc_ref[0]).astype(BF16)
    u_ext = _dot(h, win_ref[...])
    row = lax.broadcasted_iota(jnp.int32, (tm + 2 * SUBLANES, 1), 0)
    valid = jnp.logical_and(jnp.logical_or(row >= SUBLANES, i > 0),
                            jnp.logical_or(row < tm + SUBLANES, i < nt - 1))
    u_ext = jnp.where(valid, u_ext, 0.0)
    acc = cb_ref[...] + cw_ref[0:1, :] * u_ext[6:6 + tm]
    for k in range(1, 4):
        acc = acc + cw_ref[k:k + 1, :] * u_ext[6 + k:6 + k + tm]
    u_ref[0] = acc.astype(BF16)
    gl_ref[0] = jax.nn.gelu(_dot(h[SUBLANES:SUBLANES + tm], wg_ref[...])).astype(BF16)


def _rg_in(x, shift, scale, norm_g, w_in, w_gate, conv_w, conv_b, tm):
    b, n, d = x.shape
    c = w_in.shape[1]
    r = tm // SUBLANES
    last = n // SUBLANES - 1
    vec = lambda bi, i: (bi, 0, 0)
    tok = lambda bi, i: (bi, i, 0)
    return pl.pallas_call(
        functools.partial(_rg_in_kernel, tm=tm),
        grid=(b, n // tm),
        in_specs=[pl.BlockSpec((1, SUBLANES, d), lambda bi, i: (bi, jnp.maximum(i * r - 1, 0), 0)),
                  pl.BlockSpec((1, tm, d), tok),
                  pl.BlockSpec((1, SUBLANES, d), lambda bi, i: (bi, jnp.minimum((i + 1) * r, last), 0)),
                  pl.BlockSpec((1, 1, d), vec), pl.BlockSpec((1, 1, d), vec),
                  _resident(norm_g.shape), _resident(w_in.shape), _resident(w_gate.shape),
                  _resident(conv_w.shape), _resident(conv_b.shape)],
        out_specs=(pl.BlockSpec((1, tm, c), tok), pl.BlockSpec((1, tm, c), tok)),
        out_shape=(jax.ShapeDtypeStruct((b, n, c), BF16), jax.ShapeDtypeStruct((b, n, c), BF16)),
        compiler_params=_cparams(("arbitrary", "arbitrary")),
        name="rg_in",
    )(x, x, x, shift, scale, norm_g, w_in, w_gate, conv_w, conv_b)


def _rg_scan_kernel(u_ref, h0_ref, wa_ref, ba_ref, wx_ref, bx_ref, lam_ref,
                    hout_ref, hfin_ref, a_s, b_s, o_s, h_s, *, reverse, tn, pitch):
    @pl.when(pl.program_id(1) == 0)
    def _():
        h_s[...] = h0_ref[0]

    u = u_ref[0]
    width = u.shape[1]
    bw = width // RG_BLOCKS
    for blk in range(RG_BLOCKS):
        cs = slice(blk * bw, (blk + 1) * bw)
        ub = u[:, cs]
        r = jax.nn.sigmoid(_dot(ub, wa_ref[blk]) + ba_ref[:, cs])
        ig = jax.nn.sigmoid(_dot(ub, wx_ref[blk]) + bx_ref[:, cs])
        log_a = -RG_C * r * jax.nn.softplus(-lam_ref[:, cs])
        a = jnp.exp(log_a)
        bb = jnp.sqrt(1.0 - a * a) * (ig * ub.astype(F32))
        for half in range(bw // LANES):
            j = blk * (bw // LANES) + half
            a_s[j * pitch:j * pitch + tn, :] = a[:, half * LANES:(half + 1) * LANES]
            b_s[j * pitch:j * pitch + tn, :] = bb[:, half * LANES:(half + 1) * LANES]

    n_ct = width // LANES

    def body(g, h):
        for s in range(SUBLANES):
            t = g * SUBLANES + s
            if reverse:
                t = tn - 1 - t
            idx = pl.ds(t, n_ct, stride=pitch)
            h = a_s[idx, :] * h + b_s[idx, :]
            o_s[idx, :] = h
        return h

    h = lax.fori_loop(0, tn // SUBLANES, body, h_s[...])
    h_s[...] = h
    hfin_ref[0] = h
    for j in range(n_ct):
        hout_ref[0, :, j * LANES:(j + 1) * LANES] = o_s[j * pitch:j * pitch + tn, :].astype(BF16)


def _rg_scan(u, h0, w_a, b_a, w_x, b_x, lam, reverse, tn):
    b, n, c = u.shape
    nt = n // tn
    n_ct = c // LANES
    pitch = tn + SUBLANES
    tmap = (lambda bi, i: (bi, nt - 1 - i, 0)) if reverse else (lambda bi, i: (bi, i, 0))
    st = lambda bi, i: (bi, 0, 0)
    scr = pltpu.VMEM((n_ct * pitch, LANES), F32)
    return pl.pallas_call(
        functools.partial(_rg_scan_kernel, reverse=reverse, tn=tn, pitch=pitch),
        grid=(b, nt),
        in_specs=[pl.BlockSpec((1, tn, c), tmap), pl.BlockSpec((1, n_ct, LANES), st),
                  _resident(w_a.shape), _resident(b_a.shape), _resident(w_x.shape),
                  _resident(b_x.shape), _resident(lam.shape)],
        out_specs=(pl.BlockSpec((1, tn, c), tmap), pl.BlockSpec((1, n_ct, LANES), st)),
        out_shape=(jax.ShapeDtypeStruct((b, n, c), BF16), jax.ShapeDtypeStruct((b, n_ct, LANES), F32)),
        scratch_shapes=[scr, scr, scr, pltpu.VMEM((n_ct, LANES), F32)],
        compiler_params=_cparams(("arbitrary", "arbitrary")),
        name="rg_scan_bwd" if reverse else "rg_scan_fwd",
    )(u, h0, w_a, b_a, w_x, b_x, lam)


def _rg_out_kernel(x_ref, hf_ref, hb_ref, gl_ref, g1_ref, sh_ref, sc_ref, ng_ref, wout_ref, wr_ref,
                   x1_ref, h2_ref, route_ref, *, n_experts):
    y = ((hf_ref[0].astype(F32) + hb_ref[0].astype(F32)) * gl_ref[0].astype(F32)).astype(BF16)
    x1 = x_ref[0] + g1_ref[0] * _dot(y, wout_ref[...])
    x1_ref[0] = x1
    h2 = _prenorm(x1, ng_ref[...], sh_ref[0], sc_ref[0])
    h2_ref[0] = h2
    wr = wr_ref[...]
    w_hi = wr.astype(BF16)
    w_lo = (wr - w_hi.astype(F32)).astype(BF16)
    h_hi = h2.astype(BF16)
    h_lo = (h2 - h_hi.astype(F32)).astype(BF16)
    logits = _dot(h_hi, w_hi) + (_dot(h_lo, w_hi) + _dot(h_hi, w_lo))
    lane = lax.broadcasted_iota(jnp.int32, logits.shape, 1).astype(F32)
    neg = jnp.float32(-jnp.inf)
    lg = jnp.where(lane < n_experts, logits, neg)
    m1 = jnp.max(lg, axis=-1, keepdims=True)
    i1 = jnp.min(jnp.where(lg == m1, lane, float(LANES)), axis=-1, keepdims=True)
    lg2 = jnp.where(lane == i1, neg, lg)
    m2 = jnp.max(lg2, axis=-1, keepdims=True)
    i2 = jnp.min(jnp.where(lg2 == m2, lane, float(LANES)), axis=-1, keepdims=True)
    e2 = jnp.exp(m2 - m1)
    den = 1.0 + e2
    route_ref[0] = (jnp.where(lane == 0.0, i1, 0.0) + jnp.where(lane == 1.0, i2, 0.0)
                    + jnp.where(lane == 2.0, 1.0 / den, 0.0) + jnp.where(lane == 3.0, e2 / den, 0.0))


def _rg_out(x, hf, hb, gl, gate1, shift2, scale2, norm_g, w_out, w_router_pad, n_experts, tm):
    b, n, d = x.shape
    c = hf.shape[2]
    tok = lambda bi, i: (bi, i, 0)
    vec = lambda bi, i: (bi, 0, 0)
    return pl.pallas_call(
        functools.partial(_rg_out_kernel, n_experts=n_experts),
        grid=(b, n // tm),
        in_specs=[pl.BlockSpec((1, tm, d), tok), pl.BlockSpec((1, tm, c), tok), pl.BlockSpec((1, tm, c), tok),
                  pl.BlockSpec((1, tm, c), tok),
                  pl.BlockSpec((1, 1, d), vec), pl.BlockSpec((1, 1, d), vec), pl.BlockSpec((1, 1, d), vec),
                  _resident(norm_g.shape), _resident(w_out.shape), _resident(w_router_pad.shape)],
        out_specs=(pl.BlockSpec((1, tm, d), tok), pl.BlockSpec((1, tm, d), tok),
                   pl.BlockSpec((1, tm, LANES), tok)),
        out_shape=(jax.ShapeDtypeStruct((b, n, d), F32), jax.ShapeDtypeStruct((b, n, d), F32),
                   jax.ShapeDtypeStruct((b, n, LANES), F32)),
        compiler_params=_cparams(("arbitrary", "arbitrary")),
        name="rg_out_router",
    )(x, hf, hb, gl, gate1, shift2, scale2, norm_g, w_out, w_router_pad)


def _route_tables(e_idx, tm, n_experts):
    t = e_idx.shape[0]
    n_slots = TOP_K * t
    n_tiles = n_slots // tm + n_experts
    e_flat = e_idx.T.reshape(n_slots)
    experts = jnp.arange(n_experts, dtype=jnp.int32)
    counts = jnp.sum((e_flat[:, None] == experts[None, :]).astype(jnp.int32), axis=0)
    padded = (counts + tm - 1) // tm * tm
    ends = jnp.cumsum(padded)
    n_used = (ends[-1] // tm).astype(jnp.int32)
    fill_rank = jnp.arange(tm, dtype=jnp.int32)
    fill_keys = jnp.where(fill_rank[None, :] < (padded - counts)[:, None], experts[:, None], n_experts)
    keys = jnp.concatenate([e_flat, fill_keys.reshape(n_experts * tm)])
    src = jnp.argsort(keys, stable=True).astype(jnp.int32)
    p = jnp.arange(n_tiles * tm, dtype=jnp.int32)
    valid = src < n_slots
    tok = jnp.where(valid, src % t, 0)
    dst = jnp.where(valid, src, n_slots + ((p // tm + 1) % 2) * tm + p % tm)
    tile_ids = jnp.arange(n_tiles, dtype=jnp.int32)
    tile_e = jnp.sum((tile_ids[:, None] * tm >= ends[None, :]).astype(jnp.int32), axis=1)
    last_e = jnp.max(jnp.where(tile_ids < n_used, tile_e, 0))
    tile_e = jnp.where(tile_ids < n_used, tile_e, last_e)
    fill_dst = (n_slots + tm + jnp.arange(tm, dtype=jnp.int32))[None, :]
    dst_tbl = jnp.concatenate([fill_dst, dst.reshape(n_tiles, tm)], axis=0)
    tok_tbl = jnp.concatenate([tok.reshape(n_tiles, tm), jnp.zeros((1, tm), jnp.int32)], axis=0)
    return (tile_e, n_used.reshape(1), tok_tbl.reshape(n_tiles + 1, 1, tm), dst_tbl.reshape(n_tiles + 1, 1, tm))


def _moe_group_kernel(te_ref, nu_ref, tok0_ref, tokn_ref, dstp_ref, dstl_ref, h2_hbm, w1_ref, w3_ref, w2_ref,
                      y_hbm, xbuf, ybuf, gsem, ssem, *, tm, n_chunks):
    i = pl.program_id(0)
    n_used = nu_ref[0]
    slot = lax.rem(i, 2)

    def gather_row(tok_ref, s, r):
        pltpu.make_async_copy(h2_hbm.at[pl.ds(tok_ref[0, 0, r], 1)], xbuf.at[s, pl.ds(r, 1)], gsem.at[s]).start()

    def scatter_row(dst_ref, s, r):
        pltpu.make_async_copy(ybuf.at[s, pl.ds(r, 1)], y_hbm.at[pl.ds(dst_ref[0, 0, r], 1)], ssem.at[s]).start()

    def wait_gather(s):
        pltpu.make_async_copy(h2_hbm.at[pl.ds(0, tm)], xbuf.at[s], gsem.at[s]).wait()

    def wait_scatter(s):
        pltpu.make_async_copy(ybuf.at[s], y_hbm.at[pl.ds(0, tm)], ssem.at[s]).wait()

    @pl.when(i == 0)
    def _():
        ybuf[...] = jnp.zeros(ybuf.shape, F32)
        tail = y_hbm.shape[0] - 2 * tm
        pltpu.make_async_copy(ybuf.at[0], y_hbm.at[pl.ds(tail, tm)], ssem.at[0]).start()

        def body(r, carry):
            gather_row(tok0_ref, 0, r)
            return carry
        lax.fori_loop(0, tm, body, 0, unroll=8)

    @pl.when(i < n_used)
    def _():
        wait_gather(slot)
        wait_scatter(slot)
        x = xbuf[slot].astype(BF16)
        hid = w1_ref.shape[2]
        hc = hid // n_chunks
        rows = tm // n_chunks
        y = None
        for c in range(n_chunks):
            for r in range(c * rows, (c + 1) * rows):
                gather_row(tokn_ref, 1 - slot, r)
                scatter_row(dstp_ref, 1 - slot, r)
            a = _dot(x, w1_ref[0, :, c * hc:(c + 1) * hc])
            g = _dot(x, w3_ref[0, :, c * hc:(c + 1) * hc])
            act = (jax.nn.silu(a) * g).astype(BF16)
            yc = _dot(act, w2_ref[0, c * hc:(c + 1) * hc, :])
            y = yc if y is None else y + yc
        ybuf[slot] = y

    @pl.when(i == pl.num_programs(0) - 1)
    def _():
        last = lax.rem(n_used + 1, 2)

        def body(r, carry):
            scatter_row(dstl_ref, last, r)
            return carry
        lax.fori_loop(0, tm, body, 0, unroll=8)
        wait_scatter(last)
        wait_scatter(1 - last)
        wait_gather(1 - last)


def _moe_group(h2, e_idx, w1, w3, w2, tm):
    t, d = h2.shape
    n_e, _, hid = w1.shape
    tile_e, n_used, tok, dst = _route_tables(e_idx, tm, n_e)
    n_tiles = tile_e.shape[0]
    n_chunks = 4 if (hid // 4) % LANES == 0 else 1
    smem_blk = lambda f: pl.BlockSpec((1, 1, tm), f, memory_space=pltpu.SMEM)
    wspec = lambda shape: pl.BlockSpec((1,) + shape, lambda i, te, nu: (te[i], 0, 0), pipeline_mode=pl.Buffered(1))
    grid_spec = pltpu.PrefetchScalarGridSpec(
        num_scalar_prefetch=2,
        grid=(n_tiles,),
        in_specs=[smem_blk(lambda i, te, nu: (0, 0, 0)),
                  smem_blk(lambda i, te, nu: (i + 1, 0, 0)),
                  smem_blk(lambda i, te, nu: (i, 0, 0)),
                  smem_blk(lambda i, te, nu: (nu[0], 0, 0)),
                  pl.BlockSpec(memory_space=pl.ANY),
                  wspec((d, hid)), wspec((d, hid)), wspec((hid, d))],
        out_specs=pl.BlockSpec(memory_space=pl.ANY),
        scratch_shapes=[pltpu.VMEM((2, tm, d), F32), pltpu.VMEM((2, tm, d), F32),
                        pltpu.SemaphoreType.DMA((2,)), pltpu.SemaphoreType.DMA((2,))],
    )
    return pl.pallas_call(
        functools.partial(_moe_group_kernel, tm=tm, n_chunks=n_chunks),
        grid_spec=grid_spec,
        out_shape=jax.ShapeDtypeStruct((TOP_K * t + 2 * tm, d), F32),
        compiler_params=_cparams(("arbitrary",)),
        name="moe_group",
    )(tile_e, n_used, tok, tok, dst, dst, h2, w1, w3, w2)


def _moe_combine_kernel(x1_ref, y0_ref, y1_ref, route_ref, g2_ref, out_ref):
    r = route_ref[0]
    out_ref[0] = x1_ref[0] + g2_ref[0] * (r[:, 2:3] * y0_ref[...] + r[:, 3:4] * y1_ref[...])


def _moe_combine(x1, y, route, gate2, tm):
    b, n, d = x1.shape
    nt = n // tm
    t_blocks = b * nt
    tok = lambda bi, i: (bi, i, 0)
    return pl.pallas_call(
        _moe_combine_kernel,
        grid=(b, nt),
        in_specs=[pl.BlockSpec((1, tm, d), tok),
                  pl.BlockSpec((tm, d), lambda bi, i: (bi * nt + i, 0)),
                  pl.BlockSpec((tm, d), lambda bi, i: (t_blocks + bi * nt + i, 0)),
                  pl.BlockSpec((1, tm, LANES), tok),
                  pl.BlockSpec((1, 1, d), lambda bi, i: (bi, 0, 0))],
        out_specs=pl.BlockSpec((1, tm, d), tok),
        out_shape=jax.ShapeDtypeStruct((b, n, d), F32),
        compiler_params=_cparams(("arbitrary", "arbitrary")),
        name="moe_combine",
    )(x1, y, y, route, gate2)


def _moe(h2, route, x1, gate2, w1, w3, w2, tm):
    b, n, d = x1.shape
    e_idx = route.reshape(b * n, LANES)[:, :TOP_K].astype(jnp.int32)
    y = _moe_group(h2.reshape(b * n, d), e_idx, w1, w3, w2, tm)
    return _moe_combine(x1, y, route, gate2, tm)


def _prep_mla_weights(w_dq, g_q, w_uq, w_dkv, g_kv, w_ukv, g_qn, g_kn):
    ql = w_uq.shape[0]
    half = D_ROPE // 2
    pad = LANES - D_ROPE

    def ext_cols(w_rope):
        z = jnp.zeros(w_rope.shape[:-1] + (pad,), w_rope.dtype)
        r1, r2 = w_rope[..., :half], w_rope[..., half:]
        return jnp.concatenate([w_rope, z], -1), jnp.concatenate([-r2, r1, z], -1)

    def ext_gain(g):
        z = jnp.zeros((pad,), g.dtype)
        gr1, gr2 = g[D_NOPE:D_NOPE + half], g[D_NOPE + half:]
        return jnp.concatenate([g[:D_NOPE], gr1, gr2, z, gr2, gr1, z])[None, :].astype(F32)

    wq = w_uq.reshape(ql, N_HEADS, D_QK)
    qa, qb = ext_cols(wq[:, :, D_NOPE:])
    w_uq_ext = jnp.concatenate([wq[:, :, :D_NOPE], qa, qb], -1).reshape(ql, N_HEADS * Q_EXT)
    kl = w_ukv.shape[0]
    ka, kb = ext_cols(w_dkv[:, kl:])
    w_dkv_ext = jnp.concatenate([w_dkv[:, :kl], ka, kb], -1)
    wkv = w_ukv.reshape(kl, N_HEADS, D_NOPE + D_V)
    w_uk = wkv[:, :, :D_NOPE].reshape(kl, N_HEADS * D_NOPE)
    w_uvt = wkv[:, :, D_NOPE:].reshape(kl, N_HEADS * D_V).T
    return dict(w_dq=w_dq.astype(BF16), g_q=g_q[None, :].astype(F32), w_uq=w_uq_ext.astype(BF16),
                g_qn=ext_gain(g_qn), w_dkv=w_dkv_ext.astype(BF16), g_kv=g_kv[None, :].astype(F32),
                w_uk=w_uk.astype(BF16), w_uvt=w_uvt.astype(BF16), g_kn=ext_gain(g_kn))


def _rope_tables(n_tok):
    rows = n_tok // GRID_W
    row = jnp.repeat(jnp.arange(rows, dtype=F32), GRID_W)
    col = jnp.tile(jnp.arange(GRID_W, dtype=F32), rows)
    n_freq = D_ROPE // 4
    inv = ROPE_BASE ** (-jnp.arange(n_freq, dtype=F32) / n_freq)
    ang = jnp.concatenate([row[:, None] * inv, col[:, None] * inv], axis=-1)
    z = jnp.zeros((n_tok, LANES - D_ROPE), F32)
    cos, sin = jnp.cos(ang), jnp.sin(ang)
    return jnp.concatenate([cos, cos, z], -1), jnp.concatenate([sin, sin, z], -1)


def _no_rope_tables(n_tok):
    one = jnp.ones((n_tok, D_ROPE), F32)
    z = jnp.zeros((n_tok, LANES - D_ROPE), F32)
    return jnp.concatenate([one, z], -1), jnp.zeros((n_tok, LANES), F32)


def _tile(n, target):
    t = min(n, target)
    assert n % t == 0, (n, t)
    return t


def _mla_layer(x, ctx, mods_l, mods_c, norm1_g, norm2_g, w, w_o, ffn, need_ctx):
    b, n, d = x.shape
    nc = ctx.shape[1]
    tm = _tile(n, 512)
    tmc = _tile(nc, 512)
    sh1, sc1, g1, sh2, sc2, g2 = mods_l
    csh1, csc1, cg1, csh2, csc2, cg2 = mods_c
    cos_l, sin_l = _rope_tables(n)
    cos_c, sin_c = _no_rope_tables(nc)
    ql, kl, vtl = _mla_pre(x, sh1, sc1, norm1_g, cos_l, sin_l, w, tm)
    qc, kc, vtc = _mla_pre(ctx, csh1, csc1, norm1_g, cos_c, sin_c, w, tmc)
    k5 = lambda k, t: k.reshape(k.shape[0], k.shape[1], k.shape[2] // t, t, k.shape[3])
    src_l = (k5(kl, tm), vtl)
    src_c = (k5(kc, tmc), vtc)
    ol = _attention(ql, [src_l, src_c], _tile(n, 1024))
    x_new = _mla_ffn(x, ol, g1, sh2, sc2, g2, norm2_g, w_o, *ffn, tm)
    ctx_new = ctx
    if need_ctx:
        oc = _attention(qc, [src_c], _tile(nc, 1024))
        ctx_new = _mla_ffn(ctx, oc, cg1, csh2, csc2, cg2, norm2_g, w_o, *ffn, tmc)
    return x_new, ctx_new


def _rg_layer(x, ctx, mods_l, mods_c, norm1_g, norm2_g, rg, moe, need_ctx):
    b, n, d = x.shape
    nc = ctx.shape[1]
    tm = _tile(n, 512)
    tmc = _tile(nc, 512)
    tn = _tile(n, 256)
    tnc = _tile(nc, 256)
    sh1, sc1, g1, sh2, sc2, g2 = mods_l
    csh1, csc1, cg1, csh2, csc2, cg2 = mods_c
    w_in, w_gate, conv_w, conv_b, w_a, b_a, w_x, b_x, lam, w_out = rg
    w_router_pad, mw1, mw3, mw2, n_experts = moe
    ul, gll = _rg_in(x, sh1, sc1, norm1_g, w_in, w_gate, conv_w, conv_b, tm)
    uc, glc = _rg_in(ctx, csh1, csc1, norm1_g, w_in, w_gate, conv_w, conv_b, tmc)
    zero = jnp.zeros((b, w_in.shape[1] // LANES, LANES), F32)
    hl, hc = [], []
    for dr in range(2):
        args = (w_a[dr], b_a[dr], w_x[dr], b_x[dr], lam[dr])
        hcd, h_end = _rg_scan(uc, zero, *args, reverse=bool(dr), tn=tnc)
        hld, _ = _rg_scan(ul, h_end, *args, reverse=bool(dr), tn=tn)
        hl.append(hld)
        hc.append(hcd)
    x1, h2, route = _rg_out(x, hl[0], hl[1], gll, g1, sh2, sc2, norm2_g, w_out, w_router_pad, n_experts, tm)
    x_new = _moe(h2, route, x1, g2, mw1, mw3, mw2, tm)
    ctx_new = ctx
    if need_ctx:
        c1, ch2, croute = _rg_out(ctx, hc[0], hc[1], glc, cg1, csh2, csc2, norm2_g, w_out, w_router_pad,
                                  n_experts, tmc)
        ctx_new = _moe(ch2, croute, c1, cg2, mw1, mw3, mw2, tmc)
    return x_new, ctx_new


def kernel(x, c, ctx, c_ctx, ada_w, ada_b, norm1_g, norm2_g, mla_w_dq, mla_g_q, mla_w_uq, mla_w_dkv, mla_g_kv, mla_w_ukv, mla_g_qn, mla_g_kn, mla_w_o, rg_w_in, rg_w_gate, rg_conv_w, rg_conv_b, rg_w_a, rg_b_a, rg_w_x, rg_b_x, rg_lam, rg_w_out, ffn_w1, ffn_w3, ffn_w2, moe_w_router, moe_w1, moe_w3, moe_w2):
    b, n, d = x.shape
    depth = ada_w.shape[0]
    n_experts = moe_w_router.shape[2]

    rows = 2 * SUBLANES
    cvec = jnp.concatenate([c, c_ctx[None, :], jnp.zeros((rows - b - 1, d), F32)], axis=0)
    mods = _ada_all(cvec, ada_w, ada_b)

    for i in range(depth):
        need_ctx = i < depth - 1
        j = i // 2
        chunks = [mods[i, :, k * d:(k + 1) * d] for k in range(6)]
        mods_l = [m[:b, None, :] for m in chunks]
        mods_c = [jnp.broadcast_to(m[b:b + 1, None, :], (b, 1, d)) for m in chunks]
        n1 = norm1_g[i][None, :]
        n2 = norm2_g[i][None, :]
        if i % 2 == 0:
            w = _prep_mla_weights(mla_w_dq[j], mla_g_q[j], mla_w_uq[j], mla_w_dkv[j], mla_g_kv[j],
                                  mla_w_ukv[j], mla_g_qn[j], mla_g_kn[j])
            ffn = (ffn_w1[j].astype(BF16), ffn_w3[j].astype(BF16), ffn_w2[j].astype(BF16))
            x, ctx = _mla_layer(x, ctx, mods_l, mods_c, n1, n2, w, mla_w_o[j].astype(BF16), ffn, need_ctx)
        else:
            rg = (rg_w_in[j].astype(BF16), rg_w_gate[j].astype(BF16), rg_conv_w[j], rg_conv_b[j][None, :],
                  rg_w_a[j].astype(BF16), rg_b_a[j][:, None, :], rg_w_x[j].astype(BF16), rg_b_x[j][:, None, :],
                  rg_lam[j][:, None, :], rg_w_out[j].astype(BF16))
            w_router_pad = jnp.concatenate(
                [moe_w_router[j], jnp.zeros((d, LANES - n_experts), F32)], axis=1)
            moe = (w_router_pad, moe_w1[j].astype(BF16), moe_w3[j].astype(BF16), moe_w2[j].astype(BF16), n_experts)
            x, ctx = _rg_layer(x, ctx, mods_l, mods_c, n1, n2, rg, moe, need_ctx)
    return x
```

```python
import functools
import math

import jax
import jax.numpy as jnp
from jax import lax
from jax.experimental import pallas as pl
from jax.experimental.pallas import tpu as pltpu

F32 = jnp.float32
BF16 = jnp.bfloat16

EPS = 1e-6
GRID_W = 64
N_HEADS = 8
D_NOPE = 128
D_ROPE = 64
D_V = 128
D_QK = D_NOPE + D_ROPE
D_HEAD_PAD = 256
D_VX = D_V + 16
Q_EXT = 384
ROPE_BASE = 10000.0
RG_BLOCKS = 4
RG_C = 8.0
TOP_K = 2
LANES = 128
SUBLANES = 8
VMEM_LIMIT = 56 * 1024 * 1024
LOG2E = 1.4426950408889634


def _cparams(sem):
    return pltpu.CompilerParams(dimension_semantics=sem, vmem_limit_bytes=VMEM_LIMIT)


def _resident(shape):
    nd = len(shape)
    return pl.BlockSpec(shape, lambda *_: (0,) * nd, pipeline_mode=pl.Buffered(1))


def _dot(a, b):
    return jnp.dot(a, b, preferred_element_type=F32)


def _sigmoid(x):
    return 0.5 * jnp.tanh(0.5 * x) + 0.5


def _rms(xf, g):
    return xf * lax.rsqrt(jnp.mean(xf * xf, axis=-1, keepdims=True) + EPS) * g


def _prenorm(xf, g, shift, scale):
    return _rms(xf, g) * (1.0 + scale) + shift


def _ada_kernel(c_ref, w_ref, b_ref, o_ref):
    s = jax.nn.silu(c_ref[...])
    o_ref[0] = jnp.dot(s, w_ref[0], precision=lax.Precision.HIGHEST,
                       preferred_element_type=F32) + b_ref[0]


def _ada_all(cvec, ada_w, ada_b):
    depth, d, n6 = ada_w.shape
    rows = cvec.shape[0]
    tn = 1536
    return pl.pallas_call(
        _ada_kernel,
        grid=(depth, n6 // tn),
        in_specs=[pl.BlockSpec((rows, d), lambda l, j: (0, 0)),
                  pl.BlockSpec((1, d, tn), lambda l, j: (l, 0, j)),
                  pl.BlockSpec((1, 1, tn), lambda l, j: (l, 0, j))],
        out_specs=pl.BlockSpec((1, rows, tn), lambda l, j: (l, 0, j)),
        out_shape=jax.ShapeDtypeStruct((depth, rows, n6), F32),
        compiler_params=_cparams(("arbitrary", "arbitrary")),
        name="ada_mod",
    )(cvec, ada_w, ada_b.reshape(depth, 1, n6))


def _mla_pre_kernel(x_ref, sh_ref, sc_ref, ng_ref, cos_ref, sin_ref,
                    wdq_ref, gq_ref, wuq_ref, gqn_ref,
                    wdkv_ref, gkv_ref, wuk_ref, wuvt_ref, gkn_ref,
                    q_ref, k_ref, vt_ref, *, q_scale):
    x = x_ref[0]
    h = _prenorm(x, ng_ref[...], sh_ref[0], sc_ref[0]).astype(BF16)
    cosv = cos_ref[...]
    sinv = sin_ref[...]

    qn = _rms(_dot(h, wdq_ref[...]), gq_ref[...]).astype(BF16)
    qall = _dot(qn, wuq_ref[...])
    g_n = gqn_ref[:, 0:LANES]
    g_a = gqn_ref[:, LANES:2 * LANES]
    g_b = gqn_ref[:, 2 * LANES:3 * LANES]
    for hh in range(N_HEADS):
        base = hh * Q_EXT
        nope = qall[:, base:base + LANES]
        ra = qall[:, base + LANES:base + 2 * LANES]
        rb = qall[:, base + 2 * LANES:base + 3 * LANES]
        ss = jnp.sum(nope * nope, axis=-1, keepdims=True) + jnp.sum(ra * ra, axis=-1, keepdims=True)
        inv = lax.rsqrt(ss * (1.0 / D_QK) + EPS) * q_scale
        q_ref[0, hh, :, 0:LANES] = (nope * g_n * inv).astype(BF16)
        q_ref[0, hh, :, LANES:2 * LANES] = ((ra * g_a * cosv + rb * g_b * sinv) * inv).astype(BF16)

    kva = _dot(h, wdkv_ref[...])
    ckv = _rms(kva[:, 0:LANES], gkv_ref[...]).astype(BF16)
    pa = kva[:, LANES:2 * LANES]
    pb = kva[:, 2 * LANES:3 * LANES]
    pe_ss = jnp.sum(pa * pa, axis=-1, keepdims=True)
    k_n = gkn_ref[:, 0:LANES]
    k_a = gkn_ref[:, LANES:2 * LANES]
    k_b = gkn_ref[:, 2 * LANES:3 * LANES]
    rope = pa * k_a * cosv + pb * k_b * sinv
    knope = _dot(ckv, wuk_ref[...])
    vt_all = lax.dot_general(wuvt_ref[...], ckv, (((1,), (1,)), ((), ())),
                             preferred_element_type=F32)
    for hh in range(N_HEADS):
        kn = knope[:, hh * LANES:(hh + 1) * LANES]
        ss = jnp.sum(kn * kn, axis=-1, keepdims=True) + pe_ss
        inv = lax.rsqrt(ss * (1.0 / D_QK) + EPS)
        k_ref[0, hh, :, 0:LANES] = (kn * k_n * inv).astype(BF16)
        k_ref[0, hh, :, LANES:2 * LANES] = (rope * inv).astype(BF16)
        vt_ref[0, hh, 0, 0:D_V, :] = vt_all[hh * D_V:(hh + 1) * D_V, :].astype(BF16)
        vt_ref[0, hh, 0, D_V:D_VX, :] = jnp.ones((D_VX - D_V, vt_all.shape[1]), BF16)


def _mla_pre(x, shift, scale, norm_g, cos_t, sin_t, w, tm):
    b, n, d = x.shape
    nt = n // tm
    q_scale = (D_QK ** -0.5) * LOG2E
    tok = lambda bi, i: (bi, i, 0)
    vec = lambda bi, i: (bi, 0, 0)
    out_shapes = (jax.ShapeDtypeStruct((b, N_HEADS, n, D_HEAD_PAD), BF16),
                  jax.ShapeDtypeStruct((b, N_HEADS, n, D_HEAD_PAD), BF16),
                  jax.ShapeDtypeStruct((b, N_HEADS, nt, D_VX, tm), BF16))
    weights = (w["w_dq"], w["g_q"], w["w_uq"], w["g_qn"], w["w_dkv"], w["g_kv"], w["w_uk"], w["w_uvt"], w["g_kn"])
    return pl.pallas_call(
        functools.partial(_mla_pre_kernel, q_scale=q_scale),
        grid=(b, nt),
        in_specs=[pl.BlockSpec((1, tm, d), tok),
                  pl.BlockSpec((1, 1, d), vec), pl.BlockSpec((1, 1, d), vec),
                  _resident(norm_g.shape),
                  pl.BlockSpec((tm, LANES), lambda bi, i: (i, 0)),
                  pl.BlockSpec((tm, LANES), lambda bi, i: (i, 0))]
                 + [_resident(a.shape) for a in weights],
        out_specs=(pl.BlockSpec((1, N_HEADS, tm, D_HEAD_PAD), lambda bi, i: (bi, 0, i, 0)),
                   pl.BlockSpec((1, N_HEADS, tm, D_HEAD_PAD), lambda bi, i: (bi, 0, i, 0)),
                   pl.BlockSpec((1, N_HEADS, 1, D_VX, tm), lambda bi, i: (bi, 0, i, 0, 0))),
        out_shape=out_shapes,
        compiler_params=_cparams(("arbitrary", "arbitrary")),
        name="mla_pre",
    )(x, shift, scale, norm_g, cos_t, sin_t, *weights)


def _attn_kernel(q_ref, *refs, n_src):
    srcs = [(refs[2 * s], refs[2 * s + 1]) for s in range(n_src)]
    o_ref = refs[2 * n_src]
    acc_ref = refs[2 * n_src + 1]
    s_bufs = refs[2 * n_src + 2:]

    q_t = q_ref[0, 0].astype(F32).T.astype(BF16)
    acc_ref[...] = jnp.zeros(acc_ref.shape, F32)

    def produce(kc, s_ref, m_prev):
        s = _dot(kc, q_t)
        s_ref[...] = s
        return jnp.maximum(m_prev, jnp.max(s, axis=0, keepdims=True))

    def consume(s_ref, vtc, m_cur, m_prev):
        alpha = jnp.exp2(m_prev - m_cur)
        p = jnp.exp2(s_ref[...] - m_cur).astype(BF16)
        acc_ref[...] = alpha * acc_ref[...] + _dot(vtc, p)

    m_init = jnp.full((1, acc_ref.shape[1]), -1e30, F32)
    k_ref, vt_ref = srcs[0]
    n = k_ref.shape[2]
    if n == 1:
        m_cur = produce(k_ref[0, 0, 0], s_bufs[0], m_init)
        m_prev = m_init
        pending = (s_bufs[0], vt_ref, 0)
    else:
        s_a, s_b = s_bufs[0], s_bufs[1]
        m0 = produce(k_ref[0, 0, 0], s_a, m_init)

        def body(jj, carry):
            m_prev, m_cur = carry
            j = 2 * jj
            m_1 = produce(k_ref[0, 0, j + 1], s_b, m_cur)
            consume(s_a, vt_ref[0, 0, j], m_cur, m_prev)
            m_2 = produce(k_ref[0, 0, j + 2], s_a, m_1)
            consume(s_b, vt_ref[0, 0, j + 1], m_1, m_cur)
            return m_1, m_2

        m_prev, m_cur = lax.fori_loop(0, n // 2 - 1, body, (m_init, m0))
        m_1 = produce(k_ref[0, 0, n - 1], s_b, m_cur)
        consume(s_a, vt_ref[0, 0, n - 2], m_cur, m_prev)
        m_prev, m_cur = m_cur, m_1
        pending = (s_b, vt_ref, n - 1)

    for kx_ref, vtx_ref in srcs[1:]:
        m_x = produce(kx_ref[0, 0, 0], s_bufs[-1], m_cur)
        consume(pending[0], pending[1][0, 0, pending[2]], m_cur, m_prev)
        m_prev, m_cur = m_cur, m_x
        pending = (s_bufs[-1], vtx_ref, 0)
    consume(pending[0], pending[1][0, 0, pending[2]], m_cur, m_prev)

    o = acc_ref[0:D_V, :] * (1.0 / acc_ref[D_V:D_V + 1, :])
    o_ref[0] = o.T.astype(BF16)


def _attention(q, srcs, tq):
    b, h, nq, dp = q.shape
    in_specs = [pl.BlockSpec((1, 1, tq, dp), lambda bi, hi, i: (bi, hi, i, 0))]
    args = [q]
    for k5, vt5 in srcs:
        in_specs.append(pl.BlockSpec((1, 1) + k5.shape[2:], lambda bi, hi, i: (bi, hi, 0, 0, 0)))
        in_specs.append(pl.BlockSpec((1, 1) + vt5.shape[2:], lambda bi, hi, i: (bi, hi, 0, 0, 0)))
        args += [k5, vt5]
    n0, tk0 = srcs[0][0].shape[2:4]
    assert n0 == 1 or n0 % 2 == 0, n0
    assert all(k5.shape[2] == 1 for k5, _ in srcs[1:])
    s_shapes = [pltpu.VMEM((tk0, tq), F32)] * (1 if n0 == 1 else 2)
    s_shapes += [pltpu.VMEM((k5.shape[3], tq), F32) for k5, _ in srcs[1:2]]
    return pl.pallas_call(
        functools.partial(_attn_kernel, n_src=len(srcs)),
        grid=(b, h, nq // tq),
        in_specs=in_specs,
        out_specs=pl.BlockSpec((1, tq, D_V), lambda bi, hi, i: (bi, i, hi)),
        out_shape=jax.ShapeDtypeStruct((b, nq, h * D_V), BF16),
        scratch_shapes=[pltpu.VMEM((D_VX, tq), F32)] + s_shapes,
        compiler_params=_cparams(("arbitrary", "arbitrary", "arbitrary")),
        name="mla_attn",
    )(*args)


def _mla_ffn_kernel(x_ref, o_ref, g1_ref, sh_ref, sc_ref, g2_ref, ng_ref,
                    wo_ref, w1_ref, w3_ref, w2_ref, out_ref, *, n_chunks):
    x1 = x_ref[0] + g1_ref[0] * _dot(o_ref[0], wo_ref[...])
    h2 = _prenorm(x1, ng_ref[...], sh_ref[0], sc_ref[0]).astype(BF16)
    hid = w1_ref.shape[1]
    hc = hid // n_chunks
    y = None
    for c in range(n_chunks):
        a = _dot(h2, w1_ref[:, c * hc:(c + 1) * hc])
        g = _dot(h2, w3_ref[:, c * hc:(c + 1) * hc])
        act = (jax.nn.silu(a) * g).astype(BF16)
        yc = _dot(act, w2_ref[c * hc:(c + 1) * hc, :])
        y = yc if y is None else y + yc
    out_ref[0] = x1 + g2_ref[0] * y


def _mla_ffn(x, o, gate1, shift2, scale2, gate2, norm_g, w_o, w1, w3, w2, tm):
    b, n, d = x.shape
    hid = w1.shape[1]
    n_chunks = 2 if (hid // 2) % LANES == 0 else 1
    tok = lambda bi, i: (bi, i, 0)
    vec = lambda bi, i: (bi, 0, 0)
    return pl.pallas_call(
        functools.partial(_mla_ffn_kernel, n_chunks=n_chunks),
        grid=(b, n // tm),
        in_specs=[pl.BlockSpec((1, tm, d), tok), pl.BlockSpec((1, tm, o.shape[2]), tok),
                  pl.BlockSpec((1, 1, d), vec), pl.BlockSpec((1, 1, d), vec),
                  pl.BlockSpec((1, 1, d), vec), pl.BlockSpec((1, 1, d), vec),
                  _resident(norm_g.shape), _resident(w_o.shape),
                  _resident(w1.shape), _resident(w3.shape), _resident(w2.shape)],
        out_specs=pl.BlockSpec((1, tm, d), tok),
        out_shape=jax.ShapeDtypeStruct((b, n, d), F32),
        compiler_params=_cparams(("arbitrary", "arbitrary")),
        name="mla_ffn",
    )(x, o, gate1, shift2, scale2, gate2, norm_g, w_o, w1, w3, w2)


def _rg_in_kernel(xp_ref, xm_ref, xn_ref, sh_ref, sc_ref, ng_ref, win_ref, wg_ref, cw_ref, cb_ref,
                  u_ref, gl_ref, *, tm):
    i = pl.program_id(1)
    nt = pl.num_programs(1)
    x_ext = jnp.concatenate([xp_ref[0], xm_ref[0], xn_ref[0]], axis=0)
    h = _prenorm(x_ext, ng_ref[...], sh_ref[0], sc_ref[0]).astype(BF16)
    u_ext = _dot(h, win_ref[...])
    row = lax.broadcasted_iota(jnp.int32, (tm + 2 * SUBLANES, 1), 0)
    valid = jnp.logical_and(jnp.logical_or(row >= SUBLANES, i > 0),
                            jnp.logical_or(row < tm + SUBLANES, i < nt - 1))
    u_ext = jnp.where(valid, u_ext, 0.0)
    acc = cb_ref[...] + cw_ref[0:1, :] * u_ext[6:6 + tm]
    for k in range(1, 4):
        acc = acc + cw_ref[k:k + 1, :] * u_ext[6 + k:6 + k + tm]
    u_ref[0] = acc.astype(BF16)
    gl_ref[0] = jax.nn.gelu(_dot(h[SUBLANES:SUBLANES + tm], wg_ref[...])).astype(BF16)


def _rg_in(x, shift, scale, norm_g, w_in, w_gate, conv_w, conv_b, tm):
    b, n, d = x.shape
    c = w_in.shape[1]
    r = tm // SUBLANES
    last = n // SUBLANES - 1
    vec = lambda bi, i: (bi, 0, 0)
    tok = lambda bi, i: (bi, i, 0)
    return pl.pallas_call(
        functools.partial(_rg_in_kernel, tm=tm),
        grid=(b, n // tm),
        in_specs=[pl.BlockSpec((1, SUBLANES, d), lambda bi, i: (bi, jnp.maximum(i * r - 1, 0), 0)),
                  pl.BlockSpec((1, tm, d), tok),
                  pl.BlockSpec((1, SUBLANES, d), lambda bi, i: (bi, jnp.minimum((i + 1) * r, last), 0)),
                  pl.BlockSpec((1, 1, d), vec), pl.BlockSpec((1, 1, d), vec),
                  _resident(norm_g.shape), _resident(w_in.shape), _resident(w_gate.shape),
                  _resident(conv_w.shape), _resident(conv_b.shape)],
        out_specs=(pl.BlockSpec((1, tm, c), tok), pl.BlockSpec((1, tm, c), tok)),
        out_shape=(jax.ShapeDtypeStruct((b, n, c), BF16), jax.ShapeDtypeStruct((b, n, c), BF16)),
        compiler_params=_cparams(("arbitrary", "arbitrary")),
        name="rg_in",
    )(x, x, x, shift, scale, norm_g, w_in, w_gate, conv_w, conv_b)


def _rg_scan_kernel(u_ref, h0_ref, wa_ref, ba_ref, wx_ref, bx_ref, lam_ref,
                    hout_ref, hfin_ref, a_s, b_s, o_s, h_s, *, reverse, tn, pitch, nb):
    @pl.when(pl.program_id(1) == 0)
    def _():
        h_s[...] = h0_ref[...]

    width = u_ref.shape[2]
    bw = width // RG_BLOCKS
    n_ct = width // LANES
    u = u_ref[...].reshape(nb * tn, width)
    for blk in range(RG_BLOCKS):
        cs = slice(blk * bw, (blk + 1) * bw)
        ub = u[:, cs]
        r = _sigmoid(_dot(ub, wa_ref[blk]) + ba_ref[:, cs])
        ig = _sigmoid(_dot(ub, wx_ref[blk]) + bx_ref[:, cs])
        a = jnp.exp2((-RG_C * LOG2E) * jax.nn.softplus(-lam_ref[:, cs]) * r)
        z = 1.0 - a * a
        bb = (z * lax.rsqrt(jnp.maximum(z, 1e-30))) * (ig * ub.astype(F32))
        for bi in range(nb):
            for half in range(bw // LANES):
                j = blk * (bw // LANES) + half
                rows = slice(bi * tn, (bi + 1) * tn)
                a_s[bi, j * pitch:j * pitch + tn, :] = a[rows, half * LANES:(half + 1) * LANES]
                b_s[bi, j * pitch:j * pitch + tn, :] = bb[rows, half * LANES:(half + 1) * LANES]

    def body(g, hs):
        for s in range(SUBLANES):
            t = g * SUBLANES + s
            if reverse:
                t = tn - 1 - t
            idx = pl.ds(t, n_ct, stride=pitch)
            hs = tuple(a_s[bi, idx, :] * hs[bi] + b_s[bi, idx, :] for bi in range(nb))
            for bi in range(nb):
                o_s[bi, idx, :] = hs[bi]
        return hs

    hs = lax.fori_loop(0, tn // SUBLANES, body, tuple(h_s[bi] for bi in range(nb)))
    for bi in range(nb):
        h_s[bi] = hs[bi]
        hfin_ref[bi] = hs[bi]
        for j in range(n_ct):
            hout_ref[bi, :, j * LANES:(j + 1) * LANES] = o_s[bi, j * pitch:j * pitch + tn, :].astype(BF16)


def _rg_scan(u, h0, w_a, b_a, w_x, b_x, lam, reverse, tn):
    b, n, c = u.shape
    nt = n // tn
    n_ct = c // LANES
    nb = 4 if b % 4 == 0 else (2 if b % 2 == 0 else 1)
    pitch = tn + SUBLANES
    tmap = (lambda bi, i: (bi, nt - 1 - i, 0)) if reverse else (lambda bi, i: (bi, i, 0))
    st = lambda bi, i: (bi, 0, 0)
    scr = pltpu.VMEM((nb, n_ct * pitch, LANES), F32)
    return pl.pallas_call(
        functools.partial(_rg_scan_kernel, reverse=reverse, tn=tn, pitch=pitch, nb=nb),
        grid=(b // nb, nt),
        in_specs=[pl.BlockSpec((nb, tn, c), tmap), pl.BlockSpec((nb, n_ct, LANES), st),
                  _resident(w_a.shape), _resident(b_a.shape), _resident(w_x.shape),
                  _resident(b_x.shape), _resident(lam.shape)],
        out_specs=(pl.BlockSpec((nb, tn, c), tmap), pl.BlockSpec((nb, n_ct, LANES), st)),
        out_shape=(jax.ShapeDtypeStruct((b, n, c), BF16), jax.ShapeDtypeStruct((b, n_ct, LANES), F32)),
        scratch_shapes=[scr, scr, scr, pltpu.VMEM((nb, n_ct, LANES), F32)],
        compiler_params=_cparams(("arbitrary", "arbitrary")),
        name="rg_scan_bwd" if reverse else "rg_scan_fwd",
    )(u, h0, w_a, b_a, w_x, b_x, lam)


def _rg_out_kernel(x_ref, hf_ref, hb_ref, gl_ref, g1_ref, sh_ref, sc_ref, ng_ref, wout_ref, wr_ref,
                   x1_ref, h2_ref, route_ref, *, n_experts):
    y = ((hf_ref[0].astype(F32) + hb_ref[0].astype(F32)) * gl_ref[0].astype(F32)).astype(BF16)
    x1 = x_ref[0] + g1_ref[0] * _dot(y, wout_ref[...])
    x1_ref[0] = x1
    h2 = _prenorm(x1, ng_ref[...], sh_ref[0], sc_ref[0])
    h2_ref[0] = h2
    wr = wr_ref[...]
    w_hi = wr.astype(BF16)
    w_lo = (wr - w_hi.astype(F32)).astype(BF16)
    h_hi = h2.astype(BF16)
    h_lo = (h2 - h_hi.astype(F32)).astype(BF16)
    logits = _dot(h_hi, w_hi) + (_dot(h_lo, w_hi) + _dot(h_hi, w_lo))
    lane = lax.broadcasted_iota(jnp.int32, logits.shape, 1).astype(F32)
    neg = jnp.float32(-jnp.inf)
    lg = jnp.where(lane < n_experts, logits, neg)
    m1 = jnp.max(lg, axis=-1, keepdims=True)
    i1 = jnp.min(jnp.where(lg == m1, lane, float(LANES)), axis=-1, keepdims=True)
    lg2 = jnp.where(lane == i1, neg, lg)
    m2 = jnp.max(lg2, axis=-1, keepdims=True)
    i2 = jnp.min(jnp.where(lg2 == m2, lane, float(LANES)), axis=-1, keepdims=True)
    e2 = jnp.exp(m2 - m1)
    den = 1.0 + e2
    route_ref[0] = (jnp.where(lane == 0.0, i1, 0.0) + jnp.where(lane == 1.0, i2, 0.0)
                    + jnp.where(lane == 2.0, 1.0 / den, 0.0) + jnp.where(lane == 3.0, e2 / den, 0.0))


def _rg_out(x, hf, hb, gl, gate1, shift2, scale2, norm_g, w_out, w_router_pad, n_experts, tm):
    b, n, d = x.shape
    c = hf.shape[2]
    tok = lambda bi, i: (bi, i, 0)
    vec = lambda bi, i: (bi, 0, 0)
    return pl.pallas_call(
        functools.partial(_rg_out_kernel, n_experts=n_experts),
        grid=(b, n // tm),
        in_specs=[pl.BlockSpec((1, tm, d), tok), pl.BlockSpec((1, tm, c), tok), pl.BlockSpec((1, tm, c), tok),
                  pl.BlockSpec((1, tm, c), tok),
                  pl.BlockSpec((1, 1, d), vec), pl.BlockSpec((1, 1, d), vec), pl.BlockSpec((1, 1, d), vec),
                  _resident(norm_g.shape), _resident(w_out.shape), _resident(w_router_pad.shape)],
        out_specs=(pl.BlockSpec((1, tm, d), tok), pl.BlockSpec((1, tm, d), tok),
                   pl.BlockSpec((1, tm, LANES), tok)),
        out_shape=(jax.ShapeDtypeStruct((b, n, d), F32), jax.ShapeDtypeStruct((b, n, d), F32),
                   jax.ShapeDtypeStruct((b, n, LANES), F32)),
        compiler_params=_cparams(("arbitrary", "arbitrary")),
        name="rg_out_router",
    )(x, hf, hb, gl, gate1, shift2, scale2, norm_g, w_out, w_router_pad)


def _route_tables(e_idx, tm, n_experts):
    t = e_idx.shape[0]
    n_slots = TOP_K * t
    n_tiles = n_slots // tm + n_experts
    e_flat = e_idx.T.reshape(n_slots)
    experts = jnp.arange(n_experts, dtype=jnp.int32)
    counts = jnp.sum((e_flat[:, None] == experts[None, :]).astype(jnp.int32), axis=0)
    padded = (counts + tm - 1) // tm * tm
    ends = jnp.cumsum(padded)
    n_used = (ends[-1] // tm).astype(jnp.int32)
    fill_rank = jnp.arange(tm, dtype=jnp.int32)
    fill_keys = jnp.where(fill_rank[None, :] < (padded - counts)[:, None], experts[:, None], n_experts)
    keys = jnp.concatenate([e_flat, fill_keys.reshape(n_experts * tm)])
    src = jnp.argsort(keys, stable=True).astype(jnp.int32)
    p = jnp.arange(n_tiles * tm, dtype=jnp.int32)
    valid = src < n_slots
    tok = jnp.where(valid, src % t, 0)
    dst = jnp.where(valid, src, n_slots + ((p // tm + 1) % 2) * tm + p % tm)
    tile_ids = jnp.arange(n_tiles, dtype=jnp.int32)
    tile_e = jnp.sum((tile_ids[:, None] * tm >= ends[None, :]).astype(jnp.int32), axis=1)
    last_e = jnp.max(jnp.where(tile_ids < n_used, tile_e, 0))
    tile_e = jnp.where(tile_ids < n_used, tile_e, last_e)
    fill_dst = (n_slots + tm + jnp.arange(tm, dtype=jnp.int32))[None, :]
    dst_tbl = jnp.concatenate([fill_dst, dst.reshape(n_tiles, tm)], axis=0)
    tok_tbl = jnp.concatenate([tok.reshape(n_tiles, tm), jnp.zeros((1, tm), jnp.int32)], axis=0)
    return (tile_e, n_used.reshape(1), tok_tbl.reshape(n_tiles + 1, 1, tm), dst_tbl.reshape(n_tiles + 1, 1, tm))


def _moe_group_kernel(te_ref, nu_ref, tok0_ref, tokn_ref, dstp_ref, dstl_ref, h2_hbm, w1_ref, w3_ref, w2_ref,
                      y_hbm, xbuf, ybuf, gsem, ssem, *, tm, n_chunks):
    i = pl.program_id(0)
    n_used = nu_ref[0]
    slot = lax.rem(i, 2)

    def gather_row(tok_ref, s, r):
        pltpu.make_async_copy(h2_hbm.at[pl.ds(tok_ref[0, 0, r], 1)], xbuf.at[s, pl.ds(r, 1)], gsem.at[s]).start()

    def scatter_row(dst_ref, s, r):
        pltpu.make_async_copy(ybuf.at[s, pl.ds(r, 1)], y_hbm.at[pl.ds(dst_ref[0, 0, r], 1)], ssem.at[s]).start()

    def wait_gather(s):
        pltpu.make_async_copy(h2_hbm.at[pl.ds(0, tm)], xbuf.at[s], gsem.at[s]).wait()

    def wait_scatter(s):
        pltpu.make_async_copy(ybuf.at[s], y_hbm.at[pl.ds(0, tm)], ssem.at[s]).wait()

    @pl.when(i == 0)
    def _():
        ybuf[...] = jnp.zeros(ybuf.shape, F32)
        tail = y_hbm.shape[0] - 2 * tm
        pltpu.make_async_copy(ybuf.at[0], y_hbm.at[pl.ds(tail, tm)], ssem.at[0]).start()

        def body(r, carry):
            gather_row(tok0_ref, 0, r)
            return carry
        lax.fori_loop(0, tm, body, 0, unroll=8)

    @pl.when(i < n_used)
    def _():
        wait_gather(slot)
        wait_scatter(slot)
        x = xbuf[slot].astype(BF16)
        hid = w1_ref.shape[2]
        hc = hid // n_chunks
        rows = tm // n_chunks
        y = None
        for c in range(n_chunks):
            for r in range(c * rows, (c + 1) * rows):
                gather_row(tokn_ref, 1 - slot, r)
                scatter_row(dstp_ref, 1 - slot, r)
            a = _dot(x, w1_ref[0, :, c * hc:(c + 1) * hc])
            g = _dot(x, w3_ref[0, :, c * hc:(c + 1) * hc])
            act = (jax.nn.silu(a) * g).astype(BF16)
            yc = _dot(act, w2_ref[0, c * hc:(c + 1) * hc, :])
            y = yc if y is None else y + yc
        ybuf[slot] = y

    @pl.when(i == pl.num_programs(0) - 1)
    def _():
        last = lax.rem(n_used + 1, 2)

        def body(r, carry):
            scatter_row(dstl_ref, last, r)
            return carry
        lax.fori_loop(0, tm, body, 0, unroll=8)
        wait_scatter(last)
        wait_scatter(1 - last)
        wait_gather(1 - last)


def _moe_group(h2, e_idx, w1, w3, w2, tm):
    t, d = h2.shape
    n_e, _, hid = w1.shape
    tile_e, n_used, tok, dst = _route_tables(e_idx, tm, n_e)
    n_tiles = tile_e.shape[0]
    n_chunks = 4 if (hid // 4) % LANES == 0 else 1
    smem_blk = lambda f: pl.BlockSpec((1, 1, tm), f, memory_space=pltpu.SMEM)
    wspec = lambda shape: pl.BlockSpec((1,) + shape, lambda i, te, nu: (te[i], 0, 0), pipeline_mode=pl.Buffered(1))
    grid_spec = pltpu.PrefetchScalarGridSpec(
        num_scalar_prefetch=2,
        grid=(n_tiles,),
        in_specs=[smem_blk(lambda i, te, nu: (0, 0, 0)),
                  smem_blk(lambda i, te, nu: (i + 1, 0, 0)),
                  smem_blk(lambda i, te, nu: (i, 0, 0)),
                  smem_blk(lambda i, te, nu: (nu[0], 0, 0)),
                  pl.BlockSpec(memory_space=pl.ANY),
                  wspec((d, hid)), wspec((d, hid)), wspec((hid, d))],
        out_specs=pl.BlockSpec(memory_space=pl.ANY),
        scratch_shapes=[pltpu.VMEM((2, tm, d), F32), pltpu.VMEM((2, tm, d), F32),
                        pltpu.SemaphoreType.DMA((2,)), pltpu.SemaphoreType.DMA((2,))],
    )
    return pl.pallas_call(
        functools.partial(_moe_group_kernel, tm=tm, n_chunks=n_chunks),
        grid_spec=grid_spec,
        out_shape=jax.ShapeDtypeStruct((TOP_K * t + 2 * tm, d), F32),
        compiler_params=_cparams(("arbitrary",)),
        name="moe_group",
    )(tile_e, n_used, tok, tok, dst, dst, h2, w1, w3, w2)


def _moe_combine_kernel(x1_ref, y0_ref, y1_ref, route_ref, g2_ref, out_ref):
    r = route_ref[0]
    out_ref[0] = x1_ref[0] + g2_ref[0] * (r[:, 2:3] * y0_ref[...] + r[:, 3:4] * y1_ref[...])


def _moe_combine(x1, y, route, gate2, tm):
    b, n, d = x1.shape
    nt = n // tm
    t_blocks = b * nt
    tok = lambda bi, i: (bi, i, 0)
    return pl.pallas_call(
        _moe_combine_kernel,
        grid=(b, nt),
        in_specs=[pl.BlockSpec((1, tm, d), tok),
                  pl.BlockSpec((tm, d), lambda bi, i: (bi * nt + i, 0)),
                  pl.BlockSpec((tm, d), lambda bi, i: (t_blocks + bi * nt + i, 0)),
                  pl.BlockSpec((1, tm, LANES), tok),
                  pl.BlockSpec((1, 1, d), lambda bi, i: (bi, 0, 0))],
        out_specs=pl.BlockSpec((1, tm, d), tok),
        out_shape=jax.ShapeDtypeStruct((b, n, d), F32),
        compiler_params=_cparams(("arbitrary", "arbitrary")),
        name="moe_combine",
    )(x1, y, y, route, gate2)


def _moe(h2, route, x1, gate2, w1, w3, w2, tm):
    b, n, d = x1.shape
    e_idx = route.reshape(b * n, LANES)[:, :TOP_K].astype(jnp.int32)
    y = _moe_group(h2.reshape(b * n, d), e_idx, w1, w3, w2, tm)
    return _moe_combine(x1, y, route, gate2, tm)


def _prep_mla_weights(w_dq, g_q, w_uq, w_dkv, g_kv, w_ukv, g_qn, g_kn):
    ql = w_uq.shape[0]
    half = D_ROPE // 2
    pad = LANES - D_ROPE

    def ext_cols(w_rope):
        z = jnp.zeros(w_rope.shape[:-1] + (pad,), w_rope.dtype)
        r1, r2 = w_rope[..., :half], w_rope[..., half:]
        return jnp.concatenate([w_rope, z], -1), jnp.concatenate([-r2, r1, z], -1)

    def ext_gain(g):
        z = jnp.zeros((pad,), g.dtype)
        gr1, gr2 = g[D_NOPE:D_NOPE + half], g[D_NOPE + half:]
        return jnp.concatenate([g[:D_NOPE], gr1, gr2, z, gr2, gr1, z])[None, :].astype(F32)

    wq = w_uq.reshape(ql, N_HEADS, D_QK)
    qa, qb = ext_cols(wq[:, :, D_NOPE:])
    w_uq_ext = jnp.concatenate([wq[:, :, :D_NOPE], qa, qb], -1).reshape(ql, N_HEADS * Q_EXT)
    kl = w_ukv.shape[0]
    ka, kb = ext_cols(w_dkv[:, kl:])
    w_dkv_ext = jnp.concatenate([w_dkv[:, :kl], ka, kb], -1)
    wkv = w_ukv.reshape(kl, N_HEADS, D_NOPE + D_V)
    w_uk = wkv[:, :, :D_NOPE].reshape(kl, N_HEADS * D_NOPE)
    w_uvt = wkv[:, :, D_NOPE:].reshape(kl, N_HEADS * D_V).T
    return dict(w_dq=w_dq.astype(BF16), g_q=g_q[None, :].astype(F32), w_uq=w_uq_ext.astype(BF16),
                g_qn=ext_gain(g_qn), w_dkv=w_dkv_ext.astype(BF16), g_kv=g_kv[None, :].astype(F32),
                w_uk=w_uk.astype(BF16), w_uvt=w_uvt.astype(BF16), g_kn=ext_gain(g_kn))


def _rope_tables(n_tok):
    rows = n_tok // GRID_W
    row = jnp.repeat(jnp.arange(rows, dtype=F32), GRID_W)
    col = jnp.tile(jnp.arange(GRID_W, dtype=F32), rows)
    n_freq = D_ROPE // 4
    inv = ROPE_BASE ** (-jnp.arange(n_freq, dtype=F32) / n_freq)
    ang = jnp.concatenate([row[:, None] * inv, col[:, None] * inv], axis=-1)
    z = jnp.zeros((n_tok, LANES - D_ROPE), F32)
    cos, sin = jnp.cos(ang), jnp.sin(ang)
    return jnp.concatenate([cos, cos, z], -1), jnp.concatenate([sin, sin, z], -1)


def _no_rope_tables(n_tok):
    one = jnp.ones((n_tok, D_ROPE), F32)
    z = jnp.zeros((n_tok, LANES - D_ROPE), F32)
    return jnp.concatenate([one, z], -1), jnp.zeros((n_tok, LANES), F32)


def _tile(n, target):
    t = min(n, target)
    assert n % t == 0, (n, t)
    return t


def _mla_layer(x, ctx, mods_l, mods_c, norm1_g, norm2_g, w, w_o, ffn, need_ctx):
    b, n, d = x.shape
    nc = ctx.shape[1]
    tm = _tile(n, 512)
    tmc = _tile(nc, 512)
    sh1, sc1, g1, sh2, sc2, g2 = mods_l
    csh1, csc1, cg1, csh2, csc2, cg2 = mods_c
    cos_l, sin_l = _rope_tables(n)
    cos_c, sin_c = _no_rope_tables(nc)
    ql, kl, vtl = _mla_pre(x, sh1, sc1, norm1_g, cos_l, sin_l, w, tm)
    qc, kc, vtc = _mla_pre(ctx, csh1, csc1, norm1_g, cos_c, sin_c, w, tmc)
    k5 = lambda k, t: k.reshape(k.shape[0], k.shape[1], k.shape[2] // t, t, k.shape[3])
    src_l = (k5(kl, tm), vtl)
    src_c = (k5(kc, tmc), vtc)
    ol = _attention(ql, [src_l, src_c], _tile(n, 2048))
    x_new = _mla_ffn(x, ol, g1, sh2, sc2, g2, norm2_g, w_o, *ffn, tm)
    ctx_new = ctx
    if need_ctx:
        oc = _attention(qc, [src_c], _tile(nc, 1024))
        ctx_new = _mla_ffn(ctx, oc, cg1, csh2, csc2, cg2, norm2_g, w_o, *ffn, tmc)
    return x_new, ctx_new


def _rg_layer(x, ctx, mods_l, mods_c, norm1_g, norm2_g, rg, moe, need_ctx):
    b, n, d = x.shape
    nc = ctx.shape[1]
    tm = _tile(n, 512)
    tmc = _tile(nc, 512)
    tn = _tile(n, 256)
    tnc = _tile(nc, 256)
    sh1, sc1, g1, sh2, sc2, g2 = mods_l
    csh1, csc1, cg1, csh2, csc2, cg2 = mods_c
    w_in, w_gate, conv_w, conv_b, w_a, b_a, w_x, b_x, lam, w_out = rg
    w_router_pad, mw1, mw3, mw2, n_experts = moe
    ul, gll = _rg_in(x, sh1, sc1, norm1_g, w_in, w_gate, conv_w, conv_b, tm)
    uc, glc = _rg_in(ctx, csh1, csc1, norm1_g, w_in, w_gate, conv_w, conv_b, tmc)
    zero = jnp.zeros((b, w_in.shape[1] // LANES, LANES), F32)
    hl, hc = [], []
    for dr in range(2):
        args = (w_a[dr], b_a[dr], w_x[dr], b_x[dr], lam[dr])
        hcd, h_end = _rg_scan(uc, zero, *args, reverse=bool(dr), tn=tnc)
        hld, _ = _rg_scan(ul, h_end, *args, reverse=bool(dr), tn=tn)
        hl.append(hld)
        hc.append(hcd)
    x1, h2, route = _rg_out(x, hl[0], hl[1], gll, g1, sh2, sc2, norm2_g, w_out, w_router_pad, n_experts, tm)
    x_new = _moe(h2, route, x1, g2, mw1, mw3, mw2, tm)
    ctx_new = ctx
    if need_ctx:
        c1, ch2, croute = _rg_out(ctx, hc[0], hc[1], glc, cg1, csh2, csc2, norm2_g, w_out, w_router_pad,
                                  n_experts, tmc)
        ctx_new = _moe(ch2, croute, c1, cg2, mw1, mw3, mw2, tmc)
    return x_new, ctx_new


def kernel(x, c, ctx, c_ctx, ada_w, ada_b, norm1_g, norm2_g, mla_w_dq, mla_g_q, mla_w_uq, mla_w_dkv, mla_g_kv, mla_w_ukv, mla_g_qn, mla_g_kn, mla_w_o, rg_w_in, rg_w_gate, rg_conv_w, rg_conv_b, rg_w_a, rg_b_a, rg_w_x, rg_b_x, rg_lam, rg_w_out, ffn_w1, ffn_w3, ffn_w2, moe_w_router, moe_w1, moe_w3, moe_w2):
    b, n, d = x.shape
    depth = ada_w.shape[0]
    n_experts = moe_w_router.shape[2]

    rows = 2 * SUBLANES
    cvec = jnp.concatenate([c, c_ctx[None, :], jnp.zeros((rows - b - 1, d), F32)], axis=0)
    mods = _ada_all(cvec, ada_w, ada_b)

    for i in range(depth):
        need_ctx = i < depth - 1
        j = i // 2
        chunks = [mods[i, :, k * d:(k + 1) * d] for k in range(6)]
        mods_l = [m[:b, None, :] for m in chunks]
        mods_c = [jnp.broadcast_to(m[b:b + 1, None, :], (b, 1, d)) for m in chunks]
        n1 = norm1_g[i][None, :]
        n2 = norm2_g[i][None, :]
        if i % 2 == 0:
            w = _prep_mla_weights(mla_w_dq[j], mla_g_q[j], mla_w_uq[j], mla_w_dkv[j], mla_g_kv[j],
                                  mla_w_ukv[j], mla_g_qn[j], mla_g_kn[j])
            ffn = (ffn_w1[j].astype(BF16), ffn_w3[j].astype(BF16), ffn_w2[j].astype(BF16))
            x, ctx = _mla_layer(x, ctx, mods_l, mods_c, n1, n2, w, mla_w_o[j].astype(BF16), ffn, need_ctx)
        else:
            rg = (rg_w_in[j].astype(BF16), rg_w_gate[j].astype(BF16), rg_conv_w[j], rg_conv_b[j][None, :],
                  rg_w_a[j].astype(BF16), rg_b_a[j][:, None, :], rg_w_x[j].astype(BF16), rg_b_x[j][:, None, :],
                  rg_lam[j][:, None, :], rg_w_out[j].astype(BF16))
            w_router_pad = jnp.concatenate(
                [moe_w_router[j], jnp.zeros((d, LANES - n_experts), F32)], axis=1)
            moe = (w_router_pad, moe_w1[j].astype(BF16), moe_w3[j].astype(BF16), moe_w2[j].astype(BF16), n_experts)
            x, ctx = _rg_layer(x, ctx, mods_l, mods_c, n1, n2, rg, moe, need_ctx)
    return x
```

```python
import functools
import math

import jax
import jax.numpy as jnp
from jax import lax
from jax.experimental import pallas as pl
from jax.experimental.pallas import tpu as pltpu

F32 = jnp.float32
BF16 = jnp.bfloat16

EPS = 1e-6
GRID_W = 64
N_HEADS = 8
D_NOPE = 128
D_ROPE = 64
D_V = 128
D_QK = D_NOPE + D_ROPE
D_HEAD_PAD = 256
D_VX = D_V + 16
Q_EXT = 384
ROPE_BASE = 10000.0
RG_BLOCKS = 4
RG_C = 8.0
TOP_K = 2
LANES = 128
SUBLANES = 8
MXU_COLS = 256
MAX_HIDDEN_CHUNK = 3072
VMEM_LIMIT = 56 * 1024 * 1024
LOG2E = 1.4426950408889634


def _cparams(sem):
    return pltpu.CompilerParams(dimension_semantics=sem, vmem_limit_bytes=VMEM_LIMIT)


def _resident(shape):
    nd = len(shape)
    return pl.BlockSpec(shape, lambda *_: (0,) * nd, pipeline_mode=pl.Buffered(1))


def _dot(a, b):
    return jnp.dot(a, b, preferred_element_type=F32)


def _sigmoid(x):
    return 0.5 * jnp.tanh(0.5 * x) + 0.5


def _rms(xf, g):
    return xf * lax.rsqrt(jnp.mean(xf * xf, axis=-1, keepdims=True) + EPS) * g


def _prenorm(xf, g, shift, scale):
    return _rms(xf, g) * (1.0 + scale) + shift


def _ada_kernel(c_ref, w_ref, b_ref, o_ref):
    s = jax.nn.silu(c_ref[...])
    o_ref[0] = jnp.dot(s, w_ref[0], precision=lax.Precision.HIGHEST,
                       preferred_element_type=F32) + b_ref[0]


def _ada_all(cvec, ada_w, ada_b):
    depth, d, n6 = ada_w.shape
    rows = cvec.shape[0]
    tn = 1536
    return pl.pallas_call(
        _ada_kernel,
        grid=(depth, n6 // tn),
        in_specs=[pl.BlockSpec((rows, d), lambda l, j: (0, 0)),
                  pl.BlockSpec((1, d, tn), lambda l, j: (l, 0, j)),
                  pl.BlockSpec((1, 1, tn), lambda l, j: (l, 0, j))],
        out_specs=pl.BlockSpec((1, rows, tn), lambda l, j: (l, 0, j)),
        out_shape=jax.ShapeDtypeStruct((depth, rows, n6), F32),
        compiler_params=_cparams(("arbitrary", "arbitrary")),
        name="ada_mod",
    )(cvec, ada_w, ada_b.reshape(depth, 1, n6))


def _mla_pre_kernel(x_ref, sh_ref, sc_ref, ng_ref, cos_ref, sin_ref,
                    wdq_ref, gq_ref, wuq_ref, gqn_ref,
                    wdkv_ref, gkv_ref, wuk_ref, wuvt_ref, gkn_ref,
                    q_ref, k_ref, vt_ref, *, q_scale):
    x = x_ref[0]
    h = _prenorm(x, ng_ref[...], sh_ref[0], sc_ref[0]).astype(BF16)
    cosv = cos_ref[...]
    sinv = sin_ref[...]

    qn = _rms(_dot(h, wdq_ref[...]), gq_ref[...]).astype(BF16)
    qall = _dot(qn, wuq_ref[...])
    g_n = gqn_ref[:, 0:LANES]
    g_a = gqn_ref[:, LANES:2 * LANES]
    g_b = gqn_ref[:, 2 * LANES:3 * LANES]
    for hh in range(N_HEADS):
        base = hh * Q_EXT
        nope = qall[:, base:base + LANES]
        ra = qall[:, base + LANES:base + 2 * LANES]
        rb = qall[:, base + 2 * LANES:base + 3 * LANES]
        ss = jnp.sum(nope * nope, axis=-1, keepdims=True) + jnp.sum(ra * ra, axis=-1, keepdims=True)
        inv = lax.rsqrt(ss * (1.0 / D_QK) + EPS) * q_scale
        q_ref[0, hh, :, 0:LANES] = (nope * g_n * inv).astype(BF16)
        q_ref[0, hh, :, LANES:2 * LANES] = ((ra * g_a * cosv + rb * g_b * sinv) * inv).astype(BF16)

    kva = _dot(h, wdkv_ref[...])
    ckv = _rms(kva[:, 0:LANES], gkv_ref[...]).astype(BF16)
    pa = kva[:, LANES:2 * LANES]
    pb = kva[:, 2 * LANES:3 * LANES]
    pe_ss = jnp.sum(pa * pa, axis=-1, keepdims=True)
    k_n = gkn_ref[:, 0:LANES]
    k_a = gkn_ref[:, LANES:2 * LANES]
    k_b = gkn_ref[:, 2 * LANES:3 * LANES]
    rope = pa * k_a * cosv + pb * k_b * sinv
    knope = _dot(ckv, wuk_ref[...])
    vt_all = lax.dot_general(wuvt_ref[...], ckv, (((1,), (1,)), ((), ())),
                             preferred_element_type=F32)
    for hh in range(N_HEADS):
        kn = knope[:, hh * LANES:(hh + 1) * LANES]
        ss = jnp.sum(kn * kn, axis=-1, keepdims=True) + pe_ss
        inv = lax.rsqrt(ss * (1.0 / D_QK) + EPS)
        k_ref[0, hh, :, 0:LANES] = (kn * k_n * inv).astype(BF16)
        k_ref[0, hh, :, LANES:2 * LANES] = (rope * inv).astype(BF16)
        vt_ref[0, hh, 0, 0:D_V, :] = vt_all[hh * D_V:(hh + 1) * D_V, :].astype(BF16)
        vt_ref[0, hh, 0, D_V:D_VX, :] = jnp.ones((D_VX - D_V, vt_all.shape[1]), BF16)


def _mla_pre(x, shift, scale, norm_g, cos_t, sin_t, w, tm):
    b, n, d = x.shape
    nt = n // tm
    q_scale = (D_QK ** -0.5) * LOG2E
    tok = lambda bi, i: (bi, i, 0)
    vec = lambda bi, i: (bi, 0, 0)
    out_shapes = (jax.ShapeDtypeStruct((b, N_HEADS, n, D_HEAD_PAD), BF16),
                  jax.ShapeDtypeStruct((b, N_HEADS, n, D_HEAD_PAD), BF16),
                  jax.ShapeDtypeStruct((b, N_HEADS, nt, D_VX, tm), BF16))
    weights = (w["w_dq"], w["g_q"], w["w_uq"], w["g_qn"], w["w_dkv"], w["g_kv"], w["w_uk"], w["w_uvt"], w["g_kn"])
    return pl.pallas_call(
        functools.partial(_mla_pre_kernel, q_scale=q_scale),
        grid=(b, nt),
        in_specs=[pl.BlockSpec((1, tm, d), tok),
                  pl.BlockSpec((1, 1, d), vec), pl.BlockSpec((1, 1, d), vec),
                  _resident(norm_g.shape),
                  pl.BlockSpec((tm, LANES), lambda bi, i: (i, 0)),
                  pl.BlockSpec((tm, LANES), lambda bi, i: (i, 0))]
                 + [_resident(a.shape) for a in weights],
        out_specs=(pl.BlockSpec((1, N_HEADS, tm, D_HEAD_PAD), lambda bi, i: (bi, 0, i, 0)),
                   pl.BlockSpec((1, N_HEADS, tm, D_HEAD_PAD), lambda bi, i: (bi, 0, i, 0)),
                   pl.BlockSpec((1, N_HEADS, 1, D_VX, tm), lambda bi, i: (bi, 0, i, 0, 0))),
        out_shape=out_shapes,
        compiler_params=_cparams(("arbitrary", "arbitrary")),
        name="mla_pre",
    )(x, shift, scale, norm_g, cos_t, sin_t, *weights)


def _attn_kernel(q_ref, *refs, n_src):
    srcs = [(refs[2 * s], refs[2 * s + 1]) for s in range(n_src)]
    o_ref = refs[2 * n_src]
    acc_ref = refs[2 * n_src + 1]
    s_bufs = refs[2 * n_src + 2:]

    q_t = q_ref[0, 0].astype(F32).T.astype(BF16)
    acc_ref[...] = jnp.zeros(acc_ref.shape, F32)

    def produce(kc, s_ref, m_prev):
        s = _dot(kc, q_t)
        s_ref[...] = s
        return jnp.maximum(m_prev, jnp.max(s, axis=0, keepdims=True))

    def consume(s_ref, vtc, m_cur, m_prev):
        alpha = jnp.exp2(m_prev - m_cur)
        p = jnp.exp2(s_ref[...] - m_cur).astype(BF16)
        acc_ref[...] = alpha * acc_ref[...] + _dot(vtc, p)

    m_init = jnp.full((1, acc_ref.shape[1]), -1e30, F32)
    k_ref, vt_ref = srcs[0]
    n = k_ref.shape[2]
    if n == 1:
        m_cur = produce(k_ref[0, 0, 0], s_bufs[0], m_init)
        m_prev = m_init
        pending = (s_bufs[0], vt_ref, 0)
    else:
        s_a, s_b = s_bufs[0], s_bufs[1]
        m0 = produce(k_ref[0, 0, 0], s_a, m_init)

        def body(jj, carry):
            m_prev, m_cur = carry
            j = 2 * jj
            m_1 = produce(k_ref[0, 0, j + 1], s_b, m_cur)
            consume(s_a, vt_ref[0, 0, j], m_cur, m_prev)
            m_2 = produce(k_ref[0, 0, j + 2], s_a, m_1)
            consume(s_b, vt_ref[0, 0, j + 1], m_1, m_cur)
            return m_1, m_2

        m_prev, m_cur = lax.fori_loop(0, n // 2 - 1, body, (m_init, m0))
        m_1 = produce(k_ref[0, 0, n - 1], s_b, m_cur)
        consume(s_a, vt_ref[0, 0, n - 2], m_cur, m_prev)
        m_prev, m_cur = m_cur, m_1
        pending = (s_b, vt_ref, n - 1)

    for kx_ref, vtx_ref in srcs[1:]:
        m_x = produce(kx_ref[0, 0, 0], s_bufs[-1], m_cur)
        consume(pending[0], pending[1][0, 0, pending[2]], m_cur, m_prev)
        m_prev, m_cur = m_cur, m_x
        pending = (s_bufs[-1], vtx_ref, 0)
    consume(pending[0], pending[1][0, 0, pending[2]], m_cur, m_prev)

    o = acc_ref[0:D_V, :] * (1.0 / acc_ref[D_V:D_V + 1, :])
    o_ref[0] = o.T.astype(BF16)


def _attention(q, srcs, tq):
    b, h, nq, dp = q.shape
    in_specs = [pl.BlockSpec((1, 1, tq, dp), lambda bi, hi, i: (bi, hi, i, 0))]
    args = [q]
    for k5, vt5 in srcs:
        in_specs.append(pl.BlockSpec((1, 1) + k5.shape[2:], lambda bi, hi, i: (bi, hi, 0, 0, 0)))
        in_specs.append(pl.BlockSpec((1, 1) + vt5.shape[2:], lambda bi, hi, i: (bi, hi, 0, 0, 0)))
        args += [k5, vt5]
    n0, tk0 = srcs[0][0].shape[2:4]
    assert n0 == 1 or n0 % 2 == 0, n0
    assert all(k5.shape[2] == 1 for k5, _ in srcs[1:])
    s_shapes = [pltpu.VMEM((tk0, tq), F32)] * (1 if n0 == 1 else 2)
    s_shapes += [pltpu.VMEM((k5.shape[3], tq), F32) for k5, _ in srcs[1:2]]
    return pl.pallas_call(
        functools.partial(_attn_kernel, n_src=len(srcs)),
        grid=(b, h, nq // tq),
        in_specs=in_specs,
        out_specs=pl.BlockSpec((1, tq, D_V), lambda bi, hi, i: (bi, i, hi)),
        out_shape=jax.ShapeDtypeStruct((b, nq, h * D_V), BF16),
        scratch_shapes=[pltpu.VMEM((D_VX, tq), F32)] + s_shapes,
        compiler_params=_cparams(("arbitrary", "arbitrary", "arbitrary")),
        name="mla_attn",
    )(*args)


def _mla_ffn_kernel(x_ref, o_ref, g1_ref, sh_ref, sc_ref, g2_ref, ng_ref,
                    wo_ref, w1_ref, w3_ref, w2_ref, out_ref, *, n_chunks):
    x1 = x_ref[0] + g1_ref[0] * _dot(o_ref[0], wo_ref[...])
    h2 = _prenorm(x1, ng_ref[...], sh_ref[0], sc_ref[0]).astype(BF16)
    hid = w1_ref.shape[1]
    hc = hid // n_chunks
    y = None
    for c in range(n_chunks):
        a = _dot(h2, w1_ref[:, c * hc:(c + 1) * hc])
        g = _dot(h2, w3_ref[:, c * hc:(c + 1) * hc])
        act = (jax.nn.silu(a) * g).astype(BF16)
        yc = _dot(act, w2_ref[c * hc:(c + 1) * hc, :])
        y = yc if y is None else y + yc
    out_ref[0] = x1 + g2_ref[0] * y


def _mla_ffn(x, o, gate1, shift2, scale2, gate2, norm_g, w_o, w1, w3, w2, tm):
    b, n, d = x.shape
    hid = w1.shape[1]
    n_chunks = _n_hidden_chunks(hid)
    tok = lambda bi, i: (bi, i, 0)
    vec = lambda bi, i: (bi, 0, 0)
    return pl.pallas_call(
        functools.partial(_mla_ffn_kernel, n_chunks=n_chunks),
        grid=(b, n // tm),
        in_specs=[pl.BlockSpec((1, tm, d), tok), pl.BlockSpec((1, tm, o.shape[2]), tok),
                  pl.BlockSpec((1, 1, d), vec), pl.BlockSpec((1, 1, d), vec),
                  pl.BlockSpec((1, 1, d), vec), pl.BlockSpec((1, 1, d), vec),
                  _resident(norm_g.shape), _resident(w_o.shape),
                  _resident(w1.shape), _resident(w3.shape), _resident(w2.shape)],
        out_specs=pl.BlockSpec((1, tm, d), tok),
        out_shape=jax.ShapeDtypeStruct((b, n, d), F32),
        compiler_params=_cparams(("arbitrary", "arbitrary")),
        name="mla_ffn",
    )(x, o, gate1, shift2, scale2, gate2, norm_g, w_o, w1, w3, w2)


def _rg_in_kernel(xp_ref, xm_ref, xn_ref, sh_ref, sc_ref, ng_ref, win_ref, wg_ref, cw_ref, cb_ref,
                  u_ref, gl_ref, *, tm):
    i = pl.program_id(1)
    nt = pl.num_programs(1)
    x_ext = jnp.concatenate([xp_ref[0], xm_ref[0], xn_ref[0]], axis=0)
    h = _prenorm(x_ext, ng_ref[...], sh_ref[0], sc_ref[0]).astype(BF16)
    u_ext = _dot(h, win_ref[...])
    row = lax.broadcasted_iota(jnp.int32, (tm + 2 * SUBLANES, 1), 0)
    valid = jnp.logical_and(jnp.logical_or(row >= SUBLANES, i > 0),
                            jnp.logical_or(row < tm + SUBLANES, i < nt - 1))
    u_ext = jnp.where(valid, u_ext, 0.0)
    acc = cb_ref[...] + cw_ref[0:1, :] * u_ext[6:6 + tm]
    for k in range(1, 4):
        acc = acc + cw_ref[k:k + 1, :] * u_ext[6 + k:6 + k + tm]
    u_ref[0] = acc.astype(BF16)
    gl_ref[0] = jax.nn.gelu(_dot(h[SUBLANES:SUBLANES + tm], wg_ref[...])).astype(BF16)


def _rg_in(x, shift, scale, norm_g, w_in, w_gate, conv_w, conv_b, tm):
    b, n, d = x.shape
    c = w_in.shape[1]
    r = tm // SUBLANES
    last = n // SUBLANES - 1
    vec = lambda bi, i: (bi, 0, 0)
    tok = lambda bi, i: (bi, i, 0)
    return pl.pallas_call(
        functools.partial(_rg_in_kernel, tm=tm),
        grid=(b, n // tm),
        in_specs=[pl.BlockSpec((1, SUBLANES, d), lambda bi, i: (bi, jnp.maximum(i * r - 1, 0), 0)),
                  pl.BlockSpec((1, tm, d), tok),
                  pl.BlockSpec((1, SUBLANES, d), lambda bi, i: (bi, jnp.minimum((i + 1) * r, last), 0)),
                  pl.BlockSpec((1, 1, d), vec), pl.BlockSpec((1, 1, d), vec),
                  _resident(norm_g.shape), _resident(w_in.shape), _resident(w_gate.shape),
                  _resident(conv_w.shape), _resident(conv_b.shape)],
        out_specs=(pl.BlockSpec((1, tm, c), tok), pl.BlockSpec((1, tm, c), tok)),
        out_shape=(jax.ShapeDtypeStruct((b, n, c), BF16), jax.ShapeDtypeStruct((b, n, c), BF16)),
        compiler_params=_cparams(("arbitrary", "arbitrary")),
        name="rg_in",
    )(x, x, x, shift, scale, norm_g, w_in, w_gate, conv_w, conv_b)


def _rg_scan_kernel(u_ref, h0_ref, wa_ref, ba_ref, wx_ref, bx_ref, lam_ref,
                    hout_ref, hfin_ref, a_s, b_s, o_s, h_s, *, reverse, tn, pitch, nb):
    @pl.when(pl.program_id(1) == 0)
    def _():
        h_s[...] = h0_ref[...]

    width = u_ref.shape[2]
    bw = width // RG_BLOCKS
    n_ct = width // LANES
    u = u_ref[...].reshape(nb * tn, width)
    for blk in range(RG_BLOCKS):
        cs = slice(blk * bw, (blk + 1) * bw)
        ub = u[:, cs]
        r = _sigmoid(_dot(ub, wa_ref[blk]) + ba_ref[:, cs])
        ig = _sigmoid(_dot(ub, wx_ref[blk]) + bx_ref[:, cs])
        a = jnp.exp2((-RG_C * LOG2E) * jax.nn.softplus(-lam_ref[:, cs]) * r)
        z = 1.0 - a * a
        bb = (z * lax.rsqrt(jnp.maximum(z, 1e-30))) * (ig * ub.astype(F32))
        for bi in range(nb):
            for half in range(bw // LANES):
                j = blk * (bw // LANES) + half
                rows = slice(bi * tn, (bi + 1) * tn)
                a_s[bi, j * pitch:j * pitch + tn, :] = a[rows, half * LANES:(half + 1) * LANES]
                b_s[bi, j * pitch:j * pitch + tn, :] = bb[rows, half * LANES:(half + 1) * LANES]

    def body(g, hs):
        for s in range(SUBLANES):
            t = g * SUBLANES + s
            if reverse:
                t = tn - 1 - t
            idx = pl.ds(t, n_ct, stride=pitch)
            hs = tuple(a_s[bi, idx, :] * hs[bi] + b_s[bi, idx, :] for bi in range(nb))
            for bi in range(nb):
                o_s[bi, idx, :] = hs[bi]
        return hs

    hs = lax.fori_loop(0, tn // SUBLANES, body, tuple(h_s[bi] for bi in range(nb)))
    for bi in range(nb):
        h_s[bi] = hs[bi]
        hfin_ref[bi] = hs[bi]
        for j in range(n_ct):
            hout_ref[bi, :, j * LANES:(j + 1) * LANES] = o_s[bi, j * pitch:j * pitch + tn, :].astype(BF16)


def _rg_scan(u, h0, w_a, b_a, w_x, b_x, lam, reverse, tn):
    b, n, c = u.shape
    nt = n // tn
    n_ct = c // LANES
    nb = 4 if b % 4 == 0 else (2 if b % 2 == 0 else 1)
    pitch = tn + SUBLANES
    tmap = (lambda bi, i: (bi, nt - 1 - i, 0)) if reverse else (lambda bi, i: (bi, i, 0))
    st = lambda bi, i: (bi, 0, 0)
    scr = pltpu.VMEM((nb, n_ct * pitch, LANES), F32)
    return pl.pallas_call(
        functools.partial(_rg_scan_kernel, reverse=reverse, tn=tn, pitch=pitch, nb=nb),
        grid=(b // nb, nt),
        in_specs=[pl.BlockSpec((nb, tn, c), tmap), pl.BlockSpec((nb, n_ct, LANES), st),
                  _resident(w_a.shape), _resident(b_a.shape), _resident(w_x.shape),
                  _resident(b_x.shape), _resident(lam.shape)],
        out_specs=(pl.BlockSpec((nb, tn, c), tmap), pl.BlockSpec((nb, n_ct, LANES), st)),
        out_shape=(jax.ShapeDtypeStruct((b, n, c), BF16), jax.ShapeDtypeStruct((b, n_ct, LANES), F32)),
        scratch_shapes=[scr, scr, scr, pltpu.VMEM((nb, n_ct, LANES), F32)],
        compiler_params=_cparams(("arbitrary", "arbitrary")),
        name="rg_scan_bwd" if reverse else "rg_scan_fwd",
    )(u, h0, w_a, b_a, w_x, b_x, lam)


def _rg_out_kernel(x_ref, hf_ref, hb_ref, gl_ref, g1_ref, sh_ref, sc_ref, ng_ref, wout_ref, wr_ref,
                   x1_ref, h2_ref, route_ref, *, n_experts):
    y = ((hf_ref[0].astype(F32) + hb_ref[0].astype(F32)) * gl_ref[0].astype(F32)).astype(BF16)
    x1 = x_ref[0] + g1_ref[0] * _dot(y, wout_ref[...])
    x1_ref[0] = x1
    h2 = _prenorm(x1, ng_ref[...], sh_ref[0], sc_ref[0])
    h2_ref[0] = h2
    wr = wr_ref[...]
    w_hi = wr.astype(BF16)
    w_lo = (wr - w_hi.astype(F32)).astype(BF16)
    h_hi = h2.astype(BF16)
    h_lo = (h2 - h_hi.astype(F32)).astype(BF16)
    logits = _dot(h_hi, w_hi) + (_dot(h_lo, w_hi) + _dot(h_hi, w_lo))
    lane = lax.broadcasted_iota(jnp.int32, logits.shape, 1).astype(F32)
    neg = jnp.float32(-jnp.inf)
    lg = jnp.where(lane < n_experts, logits, neg)
    m1 = jnp.max(lg, axis=-1, keepdims=True)
    i1 = jnp.min(jnp.where(lg == m1, lane, float(LANES)), axis=-1, keepdims=True)
    lg2 = jnp.where(lane == i1, neg, lg)
    m2 = jnp.max(lg2, axis=-1, keepdims=True)
    i2 = jnp.min(jnp.where(lg2 == m2, lane, float(LANES)), axis=-1, keepdims=True)
    e2 = jnp.exp(m2 - m1)
    den = 1.0 + e2
    route_ref[0] = (jnp.where(lane == 0.0, i1, 0.0) + jnp.where(lane == 1.0, i2, 0.0)
                    + jnp.where(lane == 2.0, 1.0 / den, 0.0) + jnp.where(lane == 3.0, e2 / den, 0.0))


def _rg_out(x, hf, hb, gl, gate1, shift2, scale2, norm_g, w_out, w_router_pad, n_experts, tm):
    b, n, d = x.shape
    c = hf.shape[2]
    tok = lambda bi, i: (bi, i, 0)
    vec = lambda bi, i: (bi, 0, 0)
    return pl.pallas_call(
        functools.partial(_rg_out_kernel, n_experts=n_experts),
        grid=(b, n // tm),
        in_specs=[pl.BlockSpec((1, tm, d), tok), pl.BlockSpec((1, tm, c), tok), pl.BlockSpec((1, tm, c), tok),
                  pl.BlockSpec((1, tm, c), tok),
                  pl.BlockSpec((1, 1, d), vec), pl.BlockSpec((1, 1, d), vec), pl.BlockSpec((1, 1, d), vec),
                  _resident(norm_g.shape), _resident(w_out.shape), _resident(w_router_pad.shape)],
        out_specs=(pl.BlockSpec((1, tm, d), tok), pl.BlockSpec((1, tm, d), tok),
                   pl.BlockSpec((1, tm, LANES), tok)),
        out_shape=(jax.ShapeDtypeStruct((b, n, d), F32), jax.ShapeDtypeStruct((b, n, d), F32),
                   jax.ShapeDtypeStruct((b, n, LANES), F32)),
        compiler_params=_cparams(("arbitrary", "arbitrary")),
        name="rg_out_router",
    )(x, hf, hb, gl, gate1, shift2, scale2, norm_g, w_out, w_router_pad)


def _route_tables(e_idx, tm, n_experts):
    t = e_idx.shape[0]
    n_slots = TOP_K * t
    n_tiles = n_slots // tm + n_experts
    e_flat = e_idx.T.reshape(n_slots)
    experts = jnp.arange(n_experts, dtype=jnp.int32)
    counts = jnp.sum((e_flat[:, None] == experts[None, :]).astype(jnp.int32), axis=0)
    padded = (counts + tm - 1) // tm * tm
    ends = jnp.cumsum(padded)
    n_used = (ends[-1] // tm).astype(jnp.int32)
    fill_rank = jnp.arange(tm, dtype=jnp.int32)
    fill_keys = jnp.where(fill_rank[None, :] < (padded - counts)[:, None], experts[:, None], n_experts)
    keys = jnp.concatenate([e_flat, fill_keys.reshape(n_experts * tm)])
    src = jnp.argsort(keys, stable=True).astype(jnp.int32)
    p = jnp.arange(n_tiles * tm, dtype=jnp.int32)
    valid = src < n_slots
    tok = jnp.where(valid, src % t, 0)
    dst = jnp.where(valid, src, n_slots + ((p // tm + 1) % 2) * tm + p % tm)
    tile_ids = jnp.arange(n_tiles, dtype=jnp.int32)
    tile_e = jnp.sum((tile_ids[:, None] * tm >= ends[None, :]).astype(jnp.int32), axis=1)
    last_e = jnp.max(jnp.where(tile_ids < n_used, tile_e, 0))
    tile_e = jnp.where(tile_ids < n_used, tile_e, last_e)
    fill_dst = (n_slots + tm + jnp.arange(tm, dtype=jnp.int32))[None, :]
    dst_tbl = jnp.concatenate([fill_dst, dst.reshape(n_tiles, tm)], axis=0)
    tok_tbl = jnp.concatenate([tok.reshape(n_tiles, tm), jnp.zeros((1, tm), jnp.int32)], axis=0)
    return (tile_e, n_used.reshape(1), tok_tbl.reshape(n_tiles + 1, 1, tm), dst_tbl.reshape(n_tiles + 1, 1, tm))


def _moe_group_kernel(te_ref, nu_ref, tok0_ref, tokn_ref, dstp_ref, dstl_ref, h2_hbm, w1_ref, w3_ref, w2_ref,
                      y_hbm, xbuf, ybuf, gsem, ssem, *, tm, n_chunks):
    i = pl.program_id(0)
    n_used = nu_ref[0]
    slot = lax.rem(i, 2)

    def gather_row(tok_ref, s, r):
        pltpu.make_async_copy(h2_hbm.at[pl.ds(tok_ref[0, 0, r], 1)], xbuf.at[s, pl.ds(r, 1)], gsem.at[s]).start()

    def scatter_row(dst_ref, s, r):
        pltpu.make_async_copy(ybuf.at[s, pl.ds(r, 1)], y_hbm.at[pl.ds(dst_ref[0, 0, r], 1)], ssem.at[s]).start()

    def wait_gather(s):
        pltpu.make_async_copy(h2_hbm.at[pl.ds(0, tm)], xbuf.at[s], gsem.at[s]).wait()

    def wait_scatter(s):
        pltpu.make_async_copy(ybuf.at[s], y_hbm.at[pl.ds(0, tm)], ssem.at[s]).wait()

    @pl.when(i == 0)
    def _():
        ybuf[...] = jnp.zeros(ybuf.shape, F32)
        tail = y_hbm.shape[0] - 2 * tm
        pltpu.make_async_copy(ybuf.at[0], y_hbm.at[pl.ds(tail, tm)], ssem.at[0]).start()

        def body(r, carry):
            gather_row(tok0_ref, 0, r)
            return carry
        lax.fori_loop(0, tm, body, 0, unroll=8)

    @pl.when(i < n_used)
    def _():
        wait_gather(slot)
        wait_scatter(slot)
        x = xbuf[slot].astype(BF16)
        hid = w1_ref.shape[2]
        hc = hid // n_chunks
        rows = tm // n_chunks
        y = None
        for c in range(n_chunks):
            for r in range(c * rows, (c + 1) * rows):
                gather_row(tokn_ref, 1 - slot, r)
                scatter_row(dstp_ref, 1 - slot, r)
            a = _dot(x, w1_ref[0, :, c * hc:(c + 1) * hc])
            g = _dot(x, w3_ref[0, :, c * hc:(c + 1) * hc])
            act = (jax.nn.silu(a) * g).astype(BF16)
            yc = _dot(act, w2_ref[0, c * hc:(c + 1) * hc, :])
            y = yc if y is None else y + yc
        ybuf[slot] = y

    @pl.when(i == pl.num_programs(0) - 1)
    def _():
        last = lax.rem(n_used + 1, 2)

        def body(r, carry):
            scatter_row(dstl_ref, last, r)
            return carry
        lax.fori_loop(0, tm, body, 0, unroll=8)
        wait_scatter(last)
        wait_scatter(1 - last)
        wait_gather(1 - last)


def _moe_group(h2, e_idx, w1, w3, w2, tm):
    t, d = h2.shape
    n_e, _, hid = w1.shape
    tile_e, n_used, tok, dst = _route_tables(e_idx, tm, n_e)
    n_tiles = tile_e.shape[0]
    n_chunks = _n_hidden_chunks(hid)
    smem_blk = lambda f: pl.BlockSpec((1, 1, tm), f, memory_space=pltpu.SMEM)
    wspec = lambda shape: pl.BlockSpec((1,) + shape, lambda i, te, nu: (te[i], 0, 0), pipeline_mode=pl.Buffered(1))
    grid_spec = pltpu.PrefetchScalarGridSpec(
        num_scalar_prefetch=2,
        grid=(n_tiles,),
        in_specs=[smem_blk(lambda i, te, nu: (0, 0, 0)),
                  smem_blk(lambda i, te, nu: (i + 1, 0, 0)),
                  smem_blk(lambda i, te, nu: (i, 0, 0)),
                  smem_blk(lambda i, te, nu: (nu[0], 0, 0)),
                  pl.BlockSpec(memory_space=pl.ANY),
                  wspec((d, hid)), wspec((d, hid)), wspec((hid, d))],
        out_specs=pl.BlockSpec(memory_space=pl.ANY),
        scratch_shapes=[pltpu.VMEM((2, tm, d), F32), pltpu.VMEM((2, tm, d), F32),
                        pltpu.SemaphoreType.DMA((2,)), pltpu.SemaphoreType.DMA((2,))],
    )
    return pl.pallas_call(
        functools.partial(_moe_group_kernel, tm=tm, n_chunks=n_chunks),
        grid_spec=grid_spec,
        out_shape=jax.ShapeDtypeStruct((TOP_K * t + 2 * tm, d), F32),
        compiler_params=_cparams(("arbitrary",)),
        name="moe_group",
    )(tile_e, n_used, tok, tok, dst, dst, h2, w1, w3, w2)


def _moe_combine_kernel(x1_ref, y0_ref, y1_ref, route_ref, g2_ref, out_ref):
    r = route_ref[0]
    out_ref[0] = x1_ref[0] + g2_ref[0] * (r[:, 2:3] * y0_ref[...] + r[:, 3:4] * y1_ref[...])


def _moe_combine(x1, y, route, gate2, tm):
    b, n, d = x1.shape
    nt = n // tm
    t_blocks = b * nt
    tok = lambda bi, i: (bi, i, 0)
    return pl.pallas_call(
        _moe_combine_kernel,
        grid=(b, nt),
        in_specs=[pl.BlockSpec((1, tm, d), tok),
                  pl.BlockSpec((tm, d), lambda bi, i: (bi * nt + i, 0)),
                  pl.BlockSpec((tm, d), lambda bi, i: (t_blocks + bi * nt + i, 0)),
                  pl.BlockSpec((1, tm, LANES), tok),
                  pl.BlockSpec((1, 1, d), lambda bi, i: (bi, 0, 0))],
        out_specs=pl.BlockSpec((1, tm, d), tok),
        out_shape=jax.ShapeDtypeStruct((b, n, d), F32),
        compiler_params=_cparams(("arbitrary", "arbitrary")),
        name="moe_combine",
    )(x1, y, y, route, gate2)


def _moe(h2, route, x1, gate2, w1, w3, w2, tm):
    b, n, d = x1.shape
    e_idx = route.reshape(b * n, LANES)[:, :TOP_K].astype(jnp.int32)
    y = _moe_group(h2.reshape(b * n, d), e_idx, w1, w3, w2, tm)
    return _moe_combine(x1, y, route, gate2, tm)


def _prep_mla_weights(w_dq, g_q, w_uq, w_dkv, g_kv, w_ukv, g_qn, g_kn):
    ql = w_uq.shape[0]
    half = D_ROPE // 2
    pad = LANES - D_ROPE

    def ext_cols(w_rope):
        z = jnp.zeros(w_rope.shape[:-1] + (pad,), w_rope.dtype)
        r1, r2 = w_rope[..., :half], w_rope[..., half:]
        return jnp.concatenate([w_rope, z], -1), jnp.concatenate([-r2, r1, z], -1)

    def ext_gain(g):
        z = jnp.zeros((pad,), g.dtype)
        gr1, gr2 = g[D_NOPE:D_NOPE + half], g[D_NOPE + half:]
        return jnp.concatenate([g[:D_NOPE], gr1, gr2, z, gr2, gr1, z])[None, :].astype(F32)

    wq = w_uq.reshape(ql, N_HEADS, D_QK)
    qa, qb = ext_cols(wq[:, :, D_NOPE:])
    w_uq_ext = jnp.concatenate([wq[:, :, :D_NOPE], qa, qb], -1).reshape(ql, N_HEADS * Q_EXT)
    kl = w_ukv.shape[0]
    ka, kb = ext_cols(w_dkv[:, kl:])
    w_dkv_ext = jnp.concatenate([w_dkv[:, :kl], ka, kb], -1)
    wkv = w_ukv.reshape(kl, N_HEADS, D_NOPE + D_V)
    w_uk = wkv[:, :, :D_NOPE].reshape(kl, N_HEADS * D_NOPE)
    w_uvt = wkv[:, :, D_NOPE:].reshape(kl, N_HEADS * D_V).T
    return dict(w_dq=w_dq.astype(BF16), g_q=g_q[None, :].astype(F32), w_uq=w_uq_ext.astype(BF16),
                g_qn=ext_gain(g_qn), w_dkv=w_dkv_ext.astype(BF16), g_kv=g_kv[None, :].astype(F32),
                w_uk=w_uk.astype(BF16), w_uvt=w_uvt.astype(BF16), g_kn=ext_gain(g_kn))


def _rope_tables(n_tok):
    rows = n_tok // GRID_W
    row = jnp.repeat(jnp.arange(rows, dtype=F32), GRID_W)
    col = jnp.tile(jnp.arange(GRID_W, dtype=F32), rows)
    n_freq = D_ROPE // 4
    inv = ROPE_BASE ** (-jnp.arange(n_freq, dtype=F32) / n_freq)
    ang = jnp.concatenate([row[:, None] * inv, col[:, None] * inv], axis=-1)
    z = jnp.zeros((n_tok, LANES - D_ROPE), F32)
    cos, sin = jnp.cos(ang), jnp.sin(ang)
    return jnp.concatenate([cos, cos, z], -1), jnp.concatenate([sin, sin, z], -1)


def _no_rope_tables(n_tok):
    one = jnp.ones((n_tok, D_ROPE), F32)
    z = jnp.zeros((n_tok, LANES - D_ROPE), F32)
    return jnp.concatenate([one, z], -1), jnp.zeros((n_tok, LANES), F32)


def _n_hidden_chunks(hid):
    for c in range(1, hid // MXU_COLS + 1):
        if hid % (c * MXU_COLS) == 0 and hid // c <= MAX_HIDDEN_CHUNK:
            return c
    return 1


def _tile(n, target):
    t = min(n, target)
    assert n % t == 0, (n, t)
    return t


def _mla_layer(x, ctx, mods_l, mods_c, norm1_g, norm2_g, w, w_o, ffn, need_ctx):
    b, n, d = x.shape
    nc = ctx.shape[1]
    tm = _tile(n, 512)
    tmc = _tile(nc, 512)
    sh1, sc1, g1, sh2, sc2, g2 = mods_l
    csh1, csc1, cg1, csh2, csc2, cg2 = mods_c
    cos_l, sin_l = _rope_tables(n)
    cos_c, sin_c = _no_rope_tables(nc)
    ql, kl, vtl = _mla_pre(x, sh1, sc1, norm1_g, cos_l, sin_l, w, tm)
    qc, kc, vtc = _mla_pre(ctx, csh1, csc1, norm1_g, cos_c, sin_c, w, tmc)
    k5 = lambda k, t: k.reshape(k.shape[0], k.shape[1], k.shape[2] // t, t, k.shape[3])
    src_l = (k5(kl, tm), vtl)
    src_c = (k5(kc, tmc), vtc)
    ol = _attention(ql, [src_l, src_c], _tile(n, 2048))
    x_new = _mla_ffn(x, ol, g1, sh2, sc2, g2, norm2_g, w_o, *ffn, tm)
    ctx_new = ctx
    if need_ctx:
        oc = _attention(qc, [src_c], _tile(nc, 1024))
        ctx_new = _mla_ffn(ctx, oc, cg1, csh2, csc2, cg2, norm2_g, w_o, *ffn, tmc)
    return x_new, ctx_new


def _rg_layer(x, ctx, mods_l, mods_c, norm1_g, norm2_g, rg, moe, need_ctx):
    b, n, d = x.shape
    nc = ctx.shape[1]
    tm = _tile(n, 512)
    tmc = _tile(nc, 512)
    tn = _tile(n, 256)
    tnc = _tile(nc, 256)
    sh1, sc1, g1, sh2, sc2, g2 = mods_l
    csh1, csc1, cg1, csh2, csc2, cg2 = mods_c
    w_in, w_gate, conv_w, conv_b, w_a, b_a, w_x, b_x, lam, w_out = rg
    w_router_pad, mw1, mw3, mw2, n_experts = moe
    ul, gll = _rg_in(x, sh1, sc1, norm1_g, w_in, w_gate, conv_w, conv_b, tm)
    uc, glc = _rg_in(ctx, csh1, csc1, norm1_g, w_in, w_gate, conv_w, conv_b, tmc)
    zero = jnp.zeros((b, w_in.shape[1] // LANES, LANES), F32)
    hl, hc = [], []
    for dr in range(2):
        args = (w_a[dr], b_a[dr], w_x[dr], b_x[dr], lam[dr])
        hcd, h_end = _rg_scan(uc, zero, *args, reverse=bool(dr), tn=tnc)
        hld, _ = _rg_scan(ul, h_end, *args, reverse=bool(dr), tn=tn)
        hl.append(hld)
        hc.append(hcd)
    x1, h2, route = _rg_out(x, hl[0], hl[1], gll, g1, sh2, sc2, norm2_g, w_out, w_router_pad, n_experts, tm)
    x_new = _moe(h2, route, x1, g2, mw1, mw3, mw2, tm)
    ctx_new = ctx
    if need_ctx:
        c1, ch2, croute = _rg_out(ctx, hc[0], hc[1], glc, cg1, csh2, csc2, norm2_g, w_out, w_router_pad,
                                  n_experts, tmc)
        ctx_new = _moe(ch2, croute, c1, cg2, mw1, mw3, mw2, tmc)
    return x_new, ctx_new


def kernel(x, c, ctx, c_ctx, ada_w, ada_b, norm1_g, norm2_g, mla_w_dq, mla_g_q, mla_w_uq, mla_w_dkv, mla_g_kv, mla_w_ukv, mla_g_qn, mla_g_kn, mla_w_o, rg_w_in, rg_w_gate, rg_conv_w, rg_conv_b, rg_w_a, rg_b_a, rg_w_x, rg_b_x, rg_lam, rg_w_out, ffn_w1, ffn_w3, ffn_w2, moe_w_router, moe_w1, moe_w3, moe_w2):
    b, n, d = x.shape
    depth = ada_w.shape[0]
    n_experts = moe_w_router.shape[2]

    rows = 2 * SUBLANES
    cvec = jnp.concatenate([c, c_ctx[None, :], jnp.zeros((rows - b - 1, d), F32)], axis=0)
    mods = _ada_all(cvec, ada_w, ada_b)

    for i in range(depth):
        need_ctx = i < depth - 1
        j = i // 2
        chunks = [mods[i, :, k * d:(k + 1) * d] for k in range(6)]
        mods_l = [m[:b, None, :] for m in chunks]
        mods_c = [jnp.broadcast_to(m[b:b + 1, None, :], (b, 1, d)) for m in chunks]
        n1 = norm1_g[i][None, :]
        n2 = norm2_g[i][None, :]
        if i % 2 == 0:
            w = _prep_mla_weights(mla_w_dq[j], mla_g_q[j], mla_w_uq[j], mla_w_dkv[j], mla_g_kv[j],
                                  mla_w_ukv[j], mla_g_qn[j], mla_g_kn[j])
            ffn = (ffn_w1[j].astype(BF16), ffn_w3[j].astype(BF16), ffn_w2[j].astype(BF16))
            x, ctx = _mla_layer(x, ctx, mods_l, mods_c, n1, n2, w, mla_w_o[j].astype(BF16), ffn, need_ctx)
        else:
            rg = (rg_w_in[j].astype(BF16), rg_w_gate[j].astype(BF16), rg_conv_w[j], rg_conv_b[j][None, :],
                  rg_w_a[j].astype(BF16), rg_b_a[j][:, None, :], rg_w_x[j].astype(BF16), rg_b_x[j][:, None, :],
                  rg_lam[j][:, None, :], rg_w_out[j].astype(BF16))
            w_router_pad = jnp.concatenate(
                [moe_w_router[j], jnp.zeros((d, LANES - n_experts), F32)], axis=1)
            moe = (w_router_pad, moe_w1[j].astype(BF16), moe_w3[j].astype(BF16), moe_w2[j].astype(BF16), n_experts)
            x, ctx = _rg_layer(x, ctx, mods_l, mods_c, n1, n2, rg, moe, need_ctx)
    return x
```

```python
import functools
import math

import jax
import jax.numpy as jnp
from jax import lax
from jax.experimental import pallas as pl
from jax.experimental.pallas import tpu as pltpu

F32 = jnp.float32
BF16 = jnp.bfloat16

EPS = 1e-6
GRID_W = 64
N_HEADS = 8
D_NOPE = 128
D_ROPE = 64
D_V = 128
D_QK = D_NOPE + D_ROPE
D_HEAD_PAD = 256
D_VX = D_V + 16
Q_EXT = 384
ROPE_BASE = 10000.0
RG_BLOCKS = 4
RG_C = 8.0
TOP_K = 2
LANES = 128
SUBLANES = 8
MXU_COLS = 256
MAX_HIDDEN_CHUNK = 3072
VMEM_LIMIT = 56 * 1024 * 1024
LOG2E = 1.4426950408889634


def _cparams(sem):
    return pltpu.CompilerParams(dimension_semantics=sem, vmem_limit_bytes=VMEM_LIMIT)


def _resident(shape):
    nd = len(shape)
    return pl.BlockSpec(shape, lambda *_: (0,) * nd, pipeline_mode=pl.Buffered(1))


def _dot(a, b):
    return jnp.dot(a, b, preferred_element_type=F32)


def _sigmoid(x):
    return 0.5 * jnp.tanh(0.5 * x) + 0.5


def _rms(xf, g):
    return xf * lax.rsqrt(jnp.mean(xf * xf, axis=-1, keepdims=True) + EPS) * g


def _prenorm(xf, g, shift, scale):
    return _rms(xf, g) * (1.0 + scale) + shift


def _ada_kernel(c_ref, w_ref, b_ref, o_ref):
    s = jax.nn.silu(c_ref[...])
    o_ref[0] = jnp.dot(s, w_ref[0], precision=lax.Precision.HIGHEST,
                       preferred_element_type=F32) + b_ref[0]


def _ada_all(cvec, ada_w, ada_b):
    depth, d, n6 = ada_w.shape
    rows = cvec.shape[0]
    tn = 1536
    return pl.pallas_call(
        _ada_kernel,
        grid=(depth, n6 // tn),
        in_specs=[pl.BlockSpec((rows, d), lambda l, j: (0, 0)),
                  pl.BlockSpec((1, d, tn), lambda l, j: (l, 0, j)),
                  pl.BlockSpec((1, 1, tn), lambda l, j: (l, 0, j))],
        out_specs=pl.BlockSpec((1, rows, tn), lambda l, j: (l, 0, j)),
        out_shape=jax.ShapeDtypeStruct((depth, rows, n6), F32),
        compiler_params=_cparams(("arbitrary", "arbitrary")),
        name="ada_mod",
    )(cvec, ada_w, ada_b.reshape(depth, 1, n6))


def _mla_pre_kernel(*refs, q_scale, n_x):
    (sh_ref, sc_ref, ng_ref, cos_ref, sin_ref, wdq_ref, gq_ref, wuq_ref, gqn_ref,
     wdkv_ref, gkv_ref, wuk_ref, wuvt_ref, gkn_ref, q_ref, k_ref, vt_ref) = refs[n_x:n_x + 17]
    if n_x == 1:
        x = refs[0][0]
    else:
        x1_ref, y0_ref, y1_ref, route_ref, g2_ref = refs[:n_x]
        x = _moe_combined(x1_ref, y0_ref, y1_ref, route_ref, g2_ref)
        refs[n_x + 17][0] = x
    h = _prenorm(x, ng_ref[...], sh_ref[0], sc_ref[0]).astype(BF16)
    cosv = cos_ref[...]
    sinv = sin_ref[...]

    qn = _rms(_dot(h, wdq_ref[...]), gq_ref[...]).astype(BF16)
    qall = _dot(qn, wuq_ref[...])
    g_n = gqn_ref[:, 0:LANES]
    g_a = gqn_ref[:, LANES:2 * LANES]
    g_b = gqn_ref[:, 2 * LANES:3 * LANES]
    for hh in range(N_HEADS):
        base = hh * Q_EXT
        nope = qall[:, base:base + LANES]
        ra = qall[:, base + LANES:base + 2 * LANES]
        rb = qall[:, base + 2 * LANES:base + 3 * LANES]
        ss = jnp.sum(nope * nope, axis=-1, keepdims=True) + jnp.sum(ra * ra, axis=-1, keepdims=True)
        inv = lax.rsqrt(ss * (1.0 / D_QK) + EPS) * q_scale
        q_ref[0, hh, :, 0:LANES] = (nope * g_n * inv).astype(BF16)
        q_ref[0, hh, :, LANES:2 * LANES] = ((ra * g_a * cosv + rb * g_b * sinv) * inv).astype(BF16)

    kva = _dot(h, wdkv_ref[...])
    ckv = _rms(kva[:, 0:LANES], gkv_ref[...]).astype(BF16)
    pa = kva[:, LANES:2 * LANES]
    pb = kva[:, 2 * LANES:3 * LANES]
    pe_ss = jnp.sum(pa * pa, axis=-1, keepdims=True)
    k_n = gkn_ref[:, 0:LANES]
    k_a = gkn_ref[:, LANES:2 * LANES]
    k_b = gkn_ref[:, 2 * LANES:3 * LANES]
    rope = pa * k_a * cosv + pb * k_b * sinv
    knope = _dot(ckv, wuk_ref[...])
    vt_all = lax.dot_general(wuvt_ref[...], ckv, (((1,), (1,)), ((), ())),
                             preferred_element_type=F32)
    for hh in range(N_HEADS):
        kn = knope[:, hh * LANES:(hh + 1) * LANES]
        ss = jnp.sum(kn * kn, axis=-1, keepdims=True) + pe_ss
        inv = lax.rsqrt(ss * (1.0 / D_QK) + EPS)
        k_ref[0, hh, :, 0:LANES] = (kn * k_n * inv).astype(BF16)
        k_ref[0, hh, :, LANES:2 * LANES] = (rope * inv).astype(BF16)
        vt_ref[0, hh, 0, 0:D_V, :] = vt_all[hh * D_V:(hh + 1) * D_V, :].astype(BF16)
        vt_ref[0, hh, 0, D_V:D_VX, :] = jnp.ones((D_VX - D_V, vt_all.shape[1]), BF16)


def _moe_combined(x1_ref, y0_ref, y1_ref, route_ref, g2_ref):
    r = route_ref[0]
    return x1_ref[0] + g2_ref[0] * (r[:, 2:3] * y0_ref[...] + r[:, 3:4] * y1_ref[...])


def _moe_pending_specs(b, n, d, tm):
    nt = n // tm
    tok = lambda bi, i: (bi, i, 0)
    return [pl.BlockSpec((1, tm, d), tok),
            pl.BlockSpec((tm, d), lambda bi, i: (bi * nt + i, 0)),
            pl.BlockSpec((tm, d), lambda bi, i: (b * nt + bi * nt + i, 0)),
            pl.BlockSpec((1, tm, LANES), tok),
            pl.BlockSpec((1, 1, d), lambda bi, i: (bi, 0, 0))]


def _mla_pre(x, shift, scale, norm_g, cos_t, sin_t, w, tm):
    pending = isinstance(x, tuple)
    b, n, d = x[0].shape if pending else x.shape
    nt = n // tm
    q_scale = (D_QK ** -0.5) * LOG2E
    tok = lambda bi, i: (bi, i, 0)
    vec = lambda bi, i: (bi, 0, 0)
    out_shapes = (jax.ShapeDtypeStruct((b, N_HEADS, n, D_HEAD_PAD), BF16),
                  jax.ShapeDtypeStruct((b, N_HEADS, n, D_HEAD_PAD), BF16),
                  jax.ShapeDtypeStruct((b, N_HEADS, nt, D_VX, tm), BF16))
    weights = (w["w_dq"], w["g_q"], w["w_uq"], w["g_qn"], w["w_dkv"], w["g_kv"], w["w_uk"], w["w_uvt"], w["g_kn"])
    out_specs = (pl.BlockSpec((1, N_HEADS, tm, D_HEAD_PAD), lambda bi, i: (bi, 0, i, 0)),
                 pl.BlockSpec((1, N_HEADS, tm, D_HEAD_PAD), lambda bi, i: (bi, 0, i, 0)),
                 pl.BlockSpec((1, N_HEADS, 1, D_VX, tm), lambda bi, i: (bi, 0, i, 0, 0)))
    if pending:
        x1, y, route, gate2 = x
        x_args = (x1, y, y, route, gate2)
        x_specs = _moe_pending_specs(b, n, d, tm)
        out_specs += (pl.BlockSpec((1, tm, d), tok),)
        out_shapes += (jax.ShapeDtypeStruct((b, n, d), F32),)
    else:
        x_args = (x,)
        x_specs = [pl.BlockSpec((1, tm, d), tok)]
    return pl.pallas_call(
        functools.partial(_mla_pre_kernel, q_scale=q_scale, n_x=len(x_args)),
        grid=(b, nt),
        in_specs=x_specs + [
                  pl.BlockSpec((1, 1, d), vec), pl.BlockSpec((1, 1, d), vec),
                  _resident(norm_g.shape),
                  pl.BlockSpec((tm, LANES), lambda bi, i: (i, 0)),
                  pl.BlockSpec((tm, LANES), lambda bi, i: (i, 0))]
                 + [_resident(a.shape) for a in weights],
        out_specs=out_specs,
        out_shape=out_shapes,
        compiler_params=_cparams(("arbitrary", "arbitrary")),
        name="mla_pre",
    )(*x_args, shift, scale, norm_g, cos_t, sin_t, *weights)


def _attn_kernel(q_ref, *refs, n_src):
    srcs = [(refs[2 * s], refs[2 * s + 1]) for s in range(n_src)]
    o_ref = refs[2 * n_src]
    acc_ref = refs[2 * n_src + 1]
    s_bufs = refs[2 * n_src + 2:]

    q_t = q_ref[0, 0].astype(F32).T.astype(BF16)
    acc_ref[...] = jnp.zeros(acc_ref.shape, F32)

    def produce(kc, s_ref, m_prev):
        s = _dot(kc, q_t)
        s_ref[...] = s
        return jnp.maximum(m_prev, jnp.max(s, axis=0, keepdims=True))

    def consume(s_ref, vtc, m_cur, m_prev):
        alpha = jnp.exp2(m_prev - m_cur)
        p = jnp.exp2(s_ref[...] - m_cur).astype(BF16)
        acc_ref[...] = alpha * acc_ref[...] + _dot(vtc, p)

    m_init = jnp.full((1, acc_ref.shape[1]), -1e30, F32)
    k_ref, vt_ref = srcs[0]
    n = k_ref.shape[2]
    if n == 1:
        m_cur = produce(k_ref[0, 0, 0], s_bufs[0], m_init)
        m_prev = m_init
        pending = (s_bufs[0], vt_ref, 0)
    else:
        s_a, s_b = s_bufs[0], s_bufs[1]
        m0 = produce(k_ref[0, 0, 0], s_a, m_init)

        def body(jj, carry):
            m_prev, m_cur = carry
            j = 2 * jj
            m_1 = produce(k_ref[0, 0, j + 1], s_b, m_cur)
            consume(s_a, vt_ref[0, 0, j], m_cur, m_prev)
            m_2 = produce(k_ref[0, 0, j + 2], s_a, m_1)
            consume(s_b, vt_ref[0, 0, j + 1], m_1, m_cur)
            return m_1, m_2

        m_prev, m_cur = lax.fori_loop(0, n // 2 - 1, body, (m_init, m0))
        m_1 = produce(k_ref[0, 0, n - 1], s_b, m_cur)
        consume(s_a, vt_ref[0, 0, n - 2], m_cur, m_prev)
        m_prev, m_cur = m_cur, m_1
        pending = (s_b, vt_ref, n - 1)

    for kx_ref, vtx_ref in srcs[1:]:
        m_x = produce(kx_ref[0, 0, 0], s_bufs[-1], m_cur)
        consume(pending[0], pending[1][0, 0, pending[2]], m_cur, m_prev)
        m_prev, m_cur = m_cur, m_x
        pending = (s_bufs[-1], vtx_ref, 0)
    consume(pending[0], pending[1][0, 0, pending[2]], m_cur, m_prev)

    o = acc_ref[0:D_V, :] * (1.0 / acc_ref[D_V:D_V + 1, :])
    o_ref[0] = o.T.astype(BF16)


def _attention(q, srcs, tq):
    b, h, nq, dp = q.shape
    in_specs = [pl.BlockSpec((1, 1, tq, dp), lambda bi, hi, i: (bi, hi, i, 0))]
    args = [q]
    for k5, vt5 in srcs:
        in_specs.append(pl.BlockSpec((1, 1) + k5.shape[2:], lambda bi, hi, i: (bi, hi, 0, 0, 0)))
        in_specs.append(pl.BlockSpec((1, 1) + vt5.shape[2:], lambda bi, hi, i: (bi, hi, 0, 0, 0)))
        args += [k5, vt5]
    n0, tk0 = srcs[0][0].shape[2:4]
    assert n0 == 1 or n0 % 2 == 0, n0
    assert all(k5.shape[2] == 1 for k5, _ in srcs[1:])
    s_shapes = [pltpu.VMEM((tk0, tq), F32)] * (1 if n0 == 1 else 2)
    s_shapes += [pltpu.VMEM((k5.shape[3], tq), F32) for k5, _ in srcs[1:2]]
    return pl.pallas_call(
        functools.partial(_attn_kernel, n_src=len(srcs)),
        grid=(b, h, nq // tq),
        in_specs=in_specs,
        out_specs=pl.BlockSpec((1, tq, D_V), lambda bi, hi, i: (bi, i, hi)),
        out_shape=jax.ShapeDtypeStruct((b, nq, h * D_V), BF16),
        scratch_shapes=[pltpu.VMEM((D_VX, tq), F32)] + s_shapes,
        compiler_params=_cparams(("arbitrary", "arbitrary", "arbitrary")),
        name="mla_attn",
    )(*args)


def _mla_ffn_kernel(x_ref, o_ref, g1_ref, sh_ref, sc_ref, g2_ref, ng_ref,
                    wo_ref, w1_ref, w3_ref, w2_ref, out_ref, *, n_chunks):
    x1 = x_ref[0] + g1_ref[0] * _dot(o_ref[0], wo_ref[...])
    h2 = _prenorm(x1, ng_ref[...], sh_ref[0], sc_ref[0]).astype(BF16)
    hid = w1_ref.shape[1]
    hc = hid // n_chunks
    y = None
    for c in range(n_chunks):
        a = _dot(h2, w1_ref[:, c * hc:(c + 1) * hc])
        g = _dot(h2, w3_ref[:, c * hc:(c + 1) * hc])
        act = (jax.nn.silu(a) * g).astype(BF16)
        yc = _dot(act, w2_ref[c * hc:(c + 1) * hc, :])
        y = yc if y is None else y + yc
    out_ref[0] = x1 + g2_ref[0] * y


def _mla_ffn(x, o, gate1, shift2, scale2, gate2, norm_g, w_o, w1, w3, w2, tm):
    b, n, d = x.shape
    hid = w1.shape[1]
    n_chunks = _n_hidden_chunks(hid)
    tok = lambda bi, i: (bi, i, 0)
    vec = lambda bi, i: (bi, 0, 0)
    return pl.pallas_call(
        functools.partial(_mla_ffn_kernel, n_chunks=n_chunks),
        grid=(b, n // tm),
        in_specs=[pl.BlockSpec((1, tm, d), tok), pl.BlockSpec((1, tm, o.shape[2]), tok),
                  pl.BlockSpec((1, 1, d), vec), pl.BlockSpec((1, 1, d), vec),
                  pl.BlockSpec((1, 1, d), vec), pl.BlockSpec((1, 1, d), vec),
                  _resident(norm_g.shape), _resident(w_o.shape),
                  _resident(w1.shape), _resident(w3.shape), _resident(w2.shape)],
        out_specs=pl.BlockSpec((1, tm, d), tok),
        out_shape=jax.ShapeDtypeStruct((b, n, d), F32),
        compiler_params=_cparams(("arbitrary", "arbitrary")),
        name="mla_ffn",
    )(x, o, gate1, shift2, scale2, gate2, norm_g, w_o, w1, w3, w2)


def _rg_in_kernel(xp_ref, xm_ref, xn_ref, sh_ref, sc_ref, ng_ref, win_ref, wg_ref, cw_ref, cb_ref,
                  u_ref, gl_ref, *, tm):
    i = pl.program_id(1)
    nt = pl.num_programs(1)
    x_ext = jnp.concatenate([xp_ref[0], xm_ref[0], xn_ref[0]], axis=0)
    h = _prenorm(x_ext, ng_ref[...], sh_ref[0], sc_ref[0]).astype(BF16)
    u_ext = _dot(h, win_ref[...])
    row = lax.broadcasted_iota(jnp.int32, (tm + 2 * SUBLANES, 1), 0)
    valid = jnp.logical_and(jnp.logical_or(row >= SUBLANES, i > 0),
                            jnp.logical_or(row < tm + SUBLANES, i < nt - 1))
    u_ext = jnp.where(valid, u_ext, 0.0)
    acc = cb_ref[...] + cw_ref[0:1, :] * u_ext[6:6 + tm]
    for k in range(1, 4):
        acc = acc + cw_ref[k:k + 1, :] * u_ext[6 + k:6 + k + tm]
    u_ref[0] = acc.astype(BF16)
    gl_ref[0] = jax.nn.gelu(_dot(h[SUBLANES:SUBLANES + tm], wg_ref[...])).astype(BF16)


def _rg_in(x, shift, scale, norm_g, w_in, w_gate, conv_w, conv_b, tm):
    b, n, d = x.shape
    c = w_in.shape[1]
    r = tm // SUBLANES
    last = n // SUBLANES - 1
    vec = lambda bi, i: (bi, 0, 0)
    tok = lambda bi, i: (bi, i, 0)
    return pl.pallas_call(
        functools.partial(_rg_in_kernel, tm=tm),
        grid=(b, n // tm),
        in_specs=[pl.BlockSpec((1, SUBLANES, d), lambda bi, i: (bi, jnp.maximum(i * r - 1, 0), 0)),
                  pl.BlockSpec((1, tm, d), tok),
                  pl.BlockSpec((1, SUBLANES, d), lambda bi, i: (bi, jnp.minimum((i + 1) * r, last), 0)),
                  pl.BlockSpec((1, 1, d), vec), pl.BlockSpec((1, 1, d), vec),
                  _resident(norm_g.shape), _resident(w_in.shape), _resident(w_gate.shape),
                  _resident(conv_w.shape), _resident(conv_b.shape)],
        out_specs=(pl.BlockSpec((1, tm, c), tok), pl.BlockSpec((1, tm, c), tok)),
        out_shape=(jax.ShapeDtypeStruct((b, n, c), BF16), jax.ShapeDtypeStruct((b, n, c), BF16)),
        compiler_params=_cparams(("arbitrary", "arbitrary")),
        name="rg_in",
    )(x, x, x, shift, scale, norm_g, w_in, w_gate, conv_w, conv_b)


def _rg_scan_kernel(u_ref, h0_ref, wa_ref, ba_ref, wx_ref, bx_ref, lam_ref,
                    hout_ref, hfin_ref, a_s, b_s, o_s, h_s, *, reverse, tn, pitch, nb):
    @pl.when(pl.program_id(1) == 0)
    def _():
        h_s[...] = h0_ref[...]

    width = u_ref.shape[2]
    bw = width // RG_BLOCKS
    n_ct = width // LANES
    u = u_ref[...].reshape(nb * tn, width)
    for blk in range(RG_BLOCKS):
        cs = slice(blk * bw, (blk + 1) * bw)
        ub = u[:, cs]
        r = _sigmoid(_dot(ub, wa_ref[blk]) + ba_ref[:, cs])
        ig = _sigmoid(_dot(ub, wx_ref[blk]) + bx_ref[:, cs])
        a = jnp.exp2((-RG_C * LOG2E) * jax.nn.softplus(-lam_ref[:, cs]) * r)
        z = 1.0 - a * a
        bb = (z * lax.rsqrt(jnp.maximum(z, 1e-30))) * (ig * ub.astype(F32))
        for bi in range(nb):
            for half in range(bw // LANES):
                j = blk * (bw // LANES) + half
                rows = slice(bi * tn, (bi + 1) * tn)
                a_s[bi, j * pitch:j * pitch + tn, :] = a[rows, half * LANES:(half + 1) * LANES]
                b_s[bi, j * pitch:j * pitch + tn, :] = bb[rows, half * LANES:(half + 1) * LANES]

    def body(g, hs):
        for s in range(SUBLANES):
            t = g * SUBLANES + s
            if reverse:
                t = tn - 1 - t
            idx = pl.ds(t, n_ct, stride=pitch)
            hs = tuple(a_s[bi, idx, :] * hs[bi] + b_s[bi, idx, :] for bi in range(nb))
            for bi in range(nb):
                o_s[bi, idx, :] = hs[bi]
        return hs

    hs = lax.fori_loop(0, tn // SUBLANES, body, tuple(h_s[bi] for bi in range(nb)))
    for bi in range(nb):
        h_s[bi] = hs[bi]
        hfin_ref[bi] = hs[bi]
        for j in range(n_ct):
            hout_ref[bi, :, j * LANES:(j + 1) * LANES] = o_s[bi, j * pitch:j * pitch + tn, :].astype(BF16)


def _rg_scan(u, h0, w_a, b_a, w_x, b_x, lam, reverse, tn):
    b, n, c = u.shape
    nt = n // tn
    n_ct = c // LANES
    nb = 4 if b % 4 == 0 else (2 if b % 2 == 0 else 1)
    pitch = tn + SUBLANES
    tmap = (lambda bi, i: (bi, nt - 1 - i, 0)) if reverse else (lambda bi, i: (bi, i, 0))
    st = lambda bi, i: (bi, 0, 0)
    scr = pltpu.VMEM((nb, n_ct * pitch, LANES), F32)
    return pl.pallas_call(
        functools.partial(_rg_scan_kernel, reverse=reverse, tn=tn, pitch=pitch, nb=nb),
        grid=(b // nb, nt),
        in_specs=[pl.BlockSpec((nb, tn, c), tmap), pl.BlockSpec((nb, n_ct, LANES), st),
                  _resident(w_a.shape), _resident(b_a.shape), _resident(w_x.shape),
                  _resident(b_x.shape), _resident(lam.shape)],
        out_specs=(pl.BlockSpec((nb, tn, c), tmap), pl.BlockSpec((nb, n_ct, LANES), st)),
        out_shape=(jax.ShapeDtypeStruct((b, n, c), BF16), jax.ShapeDtypeStruct((b, n_ct, LANES), F32)),
        scratch_shapes=[scr, scr, scr, pltpu.VMEM((nb, n_ct, LANES), F32)],
        compiler_params=_cparams(("arbitrary", "arbitrary")),
        name="rg_scan_bwd" if reverse else "rg_scan_fwd",
    )(u, h0, w_a, b_a, w_x, b_x, lam)


def _rg_out_kernel(x_ref, hf_ref, hb_ref, gl_ref, g1_ref, sh_ref, sc_ref, ng_ref, wout_ref, wr_ref,
                   x1_ref, h2_ref, route_ref, *, n_experts):
    y = ((hf_ref[0].astype(F32) + hb_ref[0].astype(F32)) * gl_ref[0].astype(F32)).astype(BF16)
    x1 = x_ref[0] + g1_ref[0] * _dot(y, wout_ref[...])
    x1_ref[0] = x1
    h2 = _prenorm(x1, ng_ref[...], sh_ref[0], sc_ref[0])
    h2_ref[0] = h2
    wr = wr_ref[...]
    w_hi = wr.astype(BF16)
    w_lo = (wr - w_hi.astype(F32)).astype(BF16)
    h_hi = h2.astype(BF16)
    h_lo = (h2 - h_hi.astype(F32)).astype(BF16)
    logits = _dot(h_hi, w_hi) + (_dot(h_lo, w_hi) + _dot(h_hi, w_lo))
    lane = lax.broadcasted_iota(jnp.int32, logits.shape, 1).astype(F32)
    neg = jnp.float32(-jnp.inf)
    lg = jnp.where(lane < n_experts, logits, neg)
    m1 = jnp.max(lg, axis=-1, keepdims=True)
    i1 = jnp.min(jnp.where(lg == m1, lane, float(LANES)), axis=-1, keepdims=True)
    lg2 = jnp.where(lane == i1, neg, lg)
    m2 = jnp.max(lg2, axis=-1, keepdims=True)
    i2 = jnp.min(jnp.where(lg2 == m2, lane, float(LANES)), axis=-1, keepdims=True)
    e2 = jnp.exp(m2 - m1)
    den = 1.0 + e2
    route_ref[0] = (jnp.where(lane == 0.0, i1, 0.0) + jnp.where(lane == 1.0, i2, 0.0)
                    + jnp.where(lane == 2.0, 1.0 / den, 0.0) + jnp.where(lane == 3.0, e2 / den, 0.0))


def _rg_out(x, hf, hb, gl, gate1, shift2, scale2, norm_g, w_out, w_router_pad, n_experts, tm):
    b, n, d = x.shape
    c = hf.shape[2]
    tok = lambda bi, i: (bi, i, 0)
    vec = lambda bi, i: (bi, 0, 0)
    return pl.pallas_call(
        functools.partial(_rg_out_kernel, n_experts=n_experts),
        grid=(b, n // tm),
        in_specs=[pl.BlockSpec((1, tm, d), tok), pl.BlockSpec((1, tm, c), tok), pl.BlockSpec((1, tm, c), tok),
                  pl.BlockSpec((1, tm, c), tok),
                  pl.BlockSpec((1, 1, d), vec), pl.BlockSpec((1, 1, d), vec), pl.BlockSpec((1, 1, d), vec),
                  _resident(norm_g.shape), _resident(w_out.shape), _resident(w_router_pad.shape)],
        out_specs=(pl.BlockSpec((1, tm, d), tok), pl.BlockSpec((1, tm, d), tok),
                   pl.BlockSpec((1, tm, LANES), tok)),
        out_shape=(jax.ShapeDtypeStruct((b, n, d), F32), jax.ShapeDtypeStruct((b, n, d), F32),
                   jax.ShapeDtypeStruct((b, n, LANES), F32)),
        compiler_params=_cparams(("arbitrary", "arbitrary")),
        name="rg_out_router",
    )(x, hf, hb, gl, gate1, shift2, scale2, norm_g, w_out, w_router_pad)


def _route_tables(e_idx, tm, n_experts):
    t = e_idx.shape[0]
    n_slots = TOP_K * t
    n_tiles = n_slots // tm + n_experts
    e_flat = e_idx.T.reshape(n_slots)
    experts = jnp.arange(n_experts, dtype=jnp.int32)
    counts = jnp.sum((e_flat[:, None] == experts[None, :]).astype(jnp.int32), axis=0)
    padded = (counts + tm - 1) // tm * tm
    ends = jnp.cumsum(padded)
    n_used = (ends[-1] // tm).astype(jnp.int32)
    fill_rank = jnp.arange(tm, dtype=jnp.int32)
    fill_keys = jnp.where(fill_rank[None, :] < (padded - counts)[:, None], experts[:, None], n_experts)
    keys = jnp.concatenate([e_flat, fill_keys.reshape(n_experts * tm)])
    src = jnp.argsort(keys, stable=True).astype(jnp.int32)
    p = jnp.arange(n_tiles * tm, dtype=jnp.int32)
    valid = src < n_slots
    tok = jnp.where(valid, src % t, 0)
    dst = jnp.where(valid, src, n_slots + ((p // tm + 1) % 2) * tm + p % tm)
    tile_ids = jnp.arange(n_tiles, dtype=jnp.int32)
    tile_e = jnp.sum((tile_ids[:, None] * tm >= ends[None, :]).astype(jnp.int32), axis=1)
    last_e = jnp.max(jnp.where(tile_ids < n_used, tile_e, 0))
    tile_e = jnp.where(tile_ids < n_used, tile_e, last_e)
    fill_dst = (n_slots + tm + jnp.arange(tm, dtype=jnp.int32))[None, :]
    dst_tbl = jnp.concatenate([fill_dst, dst.reshape(n_tiles, tm)], axis=0)
    tok_tbl = jnp.concatenate([tok.reshape(n_tiles, tm), jnp.zeros((1, tm), jnp.int32)], axis=0)
    return (tile_e, n_used.reshape(1), tok_tbl.reshape(n_tiles + 1, 1, tm), dst_tbl.reshape(n_tiles + 1, 1, tm))


def _moe_group_kernel(te_ref, nu_ref, tok0_ref, tokn_ref, dstp_ref, dstl_ref, h2_hbm, w1_ref, w3_ref, w2_ref,
                      y_hbm, xbuf, ybuf, gsem, ssem, *, tm, n_chunks):
    i = pl.program_id(0)
    n_used = nu_ref[0]
    slot = lax.rem(i, 2)

    def gather_row(tok_ref, s, r):
        pltpu.make_async_copy(h2_hbm.at[pl.ds(tok_ref[0, 0, r], 1)], xbuf.at[s, pl.ds(r, 1)], gsem.at[s]).start()

    def scatter_row(dst_ref, s, r):
        pltpu.make_async_copy(ybuf.at[s, pl.ds(r, 1)], y_hbm.at[pl.ds(dst_ref[0, 0, r], 1)], ssem.at[s]).start()

    def wait_gather(s):
        pltpu.make_async_copy(h2_hbm.at[pl.ds(0, tm)], xbuf.at[s], gsem.at[s]).wait()

    def wait_scatter(s):
        pltpu.make_async_copy(ybuf.at[s], y_hbm.at[pl.ds(0, tm)], ssem.at[s]).wait()

    @pl.when(i == 0)
    def _():
        ybuf[...] = jnp.zeros(ybuf.shape, F32)
        tail = y_hbm.shape[0] - 2 * tm
        pltpu.make_async_copy(ybuf.at[0], y_hbm.at[pl.ds(tail, tm)], ssem.at[0]).start()

        def body(r, carry):
            gather_row(tok0_ref, 0, r)
            return carry
        lax.fori_loop(0, tm, body, 0, unroll=8)

    @pl.when(i < n_used)
    def _():
        wait_gather(slot)
        wait_scatter(slot)
        x = xbuf[slot].astype(BF16)
        hid = w1_ref.shape[3]
        hc = hid // n_chunks
        rows = tm // n_chunks
        y = None
        for c in range(n_chunks):
            for r in range(c * rows, (c + 1) * rows):
                gather_row(tokn_ref, 1 - slot, r)
                scatter_row(dstp_ref, 1 - slot, r)
            a = _dot(x, w1_ref[0, 0, :, c * hc:(c + 1) * hc])
            g = _dot(x, w3_ref[0, 0, :, c * hc:(c + 1) * hc])
            act = (jax.nn.silu(a) * g).astype(BF16)
            yc = _dot(act, w2_ref[0, 0, c * hc:(c + 1) * hc, :])
            y = yc if y is None else y + yc
        ybuf[slot] = y

    @pl.when(i == pl.num_programs(0) - 1)
    def _():
        last = lax.rem(n_used + 1, 2)

        def body(r, carry):
            scatter_row(dstl_ref, last, r)
            return carry
        lax.fori_loop(0, tm, body, 0, unroll=8)
        wait_scatter(last)
        wait_scatter(1 - last)
        wait_gather(1 - last)


def _moe_group(h2, e_idx, w1, w3, w2, layer, tm):
    t, d = h2.shape
    _, n_e, _, hid = w1.shape
    tile_e, n_used, tok, dst = _route_tables(e_idx, tm, n_e)
    n_tiles = tile_e.shape[0]
    n_chunks = _n_hidden_chunks(hid)
    smem_blk = lambda f: pl.BlockSpec((1, 1, tm), f, memory_space=pltpu.SMEM)
    wspec = lambda shape: pl.BlockSpec((1, 1) + shape, lambda i, te, nu: (layer, te[i], 0, 0),
                                       pipeline_mode=pl.Buffered(1))
    grid_spec = pltpu.PrefetchScalarGridSpec(
        num_scalar_prefetch=2,
        grid=(n_tiles,),
        in_specs=[smem_blk(lambda i, te, nu: (0, 0, 0)),
                  smem_blk(lambda i, te, nu: (i + 1, 0, 0)),
                  smem_blk(lambda i, te, nu: (i, 0, 0)),
                  smem_blk(lambda i, te, nu: (nu[0], 0, 0)),
                  pl.BlockSpec(memory_space=pl.ANY),
                  wspec((d, hid)), wspec((d, hid)), wspec((hid, d))],
        out_specs=pl.BlockSpec(memory_space=pl.ANY),
        scratch_shapes=[pltpu.VMEM((2, tm, d), F32), pltpu.VMEM((2, tm, d), F32),
                        pltpu.SemaphoreType.DMA((2,)), pltpu.SemaphoreType.DMA((2,))],
    )
    return pl.pallas_call(
        functools.partial(_moe_group_kernel, tm=tm, n_chunks=n_chunks),
        grid_spec=grid_spec,
        out_shape=jax.ShapeDtypeStruct((TOP_K * t + 2 * tm, d), F32),
        compiler_params=_cparams(("arbitrary",)),
        name="moe_group",
    )(tile_e, n_used, tok, tok, dst, dst, h2, w1, w3, w2)


def _moe_combine_kernel(x1_ref, y0_ref, y1_ref, route_ref, g2_ref, out_ref):
    out_ref[0] = _moe_combined(x1_ref, y0_ref, y1_ref, route_ref, g2_ref)


def _moe_combine(x1, y, route, gate2, tm):
    b, n, d = x1.shape
    nt = n // tm
    return pl.pallas_call(
        _moe_combine_kernel,
        grid=(b, nt),
        in_specs=_moe_pending_specs(b, n, d, tm),
        out_specs=pl.BlockSpec((1, tm, d), lambda bi, i: (bi, i, 0)),
        out_shape=jax.ShapeDtypeStruct((b, n, d), F32),
        compiler_params=_cparams(("arbitrary", "arbitrary")),
        name="moe_combine",
    )(x1, y, y, route, gate2)


def _moe(h2, route, x1, gate2, w1, w3, w2, layer, tm, defer=False):
    b, n, d = x1.shape
    e_idx = route.reshape(b * n, LANES)[:, :TOP_K].astype(jnp.int32)
    y = _moe_group(h2.reshape(b * n, d), e_idx, w1, w3, w2, layer, tm)
    if defer:
        return (x1, y, route, gate2)
    return _moe_combine(x1, y, route, gate2, tm)


def _prep_mla_weights(w_dq, g_q, w_uq, w_dkv, g_kv, w_ukv, g_qn, g_kn):
    ql = w_uq.shape[0]
    half = D_ROPE // 2
    pad = LANES - D_ROPE

    def ext_cols(w_rope):
        z = jnp.zeros(w_rope.shape[:-1] + (pad,), w_rope.dtype)
        r1, r2 = w_rope[..., :half], w_rope[..., half:]
        return jnp.concatenate([w_rope, z], -1), jnp.concatenate([-r2, r1, z], -1)

    def ext_gain(g):
        z = jnp.zeros((pad,), g.dtype)
        gr1, gr2 = g[D_NOPE:D_NOPE + half], g[D_NOPE + half:]
        return jnp.concatenate([g[:D_NOPE], gr1, gr2, z, gr2, gr1, z])[None, :].astype(F32)

    wq = w_uq.reshape(ql, N_HEADS, D_QK)
    qa, qb = ext_cols(wq[:, :, D_NOPE:])
    w_uq_ext = jnp.concatenate([wq[:, :, :D_NOPE], qa, qb], -1).reshape(ql, N_HEADS * Q_EXT)
    kl = w_ukv.shape[0]
    ka, kb = ext_cols(w_dkv[:, kl:])
    w_dkv_ext = jnp.concatenate([w_dkv[:, :kl], ka, kb], -1)
    wkv = w_ukv.reshape(kl, N_HEADS, D_NOPE + D_V)
    w_uk = wkv[:, :, :D_NOPE].reshape(kl, N_HEADS * D_NOPE)
    w_uvt = wkv[:, :, D_NOPE:].reshape(kl, N_HEADS * D_V).T
    return dict(w_dq=w_dq.astype(BF16), g_q=g_q[None, :].astype(F32), w_uq=w_uq_ext.astype(BF16),
                g_qn=ext_gain(g_qn), w_dkv=w_dkv_ext.astype(BF16), g_kv=g_kv[None, :].astype(F32),
                w_uk=w_uk.astype(BF16), w_uvt=w_uvt.astype(BF16), g_kn=ext_gain(g_kn))


def _rope_tables(n_tok):
    rows = n_tok // GRID_W
    row = jnp.repeat(jnp.arange(rows, dtype=F32), GRID_W)
    col = jnp.tile(jnp.arange(GRID_W, dtype=F32), rows)
    n_freq = D_ROPE // 4
    inv = ROPE_BASE ** (-jnp.arange(n_freq, dtype=F32) / n_freq)
    ang = jnp.concatenate([row[:, None] * inv, col[:, None] * inv], axis=-1)
    z = jnp.zeros((n_tok, LANES - D_ROPE), F32)
    cos, sin = jnp.cos(ang), jnp.sin(ang)
    return jnp.concatenate([cos, cos, z], -1), jnp.concatenate([sin, sin, z], -1)


def _no_rope_tables(n_tok):
    one = jnp.ones((n_tok, D_ROPE), F32)
    z = jnp.zeros((n_tok, LANES - D_ROPE), F32)
    return jnp.concatenate([one, z], -1), jnp.zeros((n_tok, LANES), F32)


def _n_hidden_chunks(hid):
    for c in range(1, hid // MXU_COLS + 1):
        if hid % (c * MXU_COLS) == 0 and hid // c <= MAX_HIDDEN_CHUNK:
            return c
    return 1


def _tile(n, target):
    t = min(n, target)
    assert n % t == 0, (n, t)
    return t


def _mla_layer(x, ctx, mods_l, mods_c, norm1_g, norm2_g, w, w_o, ffn, need_ctx):
    b, n, d = x[0].shape if isinstance(x, tuple) else x.shape
    nc = ctx.shape[1]
    tm = _tile(n, 512)
    tmc = _tile(nc, 512)
    sh1, sc1, g1, sh2, sc2, g2 = mods_l
    csh1, csc1, cg1, csh2, csc2, cg2 = mods_c
    cos_l, sin_l = _rope_tables(n)
    cos_c, sin_c = _no_rope_tables(nc)
    ql, kl, vtl, *x_comb = _mla_pre(x, sh1, sc1, norm1_g, cos_l, sin_l, w, tm)
    if x_comb:
        x = x_comb[0]
    qc, kc, vtc = _mla_pre(ctx, csh1, csc1, norm1_g, cos_c, sin_c, w, tmc)
    k5 = lambda k, t: k.reshape(k.shape[0], k.shape[1], k.shape[2] // t, t, k.shape[3])
    src_l = (k5(kl, tm), vtl)
    src_c = (k5(kc, tmc), vtc)
    ol = _attention(ql, [src_l, src_c], _tile(n, 2048))
    x_new = _mla_ffn(x, ol, g1, sh2, sc2, g2, norm2_g, w_o, *ffn, tm)
    ctx_new = ctx
    if need_ctx:
        oc = _attention(qc, [src_c], _tile(nc, 1024))
        ctx_new = _mla_ffn(ctx, oc, cg1, csh2, csc2, cg2, norm2_g, w_o, *ffn, tmc)
    return x_new, ctx_new


def _rg_layer(x, ctx, mods_l, mods_c, norm1_g, norm2_g, rg, moe, need_ctx, defer):
    b, n, d = x.shape
    nc = ctx.shape[1]
    tm = _tile(n, 512)
    tmc = _tile(nc, 512)
    tn = _tile(n, 256)
    tnc = _tile(nc, 256)
    sh1, sc1, g1, sh2, sc2, g2 = mods_l
    csh1, csc1, cg1, csh2, csc2, cg2 = mods_c
    w_in, w_gate, conv_w, conv_b, w_a, b_a, w_x, b_x, lam, w_out = rg
    w_router_pad, mw1, mw3, mw2, layer, n_experts = moe
    ul, gll = _rg_in(x, sh1, sc1, norm1_g, w_in, w_gate, conv_w, conv_b, tm)
    uc, glc = _rg_in(ctx, csh1, csc1, norm1_g, w_in, w_gate, conv_w, conv_b, tmc)
    zero = jnp.zeros((b, w_in.shape[1] // LANES, LANES), F32)
    hl, hc = [], []
    for dr in range(2):
        args = (w_a[dr], b_a[dr], w_x[dr], b_x[dr], lam[dr])
        hcd, h_end = _rg_scan(uc, zero, *args, reverse=bool(dr), tn=tnc)
        hld, _ = _rg_scan(ul, h_end, *args, reverse=bool(dr), tn=tn)
        hl.append(hld)
        hc.append(hcd)
    x1, h2, route = _rg_out(x, hl[0], hl[1], gll, g1, sh2, sc2, norm2_g, w_out, w_router_pad, n_experts, tm)
    x_new = _moe(h2, route, x1, g2, mw1, mw3, mw2, layer, tm, defer=defer)
    ctx_new = ctx
    if need_ctx:
        c1, ch2, croute = _rg_out(ctx, hc[0], hc[1], glc, cg1, csh2, csc2, norm2_g, w_out, w_router_pad,
                                  n_experts, tmc)
        ctx_new = _moe(ch2, croute, c1, cg2, mw1, mw3, mw2, layer, tmc)
    return x_new, ctx_new


def kernel(x, c, ctx, c_ctx, ada_w, ada_b, norm1_g, norm2_g, mla_w_dq, mla_g_q, mla_w_uq, mla_w_dkv, mla_g_kv, mla_w_ukv, mla_g_qn, mla_g_kn, mla_w_o, rg_w_in, rg_w_gate, rg_conv_w, rg_conv_b, rg_w_a, rg_b_a, rg_w_x, rg_b_x, rg_lam, rg_w_out, ffn_w1, ffn_w3, ffn_w2, moe_w_router, moe_w1, moe_w3, moe_w2):
    b, n, d = x.shape
    depth = ada_w.shape[0]
    n_experts = moe_w_router.shape[2]

    rows = 2 * SUBLANES
    cvec = jnp.concatenate([c, c_ctx[None, :], jnp.zeros((rows - b - 1, d), F32)], axis=0)
    mods = _ada_all(cvec, ada_w, ada_b)
    moe_w = (moe_w1.astype(BF16), moe_w3.astype(BF16), moe_w2.astype(BF16))

    for i in range(depth):
        need_ctx = i < depth - 1
        j = i // 2
        chunks = [mods[i, :, k * d:(k + 1) * d] for k in range(6)]
        mods_l = [m[:b, None, :] for m in chunks]
        mods_c = [jnp.broadcast_to(m[b:b + 1, None, :], (b, 1, d)) for m in chunks]
        n1 = norm1_g[i][None, :]
        n2 = norm2_g[i][None, :]
        if i % 2 == 0:
            w = _prep_mla_weights(mla_w_dq[j], mla_g_q[j], mla_w_uq[j], mla_w_dkv[j], mla_g_kv[j],
                                  mla_w_ukv[j], mla_g_qn[j], mla_g_kn[j])
            ffn = (ffn_w1[j].astype(BF16), ffn_w3[j].astype(BF16), ffn_w2[j].astype(BF16))
            x, ctx = _mla_layer(x, ctx, mods_l, mods_c, n1, n2, w, mla_w_o[j].astype(BF16), ffn, need_ctx)
        else:
            rg = (rg_w_in[j].astype(BF16), rg_w_gate[j].astype(BF16), rg_conv_w[j], rg_conv_b[j][None, :],
                  rg_w_a[j].astype(BF16), rg_b_a[j][:, None, :], rg_w_x[j].astype(BF16), rg_b_x[j][:, None, :],
                  rg_lam[j][:, None, :], rg_w_out[j].astype(BF16))
            w_router_pad = jnp.concatenate(
                [moe_w_router[j], jnp.zeros((d, LANES - n_experts), F32)], axis=1)
            moe = (w_router_pad, *moe_w, j, n_experts)
            defer = i + 1 < depth
            x, ctx = _rg_layer(x, ctx, mods_l, mods_c, n1, n2, rg, moe, need_ctx, defer)
    return x
```

```python
import functools
import math

import jax
import jax.numpy as jnp
from jax import lax
from jax.experimental import pallas as pl
from jax.experimental.pallas import tpu as pltpu

F32 = jnp.float32
BF16 = jnp.bfloat16

EPS = 1e-6
GRID_W = 64
N_HEADS = 8
D_NOPE = 128
D_ROPE = 64
D_V = 128
D_QK = D_NOPE + D_ROPE
D_HEAD_PAD = 256
D_VX = D_V + 16
Q_EXT = 384
ROPE_BASE = 10000.0
RG_BLOCKS = 4
RG_C = 8.0
TOP_K = 2
LANES = 128
SUBLANES = 8
MXU_COLS = 256
MAX_HIDDEN_CHUNK = 3072
VMEM_LIMIT = 56 * 1024 * 1024
LOG2E = 1.4426950408889634


def _cparams(sem):
    return pltpu.CompilerParams(dimension_semantics=sem, vmem_limit_bytes=VMEM_LIMIT)


def _resident(shape):
    nd = len(shape)
    return pl.BlockSpec(shape, lambda *_: (0,) * nd, pipeline_mode=pl.Buffered(1))


def _dot(a, b):
    return jnp.dot(a, b, preferred_element_type=F32)


def _sigmoid(x):
    return 0.5 * jnp.tanh(0.5 * x) + 0.5


def _rms(xf, g):
    return xf * lax.rsqrt(jnp.mean(xf * xf, axis=-1, keepdims=True) + EPS) * g


def _prenorm(xf, g, shift, scale):
    return _rms(xf, g) * (1.0 + scale) + shift


def _ada_kernel(c_ref, w_ref, b_ref, o_ref):
    s = jax.nn.silu(c_ref[...])
    o_ref[0] = jnp.dot(s, w_ref[0], precision=lax.Precision.HIGHEST,
                       preferred_element_type=F32) + b_ref[0]


def _ada_all(cvec, ada_w, ada_b):
    depth, d, n6 = ada_w.shape
    rows = cvec.shape[0]
    tn = 1536
    return pl.pallas_call(
        _ada_kernel,
        grid=(depth, n6 // tn),
        in_specs=[pl.BlockSpec((rows, d), lambda l, j: (0, 0)),
                  pl.BlockSpec((1, d, tn), lambda l, j: (l, 0, j)),
                  pl.BlockSpec((1, 1, tn), lambda l, j: (l, 0, j))],
        out_specs=pl.BlockSpec((1, rows, tn), lambda l, j: (l, 0, j)),
        out_shape=jax.ShapeDtypeStruct((depth, rows, n6), F32),
        compiler_params=_cparams(("arbitrary", "arbitrary")),
        name="ada_mod",
    )(cvec, ada_w, ada_b.reshape(depth, 1, n6))


def _mla_pre_kernel(*refs, q_scale, n_x):
    (sh_ref, sc_ref, ng_ref, cos_ref, sin_ref, wdq_ref, gq_ref, wuq_ref, gqn_ref,
     wdkv_ref, gkv_ref, wuk_ref, wuvt_ref, gkn_ref, q_ref, k_ref, vt_ref) = refs[n_x:n_x + 17]
    if n_x == 1:
        x = refs[0][0]
    else:
        x1_ref, y0_ref, y1_ref, route_ref, g2_ref = refs[:n_x]
        x = _moe_combined(x1_ref, y0_ref, y1_ref, route_ref, g2_ref)
        refs[n_x + 17][0] = x
    h = _prenorm(x, ng_ref[...], sh_ref[0], sc_ref[0]).astype(BF16)
    cosv = cos_ref[...]
    sinv = sin_ref[...]

    qn = _rms(_dot(h, wdq_ref[...]), gq_ref[...]).astype(BF16)
    qall = _dot(qn, wuq_ref[...])
    g_n = gqn_ref[:, 0:LANES]
    g_a = gqn_ref[:, LANES:2 * LANES]
    g_b = gqn_ref[:, 2 * LANES:3 * LANES]
    for hh in range(N_HEADS):
        base = hh * Q_EXT
        nope = qall[:, base:base + LANES]
        ra = qall[:, base + LANES:base + 2 * LANES]
        rb = qall[:, base + 2 * LANES:base + 3 * LANES]
        ss = jnp.sum(nope * nope, axis=-1, keepdims=True) + jnp.sum(ra * ra, axis=-1, keepdims=True)
        inv = lax.rsqrt(ss * (1.0 / D_QK) + EPS) * q_scale
        q_ref[0, hh, :, 0:LANES] = (nope * g_n * inv).astype(BF16)
        q_ref[0, hh, :, LANES:2 * LANES] = ((ra * g_a * cosv + rb * g_b * sinv) * inv).astype(BF16)

    kva = _dot(h, wdkv_ref[...])
    ckv = _rms(kva[:, 0:LANES], gkv_ref[...]).astype(BF16)
    pa = kva[:, LANES:2 * LANES]
    pb = kva[:, 2 * LANES:3 * LANES]
    pe_ss = jnp.sum(pa * pa, axis=-1, keepdims=True)
    k_n = gkn_ref[:, 0:LANES]
    k_a = gkn_ref[:, LANES:2 * LANES]
    k_b = gkn_ref[:, 2 * LANES:3 * LANES]
    rope = pa * k_a * cosv + pb * k_b * sinv
    knope = _dot(ckv, wuk_ref[...])
    vt_all = lax.dot_general(wuvt_ref[...], ckv, (((1,), (1,)), ((), ())),
                             preferred_element_type=F32)
    for hh in range(N_HEADS):
        kn = knope[:, hh * LANES:(hh + 1) * LANES]
        ss = jnp.sum(kn * kn, axis=-1, keepdims=True) + pe_ss
        inv = lax.rsqrt(ss * (1.0 / D_QK) + EPS)
        k_ref[0, hh, :, 0:LANES] = (kn * k_n * inv).astype(BF16)
        k_ref[0, hh, :, LANES:2 * LANES] = (rope * inv).astype(BF16)
        vt_ref[0, hh, 0, 0:D_V, :] = vt_all[hh * D_V:(hh + 1) * D_V, :].astype(BF16)
        vt_ref[0, hh, 0, D_V:D_VX, :] = jnp.ones((D_VX - D_V, vt_all.shape[1]), BF16)


def _moe_combined(x1_ref, y0_ref, y1_ref, route_ref, g2_ref):
    r = route_ref[0]
    return x1_ref[0] + g2_ref[0] * (r[:, 2:3] * y0_ref[...] + r[:, 3:4] * y1_ref[...])


def _moe_pending_specs(b, n, d, tm):
    nt = n // tm
    tok = lambda bi, i: (bi, i, 0)
    return [pl.BlockSpec((1, tm, d), tok),
            pl.BlockSpec((tm, d), lambda bi, i: (bi * nt + i, 0)),
            pl.BlockSpec((tm, d), lambda bi, i: (b * nt + bi * nt + i, 0)),
            pl.BlockSpec((1, tm, LANES), tok),
            pl.BlockSpec((1, 1, d), lambda bi, i: (bi, 0, 0))]


def _mla_pre(x, shift, scale, norm_g, cos_t, sin_t, w, tm):
    pending = isinstance(x, tuple)
    b, n, d = x[0].shape if pending else x.shape
    nt = n // tm
    q_scale = (D_QK ** -0.5) * LOG2E
    tok = lambda bi, i: (bi, i, 0)
    vec = lambda bi, i: (bi, 0, 0)
    out_shapes = (jax.ShapeDtypeStruct((b, N_HEADS, n, D_HEAD_PAD), BF16),
                  jax.ShapeDtypeStruct((b, N_HEADS, n, D_HEAD_PAD), BF16),
                  jax.ShapeDtypeStruct((b, N_HEADS, nt, D_VX, tm), BF16))
    weights = (w["w_dq"], w["g_q"], w["w_uq"], w["g_qn"], w["w_dkv"], w["g_kv"], w["w_uk"], w["w_uvt"], w["g_kn"])
    out_specs = (pl.BlockSpec((1, N_HEADS, tm, D_HEAD_PAD), lambda bi, i: (bi, 0, i, 0)),
                 pl.BlockSpec((1, N_HEADS, tm, D_HEAD_PAD), lambda bi, i: (bi, 0, i, 0)),
                 pl.BlockSpec((1, N_HEADS, 1, D_VX, tm), lambda bi, i: (bi, 0, i, 0, 0)))
    if pending:
        x1, y, route, gate2 = x
        x_args = (x1, y, y, route, gate2)
        x_specs = _moe_pending_specs(b, n, d, tm)
        out_specs += (pl.BlockSpec((1, tm, d), tok),)
        out_shapes += (jax.ShapeDtypeStruct((b, n, d), F32),)
    else:
        x_args = (x,)
        x_specs = [pl.BlockSpec((1, tm, d), tok)]
    return pl.pallas_call(
        functools.partial(_mla_pre_kernel, q_scale=q_scale, n_x=len(x_args)),
        grid=(b, nt),
        in_specs=x_specs + [
                  pl.BlockSpec((1, 1, d), vec), pl.BlockSpec((1, 1, d), vec),
                  _resident(norm_g.shape),
                  pl.BlockSpec((tm, LANES), lambda bi, i: (i, 0)),
                  pl.BlockSpec((tm, LANES), lambda bi, i: (i, 0))]
                 + [_resident(a.shape) for a in weights],
        out_specs=out_specs,
        out_shape=out_shapes,
        compiler_params=_cparams(("arbitrary", "arbitrary")),
        name="mla_pre",
    )(*x_args, shift, scale, norm_g, cos_t, sin_t, *weights)


def _attn_kernel(q_ref, *refs, n_src):
    srcs = [(refs[2 * s], refs[2 * s + 1]) for s in range(n_src)]
    o_ref = refs[2 * n_src]
    acc_ref = refs[2 * n_src + 1]
    s_bufs = refs[2 * n_src + 2:]

    q_t = q_ref[0, 0].astype(F32).T.astype(BF16)
    acc_ref[...] = jnp.zeros(acc_ref.shape, F32)

    def produce(kc, s_ref, m_prev):
        s = _dot(kc, q_t)
        s_ref[...] = s
        return jnp.maximum(m_prev, jnp.max(s, axis=0, keepdims=True))

    def consume(s_ref, vtc, m_cur, m_prev):
        alpha = jnp.exp2(m_prev - m_cur)
        p = jnp.exp2(s_ref[...] - m_cur).astype(BF16)
        acc_ref[...] = alpha * acc_ref[...] + _dot(vtc, p)

    m_init = jnp.full((1, acc_ref.shape[1]), -1e30, F32)
    k_ref, vt_ref = srcs[0]
    n = k_ref.shape[2]
    if n == 1:
        m_cur = produce(k_ref[0, 0, 0], s_bufs[0], m_init)
        m_prev = m_init
        pending = (s_bufs[0], vt_ref, 0)
    else:
        s_a, s_b = s_bufs[0], s_bufs[1]
        m0 = produce(k_ref[0, 0, 0], s_a, m_init)

        def body(jj, carry):
            m_prev, m_cur = carry
            j = 2 * jj
            m_1 = produce(k_ref[0, 0, j + 1], s_b, m_cur)
            consume(s_a, vt_ref[0, 0, j], m_cur, m_prev)
            m_2 = produce(k_ref[0, 0, j + 2], s_a, m_1)
            consume(s_b, vt_ref[0, 0, j + 1], m_1, m_cur)
            return m_1, m_2

        m_prev, m_cur = lax.fori_loop(0, n // 2 - 1, body, (m_init, m0))
        m_1 = produce(k_ref[0, 0, n - 1], s_b, m_cur)
        consume(s_a, vt_ref[0, 0, n - 2], m_cur, m_prev)
        m_prev, m_cur = m_cur, m_1
        pending = (s_b, vt_ref, n - 1)

    for kx_ref, vtx_ref in srcs[1:]:
        m_x = produce(kx_ref[0, 0, 0], s_bufs[-1], m_cur)
        consume(pending[0], pending[1][0, 0, pending[2]], m_cur, m_prev)
        m_prev, m_cur = m_cur, m_x
        pending = (s_bufs[-1], vtx_ref, 0)
    consume(pending[0], pending[1][0, 0, pending[2]], m_cur, m_prev)

    o = acc_ref[0:D_V, :] * (1.0 / acc_ref[D_V:D_V + 1, :])
    o_ref[0] = o.T.astype(BF16)


def _attention(q, srcs, tq):
    b, h, nq, dp = q.shape
    in_specs = [pl.BlockSpec((1, 1, tq, dp), lambda bi, hi, i: (bi, hi, i, 0))]
    args = [q]
    for k5, vt5 in srcs:
        in_specs.append(pl.BlockSpec((1, 1) + k5.shape[2:], lambda bi, hi, i: (bi, hi, 0, 0, 0)))
        in_specs.append(pl.BlockSpec((1, 1) + vt5.shape[2:], lambda bi, hi, i: (bi, hi, 0, 0, 0)))
        args += [k5, vt5]
    n0, tk0 = srcs[0][0].shape[2:4]
    assert n0 == 1 or n0 % 2 == 0, n0
    assert all(k5.shape[2] == 1 for k5, _ in srcs[1:])
    s_shapes = [pltpu.VMEM((tk0, tq), F32)] * (1 if n0 == 1 else 2)
    s_shapes += [pltpu.VMEM((k5.shape[3], tq), F32) for k5, _ in srcs[1:2]]
    return pl.pallas_call(
        functools.partial(_attn_kernel, n_src=len(srcs)),
        grid=(b, h, nq // tq),
        in_specs=in_specs,
        out_specs=pl.BlockSpec((1, tq, D_V), lambda bi, hi, i: (bi, i, hi)),
        out_shape=jax.ShapeDtypeStruct((b, nq, h * D_V), BF16),
        scratch_shapes=[pltpu.VMEM((D_VX, tq), F32)] + s_shapes,
        compiler_params=_cparams(("arbitrary", "arbitrary", "arbitrary")),
        name="mla_attn",
    )(*args)


def _mla_ffn_kernel(x_ref, o_ref, g1_ref, sh_ref, sc_ref, g2_ref, ng_ref,
                    wo_ref, w1_ref, w3_ref, w2_ref, out_ref, *, n_chunks):
    x1 = x_ref[0] + g1_ref[0] * _dot(o_ref[0], wo_ref[...])
    h2 = _prenorm(x1, ng_ref[...], sh_ref[0], sc_ref[0]).astype(BF16)
    hid = w1_ref.shape[1]
    hc = hid // n_chunks
    y = None
    for c in range(n_chunks):
        a = _dot(h2, w1_ref[:, c * hc:(c + 1) * hc])
        g = _dot(h2, w3_ref[:, c * hc:(c + 1) * hc])
        act = (jax.nn.silu(a) * g).astype(BF16)
        yc = _dot(act, w2_ref[c * hc:(c + 1) * hc, :])
        y = yc if y is None else y + yc
    out_ref[0] = x1 + g2_ref[0] * y


def _mla_ffn(x, o, gate1, shift2, scale2, gate2, norm_g, w_o, w1, w3, w2, tm):
    b, n, d = x.shape
    hid = w1.shape[1]
    n_chunks = _n_hidden_chunks(hid)
    tok = lambda bi, i: (bi, i, 0)
    vec = lambda bi, i: (bi, 0, 0)
    return pl.pallas_call(
        functools.partial(_mla_ffn_kernel, n_chunks=n_chunks),
        grid=(b, n // tm),
        in_specs=[pl.BlockSpec((1, tm, d), tok), pl.BlockSpec((1, tm, o.shape[2]), tok),
                  pl.BlockSpec((1, 1, d), vec), pl.BlockSpec((1, 1, d), vec),
                  pl.BlockSpec((1, 1, d), vec), pl.BlockSpec((1, 1, d), vec),
                  _resident(norm_g.shape), _resident(w_o.shape),
                  _resident(w1.shape), _resident(w3.shape), _resident(w2.shape)],
        out_specs=pl.BlockSpec((1, tm, d), tok),
        out_shape=jax.ShapeDtypeStruct((b, n, d), F32),
        compiler_params=_cparams(("arbitrary", "arbitrary")),
        name="mla_ffn",
    )(x, o, gate1, shift2, scale2, gate2, norm_g, w_o, w1, w3, w2)


def _rg_in_kernel(xp_ref, xm_ref, xn_ref, sh_ref, sc_ref, ng_ref, win_ref, wg_ref, cw_ref, cb_ref,
                  u_ref, gl_ref, *, tm):
    i = pl.program_id(1)
    nt = pl.num_programs(1)
    x_ext = jnp.concatenate([xp_ref[0], xm_ref[0], xn_ref[0]], axis=0)
    h = _prenorm(x_ext, ng_ref[...], sh_ref[0], sc_ref[0]).astype(BF16)
    u_ext = _dot(h, win_ref[...])
    row = lax.broadcasted_iota(jnp.int32, (tm + 2 * SUBLANES, 1), 0)
    valid = jnp.logical_and(jnp.logical_or(row >= SUBLANES, i > 0),
                            jnp.logical_or(row < tm + SUBLANES, i < nt - 1))
    u_ext = jnp.where(valid, u_ext, 0.0)
    acc = cb_ref[...] + cw_ref[0:1, :] * u_ext[6:6 + tm]
    for k in range(1, 4):
        acc = acc + cw_ref[k:k + 1, :] * u_ext[6 + k:6 + k + tm]
    u_ref[0] = acc.astype(BF16)
    gl_ref[0] = jax.nn.gelu(_dot(h[SUBLANES:SUBLANES + tm], wg_ref[...])).astype(BF16)


def _rg_in(x, shift, scale, norm_g, w_in, w_gate, conv_w, conv_b, tm):
    b, n, d = x.shape
    c = w_in.shape[1]
    r = tm // SUBLANES
    last = n // SUBLANES - 1
    vec = lambda bi, i: (bi, 0, 0)
    tok = lambda bi, i: (bi, i, 0)
    return pl.pallas_call(
        functools.partial(_rg_in_kernel, tm=tm),
        grid=(b, n // tm),
        in_specs=[pl.BlockSpec((1, SUBLANES, d), lambda bi, i: (bi, jnp.maximum(i * r - 1, 0), 0)),
                  pl.BlockSpec((1, tm, d), tok),
                  pl.BlockSpec((1, SUBLANES, d), lambda bi, i: (bi, jnp.minimum((i + 1) * r, last), 0)),
                  pl.BlockSpec((1, 1, d), vec), pl.BlockSpec((1, 1, d), vec),
                  _resident(norm_g.shape), _resident(w_in.shape), _resident(w_gate.shape),
                  _resident(conv_w.shape), _resident(conv_b.shape)],
        out_specs=(pl.BlockSpec((1, tm, c), tok), pl.BlockSpec((1, tm, c), tok)),
        out_shape=(jax.ShapeDtypeStruct((b, n, c), BF16), jax.ShapeDtypeStruct((b, n, c), BF16)),
        compiler_params=_cparams(("arbitrary", "arbitrary")),
        name="rg_in",
    )(x, x, x, shift, scale, norm_g, w_in, w_gate, conv_w, conv_b)


def _rg_scan_kernel(u_ref, h0_ref, wa_ref, ba_ref, wx_ref, bx_ref, lam_ref,
                    hout_ref, hfin_ref, a_s, b_s, o_s, h_s, *, reverse, tn, pitch, nb):
    @pl.when(pl.program_id(1) == 0)
    def _():
        h_s[...] = h0_ref[...]

    width = u_ref.shape[2]
    bw = width // RG_BLOCKS
    n_ct = width // LANES
    u = u_ref[...].reshape(nb * tn, width)
    for blk in range(RG_BLOCKS):
        cs = slice(blk * bw, (blk + 1) * bw)
        ub = u[:, cs]
        r = _sigmoid(_dot(ub, wa_ref[blk]) + ba_ref[:, cs])
        ig = _sigmoid(_dot(ub, wx_ref[blk]) + bx_ref[:, cs])
        a = jnp.exp2((-RG_C * LOG2E) * jax.nn.softplus(-lam_ref[:, cs]) * r)
        z = 1.0 - a * a
        bb = (z * lax.rsqrt(jnp.maximum(z, 1e-30))) * (ig * ub.astype(F32))
        for bi in range(nb):
            for half in range(bw // LANES):
                j = blk * (bw // LANES) + half
                rows = slice(bi * tn, (bi + 1) * tn)
                a_s[bi, j * pitch:j * pitch + tn, :] = a[rows, half * LANES:(half + 1) * LANES]
                b_s[bi, j * pitch:j * pitch + tn, :] = bb[rows, half * LANES:(half + 1) * LANES]

    def body(g, hs):
        for s in range(SUBLANES):
            t = g * SUBLANES + s
            if reverse:
                t = tn - 1 - t
            idx = pl.ds(t, n_ct, stride=pitch)
            hs = tuple(a_s[bi, idx, :] * hs[bi] + b_s[bi, idx, :] for bi in range(nb))
            for bi in range(nb):
                o_s[bi, idx, :] = hs[bi]
        return hs

    hs = lax.fori_loop(0, tn // SUBLANES, body, tuple(h_s[bi] for bi in range(nb)))
    for bi in range(nb):
        h_s[bi] = hs[bi]
        hfin_ref[bi] = hs[bi]
        for j in range(n_ct):
            hout_ref[bi, :, j * LANES:(j + 1) * LANES] = o_s[bi, j * pitch:j * pitch + tn, :].astype(BF16)


def _rg_scan(u, h0, w_a, b_a, w_x, b_x, lam, reverse, tn):
    b, n, c = u.shape
    nt = n // tn
    n_ct = c // LANES
    nb = 4 if b % 4 == 0 else (2 if b % 2 == 0 else 1)
    pitch = tn + SUBLANES
    tmap = (lambda bi, i: (bi, nt - 1 - i, 0)) if reverse else (lambda bi, i: (bi, i, 0))
    st = lambda bi, i: (bi, 0, 0)
    scr = pltpu.VMEM((nb, n_ct * pitch, LANES), F32)
    return pl.pallas_call(
        functools.partial(_rg_scan_kernel, reverse=reverse, tn=tn, pitch=pitch, nb=nb),
        grid=(b // nb, nt),
        in_specs=[pl.BlockSpec((nb, tn, c), tmap), pl.BlockSpec((nb, n_ct, LANES), st),
                  _resident(w_a.shape), _resident(b_a.shape), _resident(w_x.shape),
                  _resident(b_x.shape), _resident(lam.shape)],
        out_specs=(pl.BlockSpec((nb, tn, c), tmap), pl.BlockSpec((nb, n_ct, LANES), st)),
        out_shape=(jax.ShapeDtypeStruct((b, n, c), BF16), jax.ShapeDtypeStruct((b, n_ct, LANES), F32)),
        scratch_shapes=[scr, scr, scr, pltpu.VMEM((nb, n_ct, LANES), F32)],
        compiler_params=_cparams(("arbitrary", "arbitrary")),
        name="rg_scan_bwd" if reverse else "rg_scan_fwd",
    )(u, h0, w_a, b_a, w_x, b_x, lam)


def _rg_out_kernel(x_ref, hf_ref, hb_ref, gl_ref, g1_ref, sh_ref, sc_ref, ng_ref, wout_ref, wr_ref,
                   x1_ref, h2_ref, route_ref, *, n_experts):
    y = ((hf_ref[0].astype(F32) + hb_ref[0].astype(F32)) * gl_ref[0].astype(F32)).astype(BF16)
    x1 = x_ref[0] + g1_ref[0] * _dot(y, wout_ref[...])
    x1_ref[0] = x1
    h2 = _prenorm(x1, ng_ref[...], sh_ref[0], sc_ref[0])
    h2_ref[0] = h2
    wr = wr_ref[...]
    w_hi = wr.astype(BF16)
    w_lo = (wr - w_hi.astype(F32)).astype(BF16)
    w_hl = jnp.concatenate([w_hi, w_lo], axis=1)
    h_hi = h2.astype(BF16)
    h_lo = (h2 - h_hi.astype(F32)).astype(BF16)
    parts = _dot(h_hi, w_hl) + _dot(h_lo, w_hl)
    logits = parts[:, 0:LANES] + parts[:, LANES:2 * LANES]
    lane = lax.broadcasted_iota(jnp.int32, logits.shape, 1).astype(F32)
    neg = jnp.float32(-jnp.inf)
    lg = jnp.where(lane < n_experts, logits, neg)
    m1 = jnp.max(lg, axis=-1, keepdims=True)
    i1 = jnp.min(jnp.where(lg == m1, lane, float(LANES)), axis=-1, keepdims=True)
    lg2 = jnp.where(lane == i1, neg, lg)
    m2 = jnp.max(lg2, axis=-1, keepdims=True)
    i2 = jnp.min(jnp.where(lg2 == m2, lane, float(LANES)), axis=-1, keepdims=True)
    e2 = jnp.exp(m2 - m1)
    den = 1.0 + e2
    route_ref[0] = (jnp.where(lane == 0.0, i1, 0.0) + jnp.where(lane == 1.0, i2, 0.0)
                    + jnp.where(lane == 2.0, 1.0 / den, 0.0) + jnp.where(lane == 3.0, e2 / den, 0.0))


def _rg_out(x, hf, hb, gl, gate1, shift2, scale2, norm_g, w_out, w_router_pad, n_experts, tm):
    b, n, d = x.shape
    c = hf.shape[2]
    tok = lambda bi, i: (bi, i, 0)
    vec = lambda bi, i: (bi, 0, 0)
    return pl.pallas_call(
        functools.partial(_rg_out_kernel, n_experts=n_experts),
        grid=(b, n // tm),
        in_specs=[pl.BlockSpec((1, tm, d), tok), pl.BlockSpec((1, tm, c), tok), pl.BlockSpec((1, tm, c), tok),
                  pl.BlockSpec((1, tm, c), tok),
                  pl.BlockSpec((1, 1, d), vec), pl.BlockSpec((1, 1, d), vec), pl.BlockSpec((1, 1, d), vec),
                  _resident(norm_g.shape), _resident(w_out.shape), _resident(w_router_pad.shape)],
        out_specs=(pl.BlockSpec((1, tm, d), tok), pl.BlockSpec((1, tm, d), tok),
                   pl.BlockSpec((1, tm, LANES), tok)),
        out_shape=(jax.ShapeDtypeStruct((b, n, d), F32), jax.ShapeDtypeStruct((b, n, d), F32),
                   jax.ShapeDtypeStruct((b, n, LANES), F32)),
        compiler_params=_cparams(("arbitrary", "arbitrary")),
        name="rg_out_router",
    )(x, hf, hb, gl, gate1, shift2, scale2, norm_g, w_out, w_router_pad)


def _route_tables(e_idx, tm, n_experts):
    t = e_idx.shape[0]
    n_slots = TOP_K * t
    n_tiles = n_slots // tm + n_experts
    e_flat = e_idx.T.reshape(n_slots)
    experts = jnp.arange(n_experts, dtype=jnp.int32)
    counts = jnp.sum((e_flat[:, None] == experts[None, :]).astype(jnp.int32), axis=0)
    padded = (counts + tm - 1) // tm * tm
    ends = jnp.cumsum(padded)
    n_used = (ends[-1] // tm).astype(jnp.int32)
    fill_rank = jnp.arange(tm, dtype=jnp.int32)
    fill_keys = jnp.where(fill_rank[None, :] < (padded - counts)[:, None], experts[:, None], n_experts)
    keys = jnp.concatenate([e_flat, fill_keys.reshape(n_experts * tm)])
    p = jnp.arange(n_tiles * tm, dtype=jnp.int32)
    bits = (n_tiles * tm - 1).bit_length()
    assert (n_experts + 1) << bits < 2 ** 31
    src = jnp.sort(keys * (1 << bits) + p) & ((1 << bits) - 1)
    valid = src < n_slots
    tok = jnp.where(valid, src % t, 0)
    dst = jnp.where(valid, src, n_slots + ((p // tm + 1) % 2) * tm + p % tm)
    tile_ids = jnp.arange(n_tiles, dtype=jnp.int32)
    tile_e = jnp.sum((tile_ids[:, None] * tm >= ends[None, :]).astype(jnp.int32), axis=1)
    last_e = jnp.max(jnp.where(tile_ids < n_used, tile_e, 0))
    tile_e = jnp.where(tile_ids < n_used, tile_e, last_e)
    fill_dst = (n_slots + tm + jnp.arange(tm, dtype=jnp.int32))[None, :]
    dst_tbl = jnp.concatenate([fill_dst, dst.reshape(n_tiles, tm)], axis=0)
    tok_tbl = jnp.concatenate([tok.reshape(n_tiles, tm), jnp.zeros((1, tm), jnp.int32)], axis=0)
    return (tile_e, n_used.reshape(1), tok_tbl.reshape(n_tiles + 1, 1, tm), dst_tbl.reshape(n_tiles + 1, 1, tm))


def _moe_group_kernel(te_ref, nu_ref, tok0_ref, tokn_ref, dstp_ref, dstl_ref, h2_hbm, w1_ref, w3_ref, w2_ref,
                      y_hbm, xbuf, ybuf, gsem, ssem, *, tm, n_chunks):
    i = pl.program_id(0)
    n_used = nu_ref[0]
    slot = lax.rem(i, 2)

    def gather_row(tok_ref, s, r):
        pltpu.make_async_copy(h2_hbm.at[pl.ds(tok_ref[0, 0, r], 1)], xbuf.at[s, pl.ds(r, 1)], gsem.at[s]).start()

    def scatter_row(dst_ref, s, r):
        pltpu.make_async_copy(ybuf.at[s, pl.ds(r, 1)], y_hbm.at[pl.ds(dst_ref[0, 0, r], 1)], ssem.at[s]).start()

    def wait_gather(s):
        pltpu.make_async_copy(h2_hbm.at[pl.ds(0, tm)], xbuf.at[s], gsem.at[s]).wait()

    def wait_scatter(s):
        pltpu.make_async_copy(ybuf.at[s], y_hbm.at[pl.ds(0, tm)], ssem.at[s]).wait()

    @pl.when(i == 0)
    def _():
        ybuf[...] = jnp.zeros(ybuf.shape, F32)
        tail = y_hbm.shape[0] - 2 * tm
        pltpu.make_async_copy(ybuf.at[0], y_hbm.at[pl.ds(tail, tm)], ssem.at[0]).start()

        def body(r, carry):
            gather_row(tok0_ref, 0, r)
            return carry
        lax.fori_loop(0, tm, body, 0, unroll=8)

    @pl.when(i < n_used)
    def _():
        wait_gather(slot)
        wait_scatter(slot)
        x = xbuf[slot].astype(BF16)
        hid = w1_ref.shape[3]
        hc = hid // n_chunks
        rows = tm // n_chunks
        y = None
        for c in range(n_chunks):
            for r in range(c * rows, (c + 1) * rows):
                gather_row(tokn_ref, 1 - slot, r)
                scatter_row(dstp_ref, 1 - slot, r)
            a = _dot(x, w1_ref[0, 0, :, c * hc:(c + 1) * hc])
            g = _dot(x, w3_ref[0, 0, :, c * hc:(c + 1) * hc])
            act = (jax.nn.silu(a) * g).astype(BF16)
            yc = _dot(act, w2_ref[0, 0, c * hc:(c + 1) * hc, :])
            y = yc if y is None else y + yc
        ybuf[slot] = y

    @pl.when(i == pl.num_programs(0) - 1)
    def _():
        last = lax.rem(n_used + 1, 2)

        def body(r, carry):
            scatter_row(dstl_ref, last, r)
            return carry
        lax.fori_loop(0, tm, body, 0, unroll=8)
        wait_scatter(last)
        wait_scatter(1 - last)
        wait_gather(1 - last)


def _moe_group(h2, e_idx, w1, w3, w2, layer, tm):
    t, d = h2.shape
    _, n_e, _, hid = w1.shape
    tile_e, n_used, tok, dst = _route_tables(e_idx, tm, n_e)
    n_tiles = tile_e.shape[0]
    n_chunks = _n_hidden_chunks(hid)
    smem_blk = lambda f: pl.BlockSpec((1, 1, tm), f, memory_space=pltpu.SMEM)
    wspec = lambda shape: pl.BlockSpec((1, 1) + shape, lambda i, te, nu: (layer, te[i], 0, 0),
                                       pipeline_mode=pl.Buffered(1))
    grid_spec = pltpu.PrefetchScalarGridSpec(
        num_scalar_prefetch=2,
        grid=(n_tiles,),
        in_specs=[smem_blk(lambda i, te, nu: (0, 0, 0)),
                  smem_blk(lambda i, te, nu: (i + 1, 0, 0)),
                  smem_blk(lambda i, te, nu: (i, 0, 0)),
                  smem_blk(lambda i, te, nu: (nu[0], 0, 0)),
                  pl.BlockSpec(memory_space=pl.ANY),
                  wspec((d, hid)), wspec((d, hid)), wspec((hid, d))],
        out_specs=pl.BlockSpec(memory_space=pl.ANY),
        scratch_shapes=[pltpu.VMEM((2, tm, d), F32), pltpu.VMEM((2, tm, d), F32),
                        pltpu.SemaphoreType.DMA((2,)), pltpu.SemaphoreType.DMA((2,))],
    )
    return pl.pallas_call(
        functools.partial(_moe_group_kernel, tm=tm, n_chunks=n_chunks),
        grid_spec=grid_spec,
        out_shape=jax.ShapeDtypeStruct((TOP_K * t + 2 * tm, d), F32),
        compiler_params=_cparams(("arbitrary",)),
        name="moe_group",
    )(tile_e, n_used, tok, tok, dst, dst, h2, w1, w3, w2)


def _moe_combine_kernel(x1_ref, y0_ref, y1_ref, route_ref, g2_ref, out_ref):
    out_ref[0] = _moe_combined(x1_ref, y0_ref, y1_ref, route_ref, g2_ref)


def _moe_combine(x1, y, route, gate2, tm):
    b, n, d = x1.shape
    nt = n // tm
    return pl.pallas_call(
        _moe_combine_kernel,
        grid=(b, nt),
        in_specs=_moe_pending_specs(b, n, d, tm),
        out_specs=pl.BlockSpec((1, tm, d), lambda bi, i: (bi, i, 0)),
        out_shape=jax.ShapeDtypeStruct((b, n, d), F32),
        compiler_params=_cparams(("arbitrary", "arbitrary")),
        name="moe_combine",
    )(x1, y, y, route, gate2)


def _moe(h2, route, x1, gate2, w1, w3, w2, layer, tm, defer=False):
    b, n, d = x1.shape
    e_idx = route.reshape(b * n, LANES)[:, :TOP_K].astype(jnp.int32)
    y = _moe_group(h2.reshape(b * n, d), e_idx, w1, w3, w2, layer, tm)
    if defer:
        return (x1, y, route, gate2)
    return _moe_combine(x1, y, route, gate2, tm)


def _prep_mla_weights(w_dq, g_q, w_uq, w_dkv, g_kv, w_ukv, g_qn, g_kn):
    ql = w_uq.shape[0]
    half = D_ROPE // 2
    pad = LANES - D_ROPE

    def ext_cols(w_rope):
        z = jnp.zeros(w_rope.shape[:-1] + (pad,), w_rope.dtype)
        r1, r2 = w_rope[..., :half], w_rope[..., half:]
        return jnp.concatenate([w_rope, z], -1), jnp.concatenate([-r2, r1, z], -1)

    def ext_gain(g):
        z = jnp.zeros((pad,), g.dtype)
        gr1, gr2 = g[D_NOPE:D_NOPE + half], g[D_NOPE + half:]
        return jnp.concatenate([g[:D_NOPE], gr1, gr2, z, gr2, gr1, z])[None, :].astype(F32)

    wq = w_uq.reshape(ql, N_HEADS, D_QK)
    qa, qb = ext_cols(wq[:, :, D_NOPE:])
    w_uq_ext = jnp.concatenate([wq[:, :, :D_NOPE], qa, qb], -1).reshape(ql, N_HEADS * Q_EXT)
    kl = w_ukv.shape[0]
    ka, kb = ext_cols(w_dkv[:, kl:])
    w_dkv_ext = jnp.concatenate([w_dkv[:, :kl], ka, kb], -1)
    wkv = w_ukv.reshape(kl, N_HEADS, D_NOPE + D_V)
    w_uk = wkv[:, :, :D_NOPE].reshape(kl, N_HEADS * D_NOPE)
    w_uvt = wkv[:, :, D_NOPE:].reshape(kl, N_HEADS * D_V).T
    return dict(w_dq=w_dq.astype(BF16), g_q=g_q[None, :].astype(F32), w_uq=w_uq_ext.astype(BF16),
                g_qn=ext_gain(g_qn), w_dkv=w_dkv_ext.astype(BF16), g_kv=g_kv[None, :].astype(F32),
                w_uk=w_uk.astype(BF16), w_uvt=w_uvt.astype(BF16), g_kn=ext_gain(g_kn))


def _rope_tables(n_tok):
    rows = n_tok // GRID_W
    row = jnp.repeat(jnp.arange(rows, dtype=F32), GRID_W)
    col = jnp.tile(jnp.arange(GRID_W, dtype=F32), rows)
    n_freq = D_ROPE // 4
    inv = ROPE_BASE ** (-jnp.arange(n_freq, dtype=F32) / n_freq)
    ang = jnp.concatenate([row[:, None] * inv, col[:, None] * inv], axis=-1)
    z = jnp.zeros((n_tok, LANES - D_ROPE), F32)
    cos, sin = jnp.cos(ang), jnp.sin(ang)
    return jnp.concatenate([cos, cos, z], -1), jnp.concatenate([sin, sin, z], -1)


def _no_rope_tables(n_tok):
    one = jnp.ones((n_tok, D_ROPE), F32)
    z = jnp.zeros((n_tok, LANES - D_ROPE), F32)
    return jnp.concatenate([one, z], -1), jnp.zeros((n_tok, LANES), F32)


def _n_hidden_chunks(hid):
    for c in range(1, hid // MXU_COLS + 1):
        if hid % (c * MXU_COLS) == 0 and hid // c <= MAX_HIDDEN_CHUNK:
            return c
    return 1


def _tile(n, target):
    t = min(n, target)
    assert n % t == 0, (n, t)
    return t


def _mla_layer(x, ctx, mods_l, mods_c, norm1_g, norm2_g, w, w_o, ffn, need_ctx):
    b, n, d = x[0].shape if isinstance(x, tuple) else x.shape
    nc = ctx.shape[1]
    tm = _tile(n, 512)
    tmc = _tile(nc, 512)
    sh1, sc1, g1, sh2, sc2, g2 = mods_l
    csh1, csc1, cg1, csh2, csc2, cg2 = mods_c
    cos_l, sin_l = _rope_tables(n)
    cos_c, sin_c = _no_rope_tables(nc)
    ql, kl, vtl, *x_comb = _mla_pre(x, sh1, sc1, norm1_g, cos_l, sin_l, w, tm)
    if x_comb:
        x = x_comb[0]
    qc, kc, vtc = _mla_pre(ctx, csh1, csc1, norm1_g, cos_c, sin_c, w, tmc)
    k5 = lambda k, t: k.reshape(k.shape[0], k.shape[1], k.shape[2] // t, t, k.shape[3])
    src_l = (k5(kl, tm), vtl)
    src_c = (k5(kc, tmc), vtc)
    ol = _attention(ql, [src_l, src_c], _tile(n, 2048))
    x_new = _mla_ffn(x, ol, g1, sh2, sc2, g2, norm2_g, w_o, *ffn, tm)
    ctx_new = ctx
    if need_ctx:
        oc = _attention(qc, [src_c], _tile(nc, 1024))
        ctx_new = _mla_ffn(ctx, oc, cg1, csh2, csc2, cg2, norm2_g, w_o, *ffn, tmc)
    return x_new, ctx_new


def _rg_layer(x, ctx, mods_l, mods_c, norm1_g, norm2_g, rg, moe, need_ctx, defer):
    b, n, d = x.shape
    nc = ctx.shape[1]
    tm = _tile(n, 512)
    tmc = _tile(nc, 512)
    tn = _tile(n, 256)
    tnc = _tile(nc, 256)
    sh1, sc1, g1, sh2, sc2, g2 = mods_l
    csh1, csc1, cg1, csh2, csc2, cg2 = mods_c
    w_in, w_gate, conv_w, conv_b, w_a, b_a, w_x, b_x, lam, w_out = rg
    w_router_pad, mw1, mw3, mw2, layer, n_experts = moe
    ul, gll = _rg_in(x, sh1, sc1, norm1_g, w_in, w_gate, conv_w, conv_b, tm)
    uc, glc = _rg_in(ctx, csh1, csc1, norm1_g, w_in, w_gate, conv_w, conv_b, tmc)
    zero = jnp.zeros((b, w_in.shape[1] // LANES, LANES), F32)
    hl, hc = [], []
    for dr in range(2):
        args = (w_a[dr], b_a[dr], w_x[dr], b_x[dr], lam[dr])
        hcd, h_end = _rg_scan(uc, zero, *args, reverse=bool(dr), tn=tnc)
        hld, _ = _rg_scan(ul, h_end, *args, reverse=bool(dr), tn=tn)
        hl.append(hld)
        hc.append(hcd)
    x1, h2, route = _rg_out(x, hl[0], hl[1], gll, g1, sh2, sc2, norm2_g, w_out, w_router_pad, n_experts, tm)
    x_new = _moe(h2, route, x1, g2, mw1, mw3, mw2, layer, tm, defer=defer)
    ctx_new = ctx
    if need_ctx:
        c1, ch2, croute = _rg_out(ctx, hc[0], hc[1], glc, cg1, csh2, csc2, norm2_g, w_out, w_router_pad,
                                  n_experts, tmc)
        ctx_new = _moe(ch2, croute, c1, cg2, mw1, mw3, mw2, layer, tmc)
    return x_new, ctx_new


def kernel(x, c, ctx, c_ctx, ada_w, ada_b, norm1_g, norm2_g, mla_w_dq, mla_g_q, mla_w_uq, mla_w_dkv, mla_g_kv, mla_w_ukv, mla_g_qn, mla_g_kn, mla_w_o, rg_w_in, rg_w_gate, rg_conv_w, rg_conv_b, rg_w_a, rg_b_a, rg_w_x, rg_b_x, rg_lam, rg_w_out, ffn_w1, ffn_w3, ffn_w2, moe_w_router, moe_w1, moe_w3, moe_w2):
    b, n, d = x.shape
    depth = ada_w.shape[0]
    n_experts = moe_w_router.shape[2]

    rows = 2 * SUBLANES
    cvec = jnp.concatenate([c, c_ctx[None, :], jnp.zeros((rows - b - 1, d), F32)], axis=0)
    mods = _ada_all(cvec, ada_w, ada_b)
    moe_w = (moe_w1.astype(BF16), moe_w3.astype(BF16), moe_w2.astype(BF16))

    for i in range(depth):
        need_ctx = i < depth - 1
        j = i // 2
        chunks = [mods[i, :, k * d:(k + 1) * d] for k in range(6)]
        mods_l = [m[:b, None, :] for m in chunks]
        mods_c = [jnp.broadcast_to(m[b:b + 1, None, :], (b, 1, d)) for m in chunks]
        n1 = norm1_g[i][None, :]
        n2 = norm2_g[i][None, :]
        if i % 2 == 0:
            w = _prep_mla_weights(mla_w_dq[j], mla_g_q[j], mla_w_uq[j], mla_w_dkv[j], mla_g_kv[j],
                                  mla_w_ukv[j], mla_g_qn[j], mla_g_kn[j])
            ffn = (ffn_w1[j].astype(BF16), ffn_w3[j].astype(BF16), ffn_w2[j].astype(BF16))
            x, ctx = _mla_layer(x, ctx, mods_l, mods_c, n1, n2, w, mla_w_o[j].astype(BF16), ffn, need_ctx)
        else:
            rg = (rg_w_in[j].astype(BF16), rg_w_gate[j].astype(BF16), rg_conv_w[j], rg_conv_b[j][None, :],
                  rg_w_a[j].astype(BF16), rg_b_a[j][:, None, :], rg_w_x[j].astype(BF16), rg_b_x[j][:, None, :],
                  rg_lam[j][:, None, :], rg_w_out[j].astype(BF16))
            w_router_pad = jnp.concatenate(
                [moe_w_router[j], jnp.zeros((d, LANES - n_experts), F32)], axis=1)
            moe = (w_router_pad, *moe_w, j, n_experts)
            defer = i + 1 < depth
            x, ctx = _rg_layer(x, ctx, mods_l, mods_c, n1, n2, rg, moe, need_ctx, defer)
    return x
```

```python
import functools
import math

import jax
import jax.numpy as jnp
from jax import lax
from jax.experimental import pallas as pl
from jax.experimental.pallas import tpu as pltpu

F32 = jnp.float32
BF16 = jnp.bfloat16

EPS = 1e-6
GRID_W = 64
N_HEADS = 8
D_NOPE = 128
D_ROPE = 64
D_V = 128
D_QK = D_NOPE + D_ROPE
D_HEAD_PAD = 256
D_VX = D_V + 16
Q_EXT = 384
ROPE_BASE = 10000.0
RG_BLOCKS = 4
RG_C = 8.0
TOP_K = 2
LANES = 128
SUBLANES = 8
MXU_COLS = 256
MAX_HIDDEN_CHUNK = 3072
VMEM_LIMIT = 56 * 1024 * 1024
LOG2E = 1.4426950408889634


def _cparams(sem):
    return pltpu.CompilerParams(dimension_semantics=sem, vmem_limit_bytes=VMEM_LIMIT)


def _resident(shape):
    nd = len(shape)
    return pl.BlockSpec(shape, lambda *_: (0,) * nd, pipeline_mode=pl.Buffered(1))


def _dot(a, b):
    return jnp.dot(a, b, preferred_element_type=F32)


def _sigmoid(x):
    return 0.5 * jnp.tanh(0.5 * x) + 0.5


def _rms(xf, g):
    return xf * lax.rsqrt(jnp.mean(xf * xf, axis=-1, keepdims=True) + EPS) * g


def _prenorm(xf, g, shift, scale):
    return _rms(xf, g) * (1.0 + scale) + shift


def _ada_kernel(c_ref, w_ref, b_ref, o_ref):
    s = jax.nn.silu(c_ref[...])
    o_ref[0] = jnp.dot(s, w_ref[0], precision=lax.Precision.HIGHEST,
                       preferred_element_type=F32) + b_ref[0]


def _ada_all(cvec, ada_w, ada_b):
    depth, d, n6 = ada_w.shape
    rows = cvec.shape[0]
    tn = 1536
    return pl.pallas_call(
        _ada_kernel,
        grid=(depth, n6 // tn),
        in_specs=[pl.BlockSpec((rows, d), lambda l, j: (0, 0)),
                  pl.BlockSpec((1, d, tn), lambda l, j: (l, 0, j)),
                  pl.BlockSpec((1, 1, tn), lambda l, j: (l, 0, j))],
        out_specs=pl.BlockSpec((1, rows, tn), lambda l, j: (l, 0, j)),
        out_shape=jax.ShapeDtypeStruct((depth, rows, n6), F32),
        compiler_params=_cparams(("arbitrary", "arbitrary")),
        name="ada_mod",
    )(cvec, ada_w, ada_b.reshape(depth, 1, n6))


def _mla_pre_kernel(*refs, q_scale, n_x):
    (sh_ref, sc_ref, ng_ref, cos_ref, sin_ref, wdq_ref, gq_ref, wuq_ref, gqn_ref,
     wdkv_ref, gkv_ref, wuk_ref, wuvt_ref, gkn_ref, q_ref, k_ref, vt_ref) = refs[n_x:n_x + 17]
    if n_x == 1:
        x = refs[0][0]
    else:
        x1_ref, y0_ref, y1_ref, route_ref, g2_ref = refs[:n_x]
        x = _moe_combined(x1_ref, y0_ref, y1_ref, route_ref, g2_ref)
        refs[n_x + 17][0] = x
    h = _prenorm(x, ng_ref[...], sh_ref[0], sc_ref[0]).astype(BF16)
    cosv = cos_ref[...]
    sinv = sin_ref[...]

    qn = _rms(_dot(h, wdq_ref[...]), gq_ref[...]).astype(BF16)
    qall = _dot(qn, wuq_ref[...])
    g_n = gqn_ref[:, 0:LANES]
    g_a = gqn_ref[:, LANES:2 * LANES]
    g_b = gqn_ref[:, 2 * LANES:3 * LANES]
    for hh in range(N_HEADS):
        base = hh * Q_EXT
        nope = qall[:, base:base + LANES]
        ra = qall[:, base + LANES:base + 2 * LANES]
        rb = qall[:, base + 2 * LANES:base + 3 * LANES]
        ss = jnp.sum(nope * nope, axis=-1, keepdims=True) + jnp.sum(ra * ra, axis=-1, keepdims=True)
        inv = lax.rsqrt(ss * (1.0 / D_QK) + EPS) * q_scale
        q_ref[0, hh, :, 0:LANES] = (nope * g_n * inv).astype(BF16)
        q_ref[0, hh, :, LANES:2 * LANES] = ((ra * g_a * cosv + rb * g_b * sinv) * inv).astype(BF16)

    kva = _dot(h, wdkv_ref[...])
    ckv = _rms(kva[:, 0:LANES], gkv_ref[...]).astype(BF16)
    pa = kva[:, LANES:2 * LANES]
    pb = kva[:, 2 * LANES:3 * LANES]
    pe_ss = jnp.sum(pa * pa, axis=-1, keepdims=True)
    k_n = gkn_ref[:, 0:LANES]
    k_a = gkn_ref[:, LANES:2 * LANES]
    k_b = gkn_ref[:, 2 * LANES:3 * LANES]
    rope = pa * k_a * cosv + pb * k_b * sinv
    knope = _dot(ckv, wuk_ref[...])
    vt_all = lax.dot_general(wuvt_ref[...], ckv, (((1,), (1,)), ((), ())),
                             preferred_element_type=F32)
    for hh in range(N_HEADS):
        kn = knope[:, hh * LANES:(hh + 1) * LANES]
        ss = jnp.sum(kn * kn, axis=-1, keepdims=True) + pe_ss
        inv = lax.rsqrt(ss * (1.0 / D_QK) + EPS)
        k_ref[0, hh, :, 0:LANES] = (kn * k_n * inv).astype(BF16)
        k_ref[0, hh, :, LANES:2 * LANES] = (rope * inv).astype(BF16)
        vt_ref[0, hh, 0, 0:D_V, :] = vt_all[hh * D_V:(hh + 1) * D_V, :].astype(BF16)
        vt_ref[0, hh, 0, D_V:D_VX, :] = jnp.ones((D_VX - D_V, vt_all.shape[1]), BF16)


def _moe_combined(x1_ref, y0_ref, y1_ref, route_ref, g2_ref):
    r = route_ref[0]
    return x1_ref[0] + g2_ref[0] * (r[:, 2:3] * y0_ref[...] + r[:, 3:4] * y1_ref[...])


def _moe_pending_specs(b, n, d, tm):
    nt = n // tm
    tok = lambda bi, i: (bi, i, 0)
    return [pl.BlockSpec((1, tm, d), tok),
            pl.BlockSpec((tm, d), lambda bi, i: (bi * nt + i, 0)),
            pl.BlockSpec((tm, d), lambda bi, i: (b * nt + bi * nt + i, 0)),
            pl.BlockSpec((1, tm, LANES), tok),
            pl.BlockSpec((1, 1, d), lambda bi, i: (bi, 0, 0))]


def _mla_pre(x, shift, scale, norm_g, cos_t, sin_t, w, tm):
    pending = isinstance(x, tuple)
    b, n, d = x[0].shape if pending else x.shape
    nt = n // tm
    q_scale = (D_QK ** -0.5) * LOG2E
    tok = lambda bi, i: (bi, i, 0)
    vec = lambda bi, i: (bi, 0, 0)
    out_shapes = (jax.ShapeDtypeStruct((b, N_HEADS, n, D_HEAD_PAD), BF16),
                  jax.ShapeDtypeStruct((b, N_HEADS, n, D_HEAD_PAD), BF16),
                  jax.ShapeDtypeStruct((b, N_HEADS, nt, D_VX, tm), BF16))
    weights = (w["w_dq"], w["g_q"], w["w_uq"], w["g_qn"], w["w_dkv"], w["g_kv"], w["w_uk"], w["w_uvt"], w["g_kn"])
    out_specs = (pl.BlockSpec((1, N_HEADS, tm, D_HEAD_PAD), lambda bi, i: (bi, 0, i, 0)),
                 pl.BlockSpec((1, N_HEADS, tm, D_HEAD_PAD), lambda bi, i: (bi, 0, i, 0)),
                 pl.BlockSpec((1, N_HEADS, 1, D_VX, tm), lambda bi, i: (bi, 0, i, 0, 0)))
    if pending:
        x1, y, route, gate2 = x
        x_args = (x1, y, y, route, gate2)
        x_specs = _moe_pending_specs(b, n, d, tm)
        out_specs += (pl.BlockSpec((1, tm, d), tok),)
        out_shapes += (jax.ShapeDtypeStruct((b, n, d), F32),)
    else:
        x_args = (x,)
        x_specs = [pl.BlockSpec((1, tm, d), tok)]
    return pl.pallas_call(
        functools.partial(_mla_pre_kernel, q_scale=q_scale, n_x=len(x_args)),
        grid=(b, nt),
        in_specs=x_specs + [
                  pl.BlockSpec((1, 1, d), vec), pl.BlockSpec((1, 1, d), vec),
                  _resident(norm_g.shape),
                  pl.BlockSpec((tm, LANES), lambda bi, i: (i, 0)),
                  pl.BlockSpec((tm, LANES), lambda bi, i: (i, 0))]
                 + [_resident(a.shape) for a in weights],
        out_specs=out_specs,
        out_shape=out_shapes,
        compiler_params=_cparams(("arbitrary", "arbitrary")),
        name="mla_pre",
    )(*x_args, shift, scale, norm_g, cos_t, sin_t, *weights)


def _attn_kernel(q_ref, *refs, n_src):
    srcs = [(refs[2 * s], refs[2 * s + 1]) for s in range(n_src)]
    o_ref = refs[2 * n_src]
    acc_ref = refs[2 * n_src + 1]
    s_bufs = refs[2 * n_src + 2:]

    q_t = q_ref[0, 0].astype(F32).T.astype(BF16)
    acc_ref[...] = jnp.zeros(acc_ref.shape, F32)

    def produce(kc, s_ref, m_prev):
        s = _dot(kc, q_t)
        s_ref[...] = s
        return jnp.maximum(m_prev, jnp.max(s, axis=0, keepdims=True))

    def consume(s_ref, vt_ref, j, m_cur, m_prev):
        alpha = jnp.exp2(m_prev - m_cur)
        p = jnp.exp2(s_ref[...] - m_cur).astype(BF16)
        tkv = vt_ref.shape[4]
        pv = None
        for c in range(s_ref.shape[0] // tkv):
            part = _dot(vt_ref[0, 0, j * (s_ref.shape[0] // tkv) + c], p[c * tkv:(c + 1) * tkv, :])
            pv = part if pv is None else pv + part
        acc_ref[...] = alpha * acc_ref[...] + pv

    m_init = jnp.full((1, acc_ref.shape[1]), -1e30, F32)
    k_ref, vt_ref = srcs[0]
    n = k_ref.shape[2]
    if n == 1:
        m_cur = produce(k_ref[0, 0, 0], s_bufs[0], m_init)
        m_prev = m_init
        pending = (s_bufs[0], vt_ref, 0)
    else:
        s_a, s_b = s_bufs[0], s_bufs[1]
        m0 = produce(k_ref[0, 0, 0], s_a, m_init)

        def body(jj, carry):
            m_prev, m_cur = carry
            j = 2 * jj
            m_1 = produce(k_ref[0, 0, j + 1], s_b, m_cur)
            consume(s_a, vt_ref, j, m_cur, m_prev)
            m_2 = produce(k_ref[0, 0, j + 2], s_a, m_1)
            consume(s_b, vt_ref, j + 1, m_1, m_cur)
            return m_1, m_2

        m_prev, m_cur = lax.fori_loop(0, n // 2 - 1, body, (m_init, m0))
        m_1 = produce(k_ref[0, 0, n - 1], s_b, m_cur)
        consume(s_a, vt_ref, n - 2, m_cur, m_prev)
        m_prev, m_cur = m_cur, m_1
        pending = (s_b, vt_ref, n - 1)

    for kx_ref, vtx_ref in srcs[1:]:
        m_x = produce(kx_ref[0, 0, 0], s_bufs[-1], m_cur)
        consume(pending[0], pending[1], pending[2], m_cur, m_prev)
        m_prev, m_cur = m_cur, m_x
        pending = (s_bufs[-1], vtx_ref, 0)
    consume(pending[0], pending[1], pending[2], m_cur, m_prev)

    o = acc_ref[0:D_V, :] * (1.0 / acc_ref[D_V:D_V + 1, :])
    o_ref[0] = o.T.astype(BF16)


def _attention(q, srcs, tq):
    b, h, nq, dp = q.shape
    in_specs = [pl.BlockSpec((1, 1, tq, dp), lambda bi, hi, i: (bi, hi, i, 0))]
    args = [q]
    for k5, vt5 in srcs:
        in_specs.append(pl.BlockSpec((1, 1) + k5.shape[2:], lambda bi, hi, i: (bi, hi, 0, 0, 0)))
        in_specs.append(pl.BlockSpec((1, 1) + vt5.shape[2:], lambda bi, hi, i: (bi, hi, 0, 0, 0)))
        args += [k5, vt5]
    n0, tk0 = srcs[0][0].shape[2:4]
    assert n0 == 1 or n0 % 2 == 0, n0
    assert all(k5.shape[2] == 1 for k5, _ in srcs[1:])
    s_shapes = [pltpu.VMEM((tk0, tq), F32)] * (1 if n0 == 1 else 2)
    s_shapes += [pltpu.VMEM((k5.shape[3], tq), F32) for k5, _ in srcs[1:2]]
    return pl.pallas_call(
        functools.partial(_attn_kernel, n_src=len(srcs)),
        grid=(b, h, nq // tq),
        in_specs=in_specs,
        out_specs=pl.BlockSpec((1, tq, D_V), lambda bi, hi, i: (bi, i, hi)),
        out_shape=jax.ShapeDtypeStruct((b, nq, h * D_V), BF16),
        scratch_shapes=[pltpu.VMEM((D_VX, tq), F32)] + s_shapes,
        compiler_params=_cparams(("arbitrary", "arbitrary", "arbitrary")),
        name="mla_attn",
    )(*args)


def _mla_ffn_kernel(x_ref, o_ref, g1_ref, sh_ref, sc_ref, g2_ref, ng_ref,
                    wo_ref, w1_ref, w3_ref, w2_ref, out_ref, *, n_chunks):
    x1 = x_ref[0] + g1_ref[0] * _dot(o_ref[0], wo_ref[...])
    h2 = _prenorm(x1, ng_ref[...], sh_ref[0], sc_ref[0]).astype(BF16)
    hid = w1_ref.shape[1]
    hc = hid // n_chunks
    y = None
    for c in range(n_chunks):
        a = _dot(h2, w1_ref[:, c * hc:(c + 1) * hc])
        g = _dot(h2, w3_ref[:, c * hc:(c + 1) * hc])
        act = (jax.nn.silu(a) * g).astype(BF16)
        yc = _dot(act, w2_ref[c * hc:(c + 1) * hc, :])
        y = yc if y is None else y + yc
    out_ref[0] = x1 + g2_ref[0] * y


def _mla_ffn(x, o, gate1, shift2, scale2, gate2, norm_g, w_o, w1, w3, w2, tm):
    b, n, d = x.shape
    hid = w1.shape[1]
    n_chunks = _n_hidden_chunks(hid)
    tok = lambda bi, i: (bi, i, 0)
    vec = lambda bi, i: (bi, 0, 0)
    return pl.pallas_call(
        functools.partial(_mla_ffn_kernel, n_chunks=n_chunks),
        grid=(b, n // tm),
        in_specs=[pl.BlockSpec((1, tm, d), tok), pl.BlockSpec((1, tm, o.shape[2]), tok),
                  pl.BlockSpec((1, 1, d), vec), pl.BlockSpec((1, 1, d), vec),
                  pl.BlockSpec((1, 1, d), vec), pl.BlockSpec((1, 1, d), vec),
                  _resident(norm_g.shape), _resident(w_o.shape),
                  _resident(w1.shape), _resident(w3.shape), _resident(w2.shape)],
        out_specs=pl.BlockSpec((1, tm, d), tok),
        out_shape=jax.ShapeDtypeStruct((b, n, d), F32),
        compiler_params=_cparams(("arbitrary", "arbitrary")),
        name="mla_ffn",
    )(x, o, gate1, shift2, scale2, gate2, norm_g, w_o, w1, w3, w2)


def _rg_in_kernel(xp_ref, xm_ref, xn_ref, sh_ref, sc_ref, ng_ref, win_ref, wg_ref, cw_ref, cb_ref,
                  u_ref, gl_ref, *, tm):
    i = pl.program_id(1)
    nt = pl.num_programs(1)
    x_ext = jnp.concatenate([xp_ref[0], xm_ref[0], xn_ref[0]], axis=0)
    h = _prenorm(x_ext, ng_ref[...], sh_ref[0], sc_ref[0]).astype(BF16)
    u_ext = _dot(h, win_ref[...])
    row = lax.broadcasted_iota(jnp.int32, (tm + 2 * SUBLANES, 1), 0)
    valid = jnp.logical_and(jnp.logical_or(row >= SUBLANES, i > 0),
                            jnp.logical_or(row < tm + SUBLANES, i < nt - 1))
    u_ext = jnp.where(valid, u_ext, 0.0)
    acc = cb_ref[...] + cw_ref[0:1, :] * u_ext[6:6 + tm]
    for k in range(1, 4):
        acc = acc + cw_ref[k:k + 1, :] * u_ext[6 + k:6 + k + tm]
    u_ref[0] = acc.astype(BF16)
    gl_ref[0] = jax.nn.gelu(_dot(h[SUBLANES:SUBLANES + tm], wg_ref[...])).astype(BF16)


def _rg_in(x, shift, scale, norm_g, w_in, w_gate, conv_w, conv_b, tm):
    b, n, d = x.shape
    c = w_in.shape[1]
    r = tm // SUBLANES
    last = n // SUBLANES - 1
    vec = lambda bi, i: (bi, 0, 0)
    tok = lambda bi, i: (bi, i, 0)
    return pl.pallas_call(
        functools.partial(_rg_in_kernel, tm=tm),
        grid=(b, n // tm),
        in_specs=[pl.BlockSpec((1, SUBLANES, d), lambda bi, i: (bi, jnp.maximum(i * r - 1, 0), 0)),
                  pl.BlockSpec((1, tm, d), tok),
                  pl.BlockSpec((1, SUBLANES, d), lambda bi, i: (bi, jnp.minimum((i + 1) * r, last), 0)),
                  pl.BlockSpec((1, 1, d), vec), pl.BlockSpec((1, 1, d), vec),
                  _resident(norm_g.shape), _resident(w_in.shape), _resident(w_gate.shape),
                  _resident(conv_w.shape), _resident(conv_b.shape)],
        out_specs=(pl.BlockSpec((1, tm, c), tok), pl.BlockSpec((1, tm, c), tok)),
        out_shape=(jax.ShapeDtypeStruct((b, n, c), BF16), jax.ShapeDtypeStruct((b, n, c), BF16)),
        compiler_params=_cparams(("arbitrary", "arbitrary")),
        name="rg_in",
    )(x, x, x, shift, scale, norm_g, w_in, w_gate, conv_w, conv_b)


def _rg_scan_kernel(u_ref, h0_ref, wa_ref, ba_ref, wx_ref, bx_ref, lam_ref,
                    hout_ref, hfin_ref, a_s, b_s, o_s, h_s, *, reverse, tn, pitch, nb):
    @pl.when(pl.program_id(1) == 0)
    def _():
        h_s[...] = h0_ref[...]

    width = u_ref.shape[2]
    bw = width // RG_BLOCKS
    n_ct = width // LANES
    u = u_ref[...].reshape(nb * tn, width)
    for blk in range(RG_BLOCKS):
        cs = slice(blk * bw, (blk + 1) * bw)
        ub = u[:, cs]
        r = _sigmoid(_dot(ub, wa_ref[blk]) + ba_ref[:, cs])
        ig = _sigmoid(_dot(ub, wx_ref[blk]) + bx_ref[:, cs])
        a = jnp.exp2((-RG_C * LOG2E) * jax.nn.softplus(-lam_ref[:, cs]) * r)
        z = 1.0 - a * a
        bb = (z * lax.rsqrt(jnp.maximum(z, 1e-30))) * (ig * ub.astype(F32))
        for bi in range(nb):
            for half in range(bw // LANES):
                j = blk * (bw // LANES) + half
                rows = slice(bi * tn, (bi + 1) * tn)
                a_s[bi, j * pitch:j * pitch + tn, :] = a[rows, half * LANES:(half + 1) * LANES]
                b_s[bi, j * pitch:j * pitch + tn, :] = bb[rows, half * LANES:(half + 1) * LANES]

    def body(g, hs):
        for s in range(SUBLANES):
            t = g * SUBLANES + s
            if reverse:
                t = tn - 1 - t
            idx = pl.ds(t, n_ct, stride=pitch)
            hs = tuple(a_s[bi, idx, :] * hs[bi] + b_s[bi, idx, :] for bi in range(nb))
            for bi in range(nb):
                o_s[bi, idx, :] = hs[bi]
        return hs

    hs = lax.fori_loop(0, tn // SUBLANES, body, tuple(h_s[bi] for bi in range(nb)))
    for bi in range(nb):
        h_s[bi] = hs[bi]
        hfin_ref[bi] = hs[bi]
        for j in range(n_ct):
            hout_ref[bi, :, j * LANES:(j + 1) * LANES] = o_s[bi, j * pitch:j * pitch + tn, :].astype(BF16)


def _rg_scan(u, h0, w_a, b_a, w_x, b_x, lam, reverse, tn):
    b, n, c = u.shape
    nt = n // tn
    n_ct = c // LANES
    nb = 4 if b % 4 == 0 else (2 if b % 2 == 0 else 1)
    pitch = tn + SUBLANES
    tmap = (lambda bi, i: (bi, nt - 1 - i, 0)) if reverse else (lambda bi, i: (bi, i, 0))
    st = lambda bi, i: (bi, 0, 0)
    scr = pltpu.VMEM((nb, n_ct * pitch, LANES), F32)
    return pl.pallas_call(
        functools.partial(_rg_scan_kernel, reverse=reverse, tn=tn, pitch=pitch, nb=nb),
        grid=(b // nb, nt),
        in_specs=[pl.BlockSpec((nb, tn, c), tmap), pl.BlockSpec((nb, n_ct, LANES), st),
                  _resident(w_a.shape), _resident(b_a.shape), _resident(w_x.shape),
                  _resident(b_x.shape), _resident(lam.shape)],
        out_specs=(pl.BlockSpec((nb, tn, c), tmap), pl.BlockSpec((nb, n_ct, LANES), st)),
        out_shape=(jax.ShapeDtypeStruct((b, n, c), BF16), jax.ShapeDtypeStruct((b, n_ct, LANES), F32)),
        scratch_shapes=[scr, scr, scr, pltpu.VMEM((nb, n_ct, LANES), F32)],
        compiler_params=_cparams(("arbitrary", "arbitrary")),
        name="rg_scan_bwd" if reverse else "rg_scan_fwd",
    )(u, h0, w_a, b_a, w_x, b_x, lam)


def _rg_out_kernel(x_ref, hf_ref, hb_ref, gl_ref, g1_ref, sh_ref, sc_ref, ng_ref, wout_ref, wr_ref,
                   x1_ref, h2_ref, route_ref, *, n_experts):
    y = ((hf_ref[0].astype(F32) + hb_ref[0].astype(F32)) * gl_ref[0].astype(F32)).astype(BF16)
    x1 = x_ref[0] + g1_ref[0] * _dot(y, wout_ref[...])
    x1_ref[0] = x1
    h2 = _prenorm(x1, ng_ref[...], sh_ref[0], sc_ref[0])
    h2_ref[0] = h2
    wr = wr_ref[...]
    w_hi = wr.astype(BF16)
    w_lo = (wr - w_hi.astype(F32)).astype(BF16)
    w_hl = jnp.concatenate([w_hi, w_lo], axis=1)
    h_hi = h2.astype(BF16)
    h_lo = (h2 - h_hi.astype(F32)).astype(BF16)
    parts = _dot(h_hi, w_hl) + _dot(h_lo, w_hl)
    logits = parts[:, 0:LANES] + parts[:, LANES:2 * LANES]
    lane = lax.broadcasted_iota(jnp.int32, logits.shape, 1).astype(F32)
    neg = jnp.float32(-jnp.inf)
    lg = jnp.where(lane < n_experts, logits, neg)
    m1 = jnp.max(lg, axis=-1, keepdims=True)
    i1 = jnp.min(jnp.where(lg == m1, lane, float(LANES)), axis=-1, keepdims=True)
    lg2 = jnp.where(lane == i1, neg, lg)
    m2 = jnp.max(lg2, axis=-1, keepdims=True)
    i2 = jnp.min(jnp.where(lg2 == m2, lane, float(LANES)), axis=-1, keepdims=True)
    e2 = jnp.exp(m2 - m1)
    den = 1.0 + e2
    route_ref[0] = (jnp.where(lane == 0.0, i1, 0.0) + jnp.where(lane == 1.0, i2, 0.0)
                    + jnp.where(lane == 2.0, 1.0 / den, 0.0) + jnp.where(lane == 3.0, e2 / den, 0.0))


def _rg_out(x, hf, hb, gl, gate1, shift2, scale2, norm_g, w_out, w_router_pad, n_experts, tm):
    b, n, d = x.shape
    c = hf.shape[2]
    tok = lambda bi, i: (bi, i, 0)
    vec = lambda bi, i: (bi, 0, 0)
    return pl.pallas_call(
        functools.partial(_rg_out_kernel, n_experts=n_experts),
        grid=(b, n // tm),
        in_specs=[pl.BlockSpec((1, tm, d), tok), pl.BlockSpec((1, tm, c), tok), pl.BlockSpec((1, tm, c), tok),
                  pl.BlockSpec((1, tm, c), tok),
                  pl.BlockSpec((1, 1, d), vec), pl.BlockSpec((1, 1, d), vec), pl.BlockSpec((1, 1, d), vec),
                  _resident(norm_g.shape), _resident(w_out.shape), _resident(w_router_pad.shape)],
        out_specs=(pl.BlockSpec((1, tm, d), tok), pl.BlockSpec((1, tm, d), tok),
                   pl.BlockSpec((1, tm, LANES), tok)),
        out_shape=(jax.ShapeDtypeStruct((b, n, d), F32), jax.ShapeDtypeStruct((b, n, d), F32),
                   jax.ShapeDtypeStruct((b, n, LANES), F32)),
        compiler_params=_cparams(("arbitrary", "arbitrary")),
        name="rg_out_router",
    )(x, hf, hb, gl, gate1, shift2, scale2, norm_g, w_out, w_router_pad)


def _route_tables(e_idx, tm, n_experts):
    t = e_idx.shape[0]
    n_slots = TOP_K * t
    n_tiles = n_slots // tm + n_experts
    e_flat = e_idx.T.reshape(n_slots)
    experts = jnp.arange(n_experts, dtype=jnp.int32)
    counts = jnp.sum((e_flat[:, None] == experts[None, :]).astype(jnp.int32), axis=0)
    padded = (counts + tm - 1) // tm * tm
    ends = jnp.cumsum(padded)
    n_used = (ends[-1] // tm).astype(jnp.int32)
    fill_rank = jnp.arange(tm, dtype=jnp.int32)
    fill_keys = jnp.where(fill_rank[None, :] < (padded - counts)[:, None], experts[:, None], n_experts)
    keys = jnp.concatenate([e_flat, fill_keys.reshape(n_experts * tm)])
    p = jnp.arange(n_tiles * tm, dtype=jnp.int32)
    bits = (n_tiles * tm - 1).bit_length()
    assert (n_experts + 1) << bits < 2 ** 31
    src = jnp.sort(keys * (1 << bits) + p) & ((1 << bits) - 1)
    valid = src < n_slots
    tok = jnp.where(valid, src % t, 0)
    dst = jnp.where(valid, src, n_slots + ((p // tm + 1) % 2) * tm + p % tm)
    tile_ids = jnp.arange(n_tiles, dtype=jnp.int32)
    tile_e = jnp.sum((tile_ids[:, None] * tm >= ends[None, :]).astype(jnp.int32), axis=1)
    last_e = jnp.max(jnp.where(tile_ids < n_used, tile_e, 0))
    tile_e = jnp.where(tile_ids < n_used, tile_e, last_e)
    fill_dst = (n_slots + tm + jnp.arange(tm, dtype=jnp.int32))[None, :]
    dst_tbl = jnp.concatenate([fill_dst, dst.reshape(n_tiles, tm)], axis=0)
    tok_tbl = jnp.concatenate([tok.reshape(n_tiles, tm), jnp.zeros((1, tm), jnp.int32)], axis=0)
    return (tile_e, n_used.reshape(1), tok_tbl.reshape(n_tiles + 1, 1, tm), dst_tbl.reshape(n_tiles + 1, 1, tm))


def _moe_group_kernel(te_ref, nu_ref, tok0_ref, tokn_ref, dstp_ref, dstl_ref, h2_hbm, w1_ref, w3_ref, w2_ref,
                      y_hbm, xbuf, ybuf, gsem, ssem, *, tm, n_chunks):
    i = pl.program_id(0)
    n_used = nu_ref[0]
    slot = lax.rem(i, 2)

    def gather_row(tok_ref, s, r):
        pltpu.make_async_copy(h2_hbm.at[pl.ds(tok_ref[0, 0, r], 1)], xbuf.at[s, pl.ds(r, 1)], gsem.at[s]).start()

    def scatter_row(dst_ref, s, r):
        pltpu.make_async_copy(ybuf.at[s, pl.ds(r, 1)], y_hbm.at[pl.ds(dst_ref[0, 0, r], 1)], ssem.at[s]).start()

    def wait_gather(s):
        pltpu.make_async_copy(h2_hbm.at[pl.ds(0, tm)], xbuf.at[s], gsem.at[s]).wait()

    def wait_scatter(s):
        pltpu.make_async_copy(ybuf.at[s], y_hbm.at[pl.ds(0, tm)], ssem.at[s]).wait()

    @pl.when(i == 0)
    def _():
        ybuf[...] = jnp.zeros(ybuf.shape, F32)
        tail = y_hbm.shape[0] - 2 * tm
        pltpu.make_async_copy(ybuf.at[0], y_hbm.at[pl.ds(tail, tm)], ssem.at[0]).start()

        def body(r, carry):
            gather_row(tok0_ref, 0, r)
            return carry
        lax.fori_loop(0, tm, body, 0, unroll=8)

    @pl.when(i < n_used)
    def _():
        wait_gather(slot)
        wait_scatter(slot)
        x = xbuf[slot].astype(BF16)
        hid = w1_ref.shape[3]
        hc = hid // n_chunks
        rows = tm // n_chunks
        y = None
        for c in range(n_chunks):
            for r in range(c * rows, (c + 1) * rows):
                gather_row(tokn_ref, 1 - slot, r)
                scatter_row(dstp_ref, 1 - slot, r)
            a = _dot(x, w1_ref[0, 0, :, c * hc:(c + 1) * hc])
            g = _dot(x, w3_ref[0, 0, :, c * hc:(c + 1) * hc])
            act = (jax.nn.silu(a) * g).astype(BF16)
            yc = _dot(act, w2_ref[0, 0, c * hc:(c + 1) * hc, :])
            y = yc if y is None else y + yc
        ybuf[slot] = y

    @pl.when(i == pl.num_programs(0) - 1)
    def _():
        last = lax.rem(n_used + 1, 2)

        def body(r, carry):
            scatter_row(dstl_ref, last, r)
            return carry
        lax.fori_loop(0, tm, body, 0, unroll=8)
        wait_scatter(last)
        wait_scatter(1 - last)
        wait_gather(1 - last)


def _moe_group(h2, e_idx, w1, w3, w2, layer, tm):
    t, d = h2.shape
    _, n_e, _, hid = w1.shape
    tile_e, n_used, tok, dst = _route_tables(e_idx, tm, n_e)
    n_tiles = tile_e.shape[0]
    n_chunks = _n_hidden_chunks(hid)
    smem_blk = lambda f: pl.BlockSpec((1, 1, tm), f, memory_space=pltpu.SMEM)
    wspec = lambda shape: pl.BlockSpec((1, 1) + shape, lambda i, te, nu: (layer, te[i], 0, 0),
                                       pipeline_mode=pl.Buffered(1))
    grid_spec = pltpu.PrefetchScalarGridSpec(
        num_scalar_prefetch=2,
        grid=(n_tiles,),
        in_specs=[smem_blk(lambda i, te, nu: (0, 0, 0)),
                  smem_blk(lambda i, te, nu: (i + 1, 0, 0)),
                  smem_blk(lambda i, te, nu: (i, 0, 0)),
                  smem_blk(lambda i, te, nu: (nu[0], 0, 0)),
                  pl.BlockSpec(memory_space=pl.ANY),
                  wspec((d, hid)), wspec((d, hid)), wspec((hid, d))],
        out_specs=pl.BlockSpec(memory_space=pl.ANY),
        scratch_shapes=[pltpu.VMEM((2, tm, d), F32), pltpu.VMEM((2, tm, d), F32),
                        pltpu.SemaphoreType.DMA((2,)), pltpu.SemaphoreType.DMA((2,))],
    )
    return pl.pallas_call(
        functools.partial(_moe_group_kernel, tm=tm, n_chunks=n_chunks),
        grid_spec=grid_spec,
        out_shape=jax.ShapeDtypeStruct((TOP_K * t + 2 * tm, d), F32),
        compiler_params=_cparams(("arbitrary",)),
        name="moe_group",
    )(tile_e, n_used, tok, tok, dst, dst, h2, w1, w3, w2)


def _moe_combine_kernel(x1_ref, y0_ref, y1_ref, route_ref, g2_ref, out_ref):
    out_ref[0] = _moe_combined(x1_ref, y0_ref, y1_ref, route_ref, g2_ref)


def _moe_combine(x1, y, route, gate2, tm):
    b, n, d = x1.shape
    nt = n // tm
    return pl.pallas_call(
        _moe_combine_kernel,
        grid=(b, nt),
        in_specs=_moe_pending_specs(b, n, d, tm),
        out_specs=pl.BlockSpec((1, tm, d), lambda bi, i: (bi, i, 0)),
        out_shape=jax.ShapeDtypeStruct((b, n, d), F32),
        compiler_params=_cparams(("arbitrary", "arbitrary")),
        name="moe_combine",
    )(x1, y, y, route, gate2)


def _moe(h2, route, x1, gate2, w1, w3, w2, layer, tm, defer=False):
    b, n, d = x1.shape
    e_idx = route.reshape(b * n, LANES)[:, :TOP_K].astype(jnp.int32)
    y = _moe_group(h2.reshape(b * n, d), e_idx, w1, w3, w2, layer, tm)
    if defer:
        return (x1, y, route, gate2)
    return _moe_combine(x1, y, route, gate2, tm)


def _prep_mla_weights(w_dq, g_q, w_uq, w_dkv, g_kv, w_ukv, g_qn, g_kn):
    ql = w_uq.shape[0]
    half = D_ROPE // 2
    pad = LANES - D_ROPE

    def ext_cols(w_rope):
        z = jnp.zeros(w_rope.shape[:-1] + (pad,), w_rope.dtype)
        r1, r2 = w_rope[..., :half], w_rope[..., half:]
        return jnp.concatenate([w_rope, z], -1), jnp.concatenate([-r2, r1, z], -1)

    def ext_gain(g):
        z = jnp.zeros((pad,), g.dtype)
        gr1, gr2 = g[D_NOPE:D_NOPE + half], g[D_NOPE + half:]
        return jnp.concatenate([g[:D_NOPE], gr1, gr2, z, gr2, gr1, z])[None, :].astype(F32)

    wq = w_uq.reshape(ql, N_HEADS, D_QK)
    qa, qb = ext_cols(wq[:, :, D_NOPE:])
    w_uq_ext = jnp.concatenate([wq[:, :, :D_NOPE], qa, qb], -1).reshape(ql, N_HEADS * Q_EXT)
    kl = w_ukv.shape[0]
    ka, kb = ext_cols(w_dkv[:, kl:])
    w_dkv_ext = jnp.concatenate([w_dkv[:, :kl], ka, kb], -1)
    wkv = w_ukv.reshape(kl, N_HEADS, D_NOPE + D_V)
    w_uk = wkv[:, :, :D_NOPE].reshape(kl, N_HEADS * D_NOPE)
    w_uvt = wkv[:, :, D_NOPE:].reshape(kl, N_HEADS * D_V).T
    return dict(w_dq=w_dq.astype(BF16), g_q=g_q[None, :].astype(F32), w_uq=w_uq_ext.astype(BF16),
                g_qn=ext_gain(g_qn), w_dkv=w_dkv_ext.astype(BF16), g_kv=g_kv[None, :].astype(F32),
                w_uk=w_uk.astype(BF16), w_uvt=w_uvt.astype(BF16), g_kn=ext_gain(g_kn))


def _rope_tables(n_tok):
    rows = n_tok // GRID_W
    row = jnp.repeat(jnp.arange(rows, dtype=F32), GRID_W)
    col = jnp.tile(jnp.arange(GRID_W, dtype=F32), rows)
    n_freq = D_ROPE // 4
    inv = ROPE_BASE ** (-jnp.arange(n_freq, dtype=F32) / n_freq)
    ang = jnp.concatenate([row[:, None] * inv, col[:, None] * inv], axis=-1)
    z = jnp.zeros((n_tok, LANES - D_ROPE), F32)
    cos, sin = jnp.cos(ang), jnp.sin(ang)
    return jnp.concatenate([cos, cos, z], -1), jnp.concatenate([sin, sin, z], -1)


def _no_rope_tables(n_tok):
    one = jnp.ones((n_tok, D_ROPE), F32)
    z = jnp.zeros((n_tok, LANES - D_ROPE), F32)
    return jnp.concatenate([one, z], -1), jnp.zeros((n_tok, LANES), F32)


def _n_hidden_chunks(hid):
    for c in range(1, hid // MXU_COLS + 1):
        if hid % (c * MXU_COLS) == 0 and hid // c <= MAX_HIDDEN_CHUNK:
            return c
    return 1


def _tile(n, target):
    t = min(n, target)
    assert n % t == 0, (n, t)
    return t


def _mla_layer(x, ctx, mods_l, mods_c, norm1_g, norm2_g, w, w_o, ffn, need_ctx):
    b, n, d = x[0].shape if isinstance(x, tuple) else x.shape
    nc = ctx.shape[1]
    tm = _tile(n, 512)
    tmc = _tile(nc, 512)
    sh1, sc1, g1, sh2, sc2, g2 = mods_l
    csh1, csc1, cg1, csh2, csc2, cg2 = mods_c
    cos_l, sin_l = _rope_tables(n)
    cos_c, sin_c = _no_rope_tables(nc)
    ql, kl, vtl, *x_comb = _mla_pre(x, sh1, sc1, norm1_g, cos_l, sin_l, w, tm)
    if x_comb:
        x = x_comb[0]
    qc, kc, vtc = _mla_pre(ctx, csh1, csc1, norm1_g, cos_c, sin_c, w, tmc)
    k5 = lambda k, t: k.reshape(k.shape[0], k.shape[1], k.shape[2] // t, t, k.shape[3])
    src_l = (k5(kl, 2 * tm if (n // tm) % 4 == 0 else tm), vtl)
    src_c = (k5(kc, tmc), vtc)
    ol = _attention(ql, [src_l, src_c], _tile(n, 2048))
    x_new = _mla_ffn(x, ol, g1, sh2, sc2, g2, norm2_g, w_o, *ffn, tm)
    ctx_new = ctx
    if need_ctx:
        oc = _attention(qc, [src_c], _tile(nc, 1024))
        ctx_new = _mla_ffn(ctx, oc, cg1, csh2, csc2, cg2, norm2_g, w_o, *ffn, tmc)
    return x_new, ctx_new


def _rg_layer(x, ctx, mods_l, mods_c, norm1_g, norm2_g, rg, moe, need_ctx, defer):
    b, n, d = x.shape
    nc = ctx.shape[1]
    tm = _tile(n, 512)
    tmc = _tile(nc, 512)
    tn = _tile(n, 256)
    tnc = _tile(nc, 256)
    sh1, sc1, g1, sh2, sc2, g2 = mods_l
    csh1, csc1, cg1, csh2, csc2, cg2 = mods_c
    w_in, w_gate, conv_w, conv_b, w_a, b_a, w_x, b_x, lam, w_out = rg
    w_router_pad, mw1, mw3, mw2, layer, n_experts = moe
    ul, gll = _rg_in(x, sh1, sc1, norm1_g, w_in, w_gate, conv_w, conv_b, tm)
    uc, glc = _rg_in(ctx, csh1, csc1, norm1_g, w_in, w_gate, conv_w, conv_b, tmc)
    zero = jnp.zeros((b, w_in.shape[1] // LANES, LANES), F32)
    hl, hc = [], []
    for dr in range(2):
        args = (w_a[dr], b_a[dr], w_x[dr], b_x[dr], lam[dr])
        hcd, h_end = _rg_scan(uc, zero, *args, reverse=bool(dr), tn=tnc)
        hld, _ = _rg_scan(ul, h_end, *args, reverse=bool(dr), tn=tn)
        hl.append(hld)
        hc.append(hcd)
    x1, h2, route = _rg_out(x, hl[0], hl[1], gll, g1, sh2, sc2, norm2_g, w_out, w_router_pad, n_experts, tm)
    x_new = _moe(h2, route, x1, g2, mw1, mw3, mw2, layer, tm, defer=defer)
    ctx_new = ctx
    if need_ctx:
        c1, ch2, croute = _rg_out(ctx, hc[0], hc[1], glc, cg1, csh2, csc2, norm2_g, w_out, w_router_pad,
                                  n_experts, tmc)
        ctx_new = _moe(ch2, croute, c1, cg2, mw1, mw3, mw2, layer, tmc)
    return x_new, ctx_new


def kernel(x, c, ctx, c_ctx, ada_w, ada_b, norm1_g, norm2_g, mla_w_dq, mla_g_q, mla_w_uq, mla_w_dkv, mla_g_kv, mla_w_ukv, mla_g_qn, mla_g_kn, mla_w_o, rg_w_in, rg_w_gate, rg_conv_w, rg_conv_b, rg_w_a, rg_b_a, rg_w_x, rg_b_x, rg_lam, rg_w_out, ffn_w1, ffn_w3, ffn_w2, moe_w_router, moe_w1, moe_w3, moe_w2):
    b, n, d = x.shape
    depth = ada_w.shape[0]
    n_experts = moe_w_router.shape[2]

    rows = 2 * SUBLANES
    cvec = jnp.concatenate([c, c_ctx[None, :], jnp.zeros((rows - b - 1, d), F32)], axis=0)
    mods = _ada_all(cvec, ada_w, ada_b)
    moe_w = (moe_w1.astype(BF16), moe_w3.astype(BF16), moe_w2.astype(BF16))

    for i in range(depth):
        need_ctx = i < depth - 1
        j = i // 2
        chunks = [mods[i, :, k * d:(k + 1) * d] for k in range(6)]
        mods_l = [m[:b, None, :] for m in chunks]
        mods_c = [jnp.broadcast_to(m[b:b + 1, None, :], (b, 1, d)) for m in chunks]
        n1 = norm1_g[i][None, :]
        n2 = norm2_g[i][None, :]
        if i % 2 == 0:
            w = _prep_mla_weights(mla_w_dq[j], mla_g_q[j], mla_w_uq[j], mla_w_dkv[j], mla_g_kv[j],
                                  mla_w_ukv[j], mla_g_qn[j], mla_g_kn[j])
            ffn = (ffn_w1[j].astype(BF16), ffn_w3[j].astype(BF16), ffn_w2[j].astype(BF16))
            x, ctx = _mla_layer(x, ctx, mods_l, mods_c, n1, n2, w, mla_w_o[j].astype(BF16), ffn, need_ctx)
        else:
            rg = (rg_w_in[j].astype(BF16), rg_w_gate[j].astype(BF16), rg_conv_w[j], rg_conv_b[j][None, :],
                  rg_w_a[j].astype(BF16), rg_b_a[j][:, None, :], rg_w_x[j].astype(BF16), rg_b_x[j][:, None, :],
                  rg_lam[j][:, None, :], rg_w_out[j].astype(BF16))
            w_router_pad = jnp.concatenate(
                [moe_w_router[j], jnp.zeros((d, LANES - n_experts), F32)], axis=1)
            moe = (w_router_pad, *moe_w, j, n_experts)
            defer = i + 1 < depth
            x, ctx = _rg_layer(x, ctx, mods_l, mods_c, n1, n2, rg, moe, need_ctx, defer)
    return x
```

```python
import functools
import math

import jax
import jax.numpy as jnp
from jax import lax
from jax.experimental import pallas as pl
from jax.experimental.pallas import tpu as pltpu

F32 = jnp.float32
BF16 = jnp.bfloat16

EPS = 1e-6
GRID_W = 64
N_HEADS = 8
D_NOPE = 128
D_ROPE = 64
D_V = 128
D_QK = D_NOPE + D_ROPE
D_HEAD_PAD = 256
Q_EXT = 384
ROPE_BASE = 10000.0
RG_BLOCKS = 4
RG_C = 8.0
TOP_K = 2
LANES = 128
SUBLANES = 8
D_VX = D_V + 2 * SUBLANES
MXU_COLS = 256
MAX_HIDDEN_CHUNK = 3072
VMEM_LIMIT = 56 * 1024 * 1024
LOG2E = 1.4426950408889634


def _cparams(sem):
    return pltpu.CompilerParams(dimension_semantics=sem, vmem_limit_bytes=VMEM_LIMIT)


def _resident(shape):
    nd = len(shape)
    return pl.BlockSpec(shape, lambda *_: (0,) * nd, pipeline_mode=pl.Buffered(1))


def _dot(a, b):
    return jnp.dot(a, b, preferred_element_type=F32)


def _sigmoid(x):
    return 0.5 * jnp.tanh(0.5 * x) + 0.5


def _rms(xf, g):
    return xf * lax.rsqrt(jnp.mean(xf * xf, axis=-1, keepdims=True) + EPS) * g


def _prenorm(xf, g, shift, scale):
    return _rms(xf, g) * (1.0 + scale) + shift


def _ada_kernel(c_ref, w_ref, b_ref, o_ref):
    s = jax.nn.silu(c_ref[...])
    o_ref[0] = jnp.dot(s, w_ref[0], precision=lax.Precision.HIGHEST,
                       preferred_element_type=F32) + b_ref[0]


def _ada_all(cvec, ada_w, ada_b):
    depth, d, n6 = ada_w.shape
    rows = cvec.shape[0]
    tn = 1536
    return pl.pallas_call(
        _ada_kernel,
        grid=(depth, n6 // tn),
        in_specs=[pl.BlockSpec((rows, d), lambda l, j: (0, 0)),
                  pl.BlockSpec((1, d, tn), lambda l, j: (l, 0, j)),
                  pl.BlockSpec((1, 1, tn), lambda l, j: (l, 0, j))],
        out_specs=pl.BlockSpec((1, rows, tn), lambda l, j: (l, 0, j)),
        out_shape=jax.ShapeDtypeStruct((depth, rows, n6), F32),
        compiler_params=_cparams(("arbitrary", "arbitrary")),
        name="ada_mod",
    )(cvec, ada_w, ada_b.reshape(depth, 1, n6))


def _mla_pre_kernel(*refs, q_scale, n_x):
    (sh_ref, sc_ref, ng_ref, cos_ref, sin_ref, wdq_ref, gq_ref, wuq_ref, gqn_ref,
     wdkv_ref, gkv_ref, wuk_ref, wuvt_ref, gkn_ref, q_ref, k_ref, vt_ref) = refs[n_x:n_x + 17]
    if n_x == 1:
        x = refs[0][0]
    else:
        x1_ref, y0_ref, y1_ref, route_ref, g2_ref = refs[:n_x]
        x = _moe_combined(x1_ref, y0_ref, y1_ref, route_ref, g2_ref)
        refs[n_x + 17][0] = x
    h = _prenorm(x, ng_ref[...], sh_ref[0], sc_ref[0]).astype(BF16)
    cosv = cos_ref[...]
    sinv = sin_ref[...]

    qn = _rms(_dot(h, wdq_ref[...]), gq_ref[...]).astype(BF16)
    qall = _dot(qn, wuq_ref[...])
    g_n = gqn_ref[:, 0:LANES]
    g_a = gqn_ref[:, LANES:2 * LANES]
    g_b = gqn_ref[:, 2 * LANES:3 * LANES]
    for hh in range(N_HEADS):
        base = hh * Q_EXT
        nope = qall[:, base:base + LANES]
        ra = qall[:, base + LANES:base + 2 * LANES]
        rb = qall[:, base + 2 * LANES:base + 3 * LANES]
        ss = jnp.sum(nope * nope, axis=-1, keepdims=True) + jnp.sum(ra * ra, axis=-1, keepdims=True)
        inv = lax.rsqrt(ss * (1.0 / D_QK) + EPS) * q_scale
        q_ref[0, hh, :, 0:LANES] = (nope * g_n * inv).astype(BF16)
        q_ref[0, hh, :, LANES:2 * LANES] = ((ra * g_a * cosv + rb * g_b * sinv) * inv).astype(BF16)

    kva = _dot(h, wdkv_ref[...])
    ckv = _rms(kva[:, 0:LANES], gkv_ref[...]).astype(BF16)
    pa = kva[:, LANES:2 * LANES]
    pb = kva[:, 2 * LANES:3 * LANES]
    pe_ss = jnp.sum(pa * pa, axis=-1, keepdims=True)
    k_n = gkn_ref[:, 0:LANES]
    k_a = gkn_ref[:, LANES:2 * LANES]
    k_b = gkn_ref[:, 2 * LANES:3 * LANES]
    rope = pa * k_a * cosv + pb * k_b * sinv
    knope = _dot(ckv, wuk_ref[...])
    vt_all = lax.dot_general(wuvt_ref[...], ckv, (((1,), (1,)), ((), ())),
                             preferred_element_type=F32)
    for hh in range(N_HEADS):
        kn = knope[:, hh * LANES:(hh + 1) * LANES]
        ss = jnp.sum(kn * kn, axis=-1, keepdims=True) + pe_ss
        inv = lax.rsqrt(ss * (1.0 / D_QK) + EPS)
        k_ref[0, hh, :, 0:LANES] = (kn * k_n * inv).astype(BF16)
        k_ref[0, hh, :, LANES:2 * LANES] = (rope * inv).astype(BF16)
        vt_ref[0, hh, 0, 0:D_V, :] = vt_all[hh * D_V:(hh + 1) * D_V, :].astype(BF16)
        vt_ref[0, hh, 0, D_V:D_VX, :] = jnp.ones((D_VX - D_V, vt_all.shape[1]), BF16)


def _moe_combined(x1_ref, y0_ref, y1_ref, route_ref, g2_ref):
    r = route_ref[0]
    return x1_ref[0] + g2_ref[0] * (r[:, 2:3] * y0_ref[...] + r[:, 3:4] * y1_ref[...])


def _moe_pending_specs(b, n, d, tm):
    nt = n // tm
    tok = lambda bi, i: (bi, i, 0)
    return [pl.BlockSpec((1, tm, d), tok),
            pl.BlockSpec((tm, d), lambda bi, i: (bi * nt + i, 0)),
            pl.BlockSpec((tm, d), lambda bi, i: (b * nt + bi * nt + i, 0)),
            pl.BlockSpec((1, tm, LANES), tok),
            pl.BlockSpec((1, 1, d), lambda bi, i: (bi, 0, 0))]


def _mla_pre(x, shift, scale, norm_g, cos_t, sin_t, w, tm):
    pending = isinstance(x, tuple)
    b, n, d = x[0].shape if pending else x.shape
    nt = n // tm
    q_scale = (D_QK ** -0.5) * LOG2E
    tok = lambda bi, i: (bi, i, 0)
    vec = lambda bi, i: (bi, 0, 0)
    out_shapes = (jax.ShapeDtypeStruct((b, N_HEADS, n, D_HEAD_PAD), BF16),
                  jax.ShapeDtypeStruct((b, N_HEADS, n, D_HEAD_PAD), BF16),
                  jax.ShapeDtypeStruct((b, N_HEADS, nt, D_VX, tm), BF16))
    weights = (w["w_dq"], w["g_q"], w["w_uq"], w["g_qn"], w["w_dkv"], w["g_kv"], w["w_uk"], w["w_uvt"], w["g_kn"])
    out_specs = (pl.BlockSpec((1, N_HEADS, tm, D_HEAD_PAD), lambda bi, i: (bi, 0, i, 0)),
                 pl.BlockSpec((1, N_HEADS, tm, D_HEAD_PAD), lambda bi, i: (bi, 0, i, 0)),
                 pl.BlockSpec((1, N_HEADS, 1, D_VX, tm), lambda bi, i: (bi, 0, i, 0, 0)))
    if pending:
        x1, y, route, gate2 = x
        x_args = (x1, y, y, route, gate2)
        x_specs = _moe_pending_specs(b, n, d, tm)
        out_specs += (pl.BlockSpec((1, tm, d), tok),)
        out_shapes += (jax.ShapeDtypeStruct((b, n, d), F32),)
    else:
        x_args = (x,)
        x_specs = [pl.BlockSpec((1, tm, d), tok)]
    return pl.pallas_call(
        functools.partial(_mla_pre_kernel, q_scale=q_scale, n_x=len(x_args)),
        grid=(b, nt),
        in_specs=x_specs + [
                  pl.BlockSpec((1, 1, d), vec), pl.BlockSpec((1, 1, d), vec),
                  _resident(norm_g.shape),
                  pl.BlockSpec((tm, LANES), lambda bi, i: (i, 0)),
                  pl.BlockSpec((tm, LANES), lambda bi, i: (i, 0))]
                 + [_resident(a.shape) for a in weights],
        out_specs=out_specs,
        out_shape=out_shapes,
        compiler_params=_cparams(("arbitrary", "arbitrary")),
        name="mla_pre",
    )(*x_args, shift, scale, norm_g, cos_t, sin_t, *weights)


def _attn_kernel(q_ref, *refs, n_src):
    srcs = [(refs[2 * s], refs[2 * s + 1]) for s in range(n_src)]
    o_ref = refs[2 * n_src]
    acc_ref = refs[2 * n_src + 1]
    s_bufs = refs[2 * n_src + 2:]

    q_t = q_ref[0, 0].astype(F32).T.astype(BF16)
    acc_ref[...] = jnp.zeros(acc_ref.shape, F32)

    def produce(kc, s_ref, m_prev):
        s = _dot(kc, q_t)
        s_ref[...] = s
        return jnp.maximum(m_prev, jnp.max(s, axis=0, keepdims=True))

    def consume(s_ref, vt_ref, j, m_cur, m_prev):
        alpha = jnp.exp2(m_prev - m_cur)
        p = jnp.exp2(s_ref[...] - m_cur).astype(BF16)
        tkv = vt_ref.shape[4]
        pv = None
        for c in range(s_ref.shape[0] // tkv):
            part = _dot(vt_ref[0, 0, j * (s_ref.shape[0] // tkv) + c], p[c * tkv:(c + 1) * tkv, :])
            pv = part if pv is None else pv + part
        acc_ref[...] = alpha * acc_ref[...] + pv

    m_init = jnp.full((1, acc_ref.shape[1]), -1e30, F32)
    k_ref, vt_ref = srcs[0]
    n = k_ref.shape[2]
    if n == 1:
        m_cur = produce(k_ref[0, 0, 0], s_bufs[0], m_init)
        m_prev = m_init
        pending = (s_bufs[0], vt_ref, 0)
    else:
        s_a, s_b = s_bufs[0], s_bufs[1]
        m0 = produce(k_ref[0, 0, 0], s_a, m_init)

        def body(jj, carry):
            m_prev, m_cur = carry
            j = 2 * jj
            m_1 = produce(k_ref[0, 0, j + 1], s_b, m_cur)
            consume(s_a, vt_ref, j, m_cur, m_prev)
            m_2 = produce(k_ref[0, 0, j + 2], s_a, m_1)
            consume(s_b, vt_ref, j + 1, m_1, m_cur)
            return m_1, m_2

        m_prev, m_cur = lax.fori_loop(0, n // 2 - 1, body, (m_init, m0))
        m_1 = produce(k_ref[0, 0, n - 1], s_b, m_cur)
        consume(s_a, vt_ref, n - 2, m_cur, m_prev)
        m_prev, m_cur = m_cur, m_1
        pending = (s_b, vt_ref, n - 1)

    for kx_ref, vtx_ref in srcs[1:]:
        m_x = produce(kx_ref[0, 0, 0], s_bufs[-1], m_cur)
        consume(pending[0], pending[1], pending[2], m_cur, m_prev)
        m_prev, m_cur = m_cur, m_x
        pending = (s_bufs[-1], vtx_ref, 0)
    consume(pending[0], pending[1], pending[2], m_cur, m_prev)

    o = acc_ref[0:D_V, :] * (1.0 / acc_ref[D_V:D_V + 1, :])
    o_ref[0] = o.T.astype(BF16)


def _attention(q, srcs, tq):
    b, h, nq, dp = q.shape
    in_specs = [pl.BlockSpec((1, 1, tq, dp), lambda bi, hi, i: (bi, hi, i, 0))]
    args = [q]
    for k5, vt5 in srcs:
        in_specs.append(pl.BlockSpec((1, 1) + k5.shape[2:], lambda bi, hi, i: (bi, hi, 0, 0, 0)))
        in_specs.append(pl.BlockSpec((1, 1) + vt5.shape[2:], lambda bi, hi, i: (bi, hi, 0, 0, 0)))
        args += [k5, vt5]
    n0, tk0 = srcs[0][0].shape[2:4]
    assert n0 == 1 or n0 % 2 == 0, n0
    assert all(k5.shape[2] == 1 for k5, _ in srcs[1:])
    s_shapes = [pltpu.VMEM((tk0, tq), F32)] * (1 if n0 == 1 else 2)
    s_shapes += [pltpu.VMEM((k5.shape[3], tq), F32) for k5, _ in srcs[1:2]]
    return pl.pallas_call(
        functools.partial(_attn_kernel, n_src=len(srcs)),
        grid=(b, h, nq // tq),
        in_specs=in_specs,
        out_specs=pl.BlockSpec((1, tq, D_V), lambda bi, hi, i: (bi, i, hi)),
        out_shape=jax.ShapeDtypeStruct((b, nq, h * D_V), BF16),
        scratch_shapes=[pltpu.VMEM((D_VX, tq), F32)] + s_shapes,
        compiler_params=_cparams(("arbitrary", "arbitrary", "arbitrary")),
        name="mla_attn",
    )(*args)


def _mla_ffn_kernel(x_ref, o_ref, g1_ref, sh_ref, sc_ref, g2_ref, ng_ref,
                    wo_ref, w1_ref, w3_ref, w2_ref, out_ref, *, n_chunks):
    x1 = x_ref[0] + g1_ref[0] * _dot(o_ref[0], wo_ref[...])
    h2 = _prenorm(x1, ng_ref[...], sh_ref[0], sc_ref[0]).astype(BF16)
    hid = w1_ref.shape[1]
    hc = hid // n_chunks
    y = None
    for c in range(n_chunks):
        a = _dot(h2, w1_ref[:, c * hc:(c + 1) * hc])
        g = _dot(h2, w3_ref[:, c * hc:(c + 1) * hc])
        act = (jax.nn.silu(a) * g).astype(BF16)
        yc = _dot(act, w2_ref[c * hc:(c + 1) * hc, :])
        y = yc if y is None else y + yc
    out_ref[0] = x1 + g2_ref[0] * y


def _mla_ffn(x, o, gate1, shift2, scale2, gate2, norm_g, w_o, w1, w3, w2, tm):
    b, n, d = x.shape
    hid = w1.shape[1]
    n_chunks = _n_hidden_chunks(hid)
    tok = lambda bi, i: (bi, i, 0)
    vec = lambda bi, i: (bi, 0, 0)
    return pl.pallas_call(
        functools.partial(_mla_ffn_kernel, n_chunks=n_chunks),
        grid=(b, n // tm),
        in_specs=[pl.BlockSpec((1, tm, d), tok), pl.BlockSpec((1, tm, o.shape[2]), tok),
                  pl.BlockSpec((1, 1, d), vec), pl.BlockSpec((1, 1, d), vec),
                  pl.BlockSpec((1, 1, d), vec), pl.BlockSpec((1, 1, d), vec),
                  _resident(norm_g.shape), _resident(w_o.shape),
                  _resident(w1.shape), _resident(w3.shape), _resident(w2.shape)],
        out_specs=pl.BlockSpec((1, tm, d), tok),
        out_shape=jax.ShapeDtypeStruct((b, n, d), F32),
        compiler_params=_cparams(("arbitrary", "arbitrary")),
        name="mla_ffn",
    )(x, o, gate1, shift2, scale2, gate2, norm_g, w_o, w1, w3, w2)


def _rg_in_kernel(xp_ref, xm_ref, xn_ref, sh_ref, sc_ref, ng_ref, win_ref, wg_ref, cw_ref, cb_ref,
                  u_ref, gl_ref, *, tm):
    i = pl.program_id(1)
    nt = pl.num_programs(1)
    x_ext = jnp.concatenate([xp_ref[0], xm_ref[0], xn_ref[0]], axis=0)
    h = _prenorm(x_ext, ng_ref[...], sh_ref[0], sc_ref[0]).astype(BF16)
    u_ext = _dot(h, win_ref[...])
    row = lax.broadcasted_iota(jnp.int32, (tm + 2 * SUBLANES, 1), 0)
    valid = jnp.logical_and(jnp.logical_or(row >= SUBLANES, i > 0),
                            jnp.logical_or(row < tm + SUBLANES, i < nt - 1))
    u_ext = jnp.where(valid, u_ext, 0.0)
    acc = cb_ref[...] + cw_ref[0:1, :] * u_ext[6:6 + tm]
    for k in range(1, 4):
        acc = acc + cw_ref[k:k + 1, :] * u_ext[6 + k:6 + k + tm]
    u_ref[0] = acc.astype(BF16)
    gl_ref[0] = jax.nn.gelu(_dot(h[SUBLANES:SUBLANES + tm], wg_ref[...])).astype(BF16)


def _rg_in(x, shift, scale, norm_g, w_in, w_gate, conv_w, conv_b, tm):
    b, n, d = x.shape
    c = w_in.shape[1]
    r = tm // SUBLANES
    last = n // SUBLANES - 1
    vec = lambda bi, i: (bi, 0, 0)
    tok = lambda bi, i: (bi, i, 0)
    return pl.pallas_call(
        functools.partial(_rg_in_kernel, tm=tm),
        grid=(b, n // tm),
        in_specs=[pl.BlockSpec((1, SUBLANES, d), lambda bi, i: (bi, jnp.maximum(i * r - 1, 0), 0)),
                  pl.BlockSpec((1, tm, d), tok),
                  pl.BlockSpec((1, SUBLANES, d), lambda bi, i: (bi, jnp.minimum((i + 1) * r, last), 0)),
                  pl.BlockSpec((1, 1, d), vec), pl.BlockSpec((1, 1, d), vec),
                  _resident(norm_g.shape), _resident(w_in.shape), _resident(w_gate.shape),
                  _resident(conv_w.shape), _resident(conv_b.shape)],
        out_specs=(pl.BlockSpec((1, tm, c), tok), pl.BlockSpec((1, tm, c), tok)),
        out_shape=(jax.ShapeDtypeStruct((b, n, c), BF16), jax.ShapeDtypeStruct((b, n, c), BF16)),
        compiler_params=_cparams(("arbitrary", "arbitrary")),
        name="rg_in",
    )(x, x, x, shift, scale, norm_g, w_in, w_gate, conv_w, conv_b)


def _rg_scan_kernel(u_ref, h0_ref, wa_ref, ba_ref, wx_ref, bx_ref, lam_ref,
                    hout_ref, hfin_ref, a_s, b_s, o_s, h_s, *, reverse, tn, pitch, nb):
    @pl.when(pl.program_id(1) == 0)
    def _():
        h_s[...] = h0_ref[...]

    width = u_ref.shape[2]
    bw = width // RG_BLOCKS
    n_ct = width // LANES
    u = u_ref[...].reshape(nb * tn, width)
    for blk in range(RG_BLOCKS):
        cs = slice(blk * bw, (blk + 1) * bw)
        ub = u[:, cs]
        tr = jnp.tanh(_dot(ub, wa_ref[blk]) + ba_ref[:, cs]) + 1.0
        ti = jnp.tanh(_dot(ub, wx_ref[blk]) + bx_ref[:, cs]) + 1.0
        a = jnp.exp2((-0.5 * RG_C * LOG2E) * jax.nn.softplus(-lam_ref[:, cs]) * tr)
        z = 1.0 - a * a
        bb = (z * lax.rsqrt(jnp.maximum(z, 1e-30))) * (ti * (0.5 * ub.astype(F32)))
        for bi in range(nb):
            for half in range(bw // LANES):
                j = blk * (bw // LANES) + half
                rows = slice(bi * tn, (bi + 1) * tn)
                a_s[bi, j * pitch:j * pitch + tn, :] = a[rows, half * LANES:(half + 1) * LANES]
                b_s[bi, j * pitch:j * pitch + tn, :] = bb[rows, half * LANES:(half + 1) * LANES]

    def body(g, hs):
        for s in range(SUBLANES):
            t = g * SUBLANES + s
            if reverse:
                t = tn - 1 - t
            idx = pl.ds(t, n_ct, stride=pitch)
            hs = tuple(a_s[bi, idx, :] * hs[bi] + b_s[bi, idx, :] for bi in range(nb))
            for bi in range(nb):
                o_s[bi, idx, :] = hs[bi]
        return hs

    hs = lax.fori_loop(0, tn // SUBLANES, body, tuple(h_s[bi] for bi in range(nb)))
    for bi in range(nb):
        h_s[bi] = hs[bi]
        hfin_ref[bi] = hs[bi]
        for j in range(n_ct):
            hout_ref[bi, :, j * LANES:(j + 1) * LANES] = o_s[bi, j * pitch:j * pitch + tn, :].astype(BF16)


def _rg_scan(u, h0, w_a, b_a, w_x, b_x, lam, reverse, tn):
    b, n, c = u.shape
    nt = n // tn
    n_ct = c // LANES
    nb = 4 if b % 4 == 0 else (2 if b % 2 == 0 else 1)
    pitch = tn + SUBLANES
    tmap = (lambda bi, i: (bi, nt - 1 - i, 0)) if reverse else (lambda bi, i: (bi, i, 0))
    st = lambda bi, i: (bi, 0, 0)
    scr = pltpu.VMEM((nb, n_ct * pitch, LANES), F32)
    return pl.pallas_call(
        functools.partial(_rg_scan_kernel, reverse=reverse, tn=tn, pitch=pitch, nb=nb),
        grid=(b // nb, nt),
        in_specs=[pl.BlockSpec((nb, tn, c), tmap), pl.BlockSpec((nb, n_ct, LANES), st),
                  _resident(w_a.shape), _resident(b_a.shape), _resident(w_x.shape),
                  _resident(b_x.shape), _resident(lam.shape)],
        out_specs=(pl.BlockSpec((nb, tn, c), tmap), pl.BlockSpec((nb, n_ct, LANES), st)),
        out_shape=(jax.ShapeDtypeStruct((b, n, c), BF16), jax.ShapeDtypeStruct((b, n_ct, LANES), F32)),
        scratch_shapes=[scr, scr, scr, pltpu.VMEM((nb, n_ct, LANES), F32)],
        compiler_params=_cparams(("arbitrary", "arbitrary")),
        name="rg_scan_bwd" if reverse else "rg_scan_fwd",
    )(u, h0, w_a, b_a, w_x, b_x, lam)


def _rg_out_kernel(x_ref, hf_ref, hb_ref, gl_ref, g1_ref, sh_ref, sc_ref, ng_ref, wout_ref, wr_ref,
                   x1_ref, h2_ref, route_ref, *, n_experts):
    y = ((hf_ref[0].astype(F32) + hb_ref[0].astype(F32)) * gl_ref[0].astype(F32)).astype(BF16)
    x1 = x_ref[0] + g1_ref[0] * _dot(y, wout_ref[...])
    x1_ref[0] = x1
    h2 = _prenorm(x1, ng_ref[...], sh_ref[0], sc_ref[0])
    h2_ref[0] = h2
    wr = wr_ref[...]
    w_hi = wr.astype(BF16)
    w_lo = (wr - w_hi.astype(F32)).astype(BF16)
    w_hl = jnp.concatenate([w_hi, w_lo], axis=1)
    h_hi = h2.astype(BF16)
    h_lo = (h2 - h_hi.astype(F32)).astype(BF16)
    parts = _dot(h_hi, w_hl) + _dot(h_lo, w_hl)
    logits = parts[:, 0:LANES] + parts[:, LANES:2 * LANES]
    lane = lax.broadcasted_iota(jnp.int32, logits.shape, 1).astype(F32)
    neg = jnp.float32(-jnp.inf)
    lg = jnp.where(lane < n_experts, logits, neg)
    m1 = jnp.max(lg, axis=-1, keepdims=True)
    i1 = jnp.min(jnp.where(lg == m1, lane, float(LANES)), axis=-1, keepdims=True)
    lg2 = jnp.where(lane == i1, neg, lg)
    m2 = jnp.max(lg2, axis=-1, keepdims=True)
    i2 = jnp.min(jnp.where(lg2 == m2, lane, float(LANES)), axis=-1, keepdims=True)
    e2 = jnp.exp(m2 - m1)
    den = 1.0 + e2
    route_ref[0] = (jnp.where(lane == 0.0, i1, 0.0) + jnp.where(lane == 1.0, i2, 0.0)
                    + jnp.where(lane == 2.0, 1.0 / den, 0.0) + jnp.where(lane == 3.0, e2 / den, 0.0))


def _rg_out(x, hf, hb, gl, gate1, shift2, scale2, norm_g, w_out, w_router_pad, n_experts, tm):
    b, n, d = x.shape
    c = hf.shape[2]
    tok = lambda bi, i: (bi, i, 0)
    vec = lambda bi, i: (bi, 0, 0)
    return pl.pallas_call(
        functools.partial(_rg_out_kernel, n_experts=n_experts),
        grid=(b, n // tm),
        in_specs=[pl.BlockSpec((1, tm, d), tok), pl.BlockSpec((1, tm, c), tok), pl.BlockSpec((1, tm, c), tok),
                  pl.BlockSpec((1, tm, c), tok),
                  pl.BlockSpec((1, 1, d), vec), pl.BlockSpec((1, 1, d), vec), pl.BlockSpec((1, 1, d), vec),
                  _resident(norm_g.shape), _resident(w_out.shape), _resident(w_router_pad.shape)],
        out_specs=(pl.BlockSpec((1, tm, d), tok), pl.BlockSpec((1, tm, d), tok),
                   pl.BlockSpec((1, tm, LANES), tok)),
        out_shape=(jax.ShapeDtypeStruct((b, n, d), F32), jax.ShapeDtypeStruct((b, n, d), F32),
                   jax.ShapeDtypeStruct((b, n, LANES), F32)),
        compiler_params=_cparams(("arbitrary", "arbitrary")),
        name="rg_out_router",
    )(x, hf, hb, gl, gate1, shift2, scale2, norm_g, w_out, w_router_pad)


def _route_tables(e_idx, tm, n_experts):
    t = e_idx.shape[0]
    n_slots = TOP_K * t
    n_tiles = n_slots // tm + n_experts
    e_flat = e_idx.T.reshape(n_slots)
    experts = jnp.arange(n_experts, dtype=jnp.int32)
    counts = jnp.sum((e_flat[:, None] == experts[None, :]).astype(jnp.int32), axis=0)
    padded = (counts + tm - 1) // tm * tm
    ends = jnp.cumsum(padded)
    n_used = (ends[-1] // tm).astype(jnp.int32)
    fill_rank = jnp.arange(tm, dtype=jnp.int32)
    fill_keys = jnp.where(fill_rank[None, :] < (padded - counts)[:, None], experts[:, None], n_experts)
    keys = jnp.concatenate([e_flat, fill_keys.reshape(n_experts * tm)])
    p = jnp.arange(n_tiles * tm, dtype=jnp.int32)
    bits = (n_tiles * tm - 1).bit_length()
    assert (n_experts + 1) << bits < 2 ** 31
    src = jnp.sort(keys * (1 << bits) + p) & ((1 << bits) - 1)
    valid = src < n_slots
    tok = jnp.where(valid, src % t, 0)
    dst = jnp.where(valid, src, n_slots + ((p // tm + 1) % 2) * tm + p % tm)
    tile_ids = jnp.arange(n_tiles, dtype=jnp.int32)
    tile_e = jnp.sum((tile_ids[:, None] * tm >= ends[None, :]).astype(jnp.int32), axis=1)
    last_e = jnp.max(jnp.where(tile_ids < n_used, tile_e, 0))
    tile_e = jnp.where(tile_ids < n_used, tile_e, last_e)
    fill_dst = (n_slots + tm + jnp.arange(tm, dtype=jnp.int32))[None, :]
    dst_tbl = jnp.concatenate([fill_dst, dst.reshape(n_tiles, tm)], axis=0)
    tok_tbl = jnp.concatenate([tok.reshape(n_tiles, tm), jnp.zeros((1, tm), jnp.int32)], axis=0)
    return (tile_e, n_used.reshape(1), tok_tbl.reshape(n_tiles + 1, 1, tm), dst_tbl.reshape(n_tiles + 1, 1, tm))


def _moe_group_kernel(te_ref, nu_ref, tok0_ref, tokn_ref, dstp_ref, dstl_ref, h2_hbm, w1_ref, w3_ref, w2_ref,
                      y_hbm, xbuf, ybuf, gsem, ssem, *, tm, n_chunks):
    i = pl.program_id(0)
    n_used = nu_ref[0]
    slot = lax.rem(i, 2)

    def gather_row(tok_ref, s, r):
        pltpu.make_async_copy(h2_hbm.at[pl.ds(tok_ref[0, 0, r], 1)], xbuf.at[s, pl.ds(r, 1)], gsem.at[s]).start()

    def scatter_row(dst_ref, s, r):
        pltpu.make_async_copy(ybuf.at[s, pl.ds(r, 1)], y_hbm.at[pl.ds(dst_ref[0, 0, r], 1)], ssem.at[s]).start()

    def wait_gather(s):
        pltpu.make_async_copy(h2_hbm.at[pl.ds(0, tm)], xbuf.at[s], gsem.at[s]).wait()

    def wait_scatter(s):
        pltpu.make_async_copy(ybuf.at[s], y_hbm.at[pl.ds(0, tm)], ssem.at[s]).wait()

    @pl.when(i == 0)
    def _():
        ybuf[...] = jnp.zeros(ybuf.shape, F32)
        tail = y_hbm.shape[0] - 2 * tm
        pltpu.make_async_copy(ybuf.at[0], y_hbm.at[pl.ds(tail, tm)], ssem.at[0]).start()

        def body(r, carry):
            gather_row(tok0_ref, 0, r)
            return carry
        lax.fori_loop(0, tm, body, 0, unroll=8)

    @pl.when(i < n_used)
    def _():
        wait_gather(slot)
        wait_scatter(slot)
        x = xbuf[slot].astype(BF16)
        hid = w1_ref.shape[3]
        hc = hid // n_chunks
        rows = tm // n_chunks
        y = None
        for c in range(n_chunks):
            for r in range(c * rows, (c + 1) * rows):
                gather_row(tokn_ref, 1 - slot, r)
                scatter_row(dstp_ref, 1 - slot, r)
            a = _dot(x, w1_ref[0, 0, :, c * hc:(c + 1) * hc])
            g = _dot(x, w3_ref[0, 0, :, c * hc:(c + 1) * hc])
            act = (jax.nn.silu(a) * g).astype(BF16)
            yc = _dot(act, w2_ref[0, 0, c * hc:(c + 1) * hc, :])
            y = yc if y is None else y + yc
        ybuf[slot] = y

    @pl.when(i == pl.num_programs(0) - 1)
    def _():
        last = lax.rem(n_used + 1, 2)

        def body(r, carry):
            scatter_row(dstl_ref, last, r)
            return carry
        lax.fori_loop(0, tm, body, 0, unroll=8)
        wait_scatter(last)
        wait_scatter(1 - last)
        wait_gather(1 - last)


def _moe_group(h2, e_idx, w1, w3, w2, layer, tm):
    t, d = h2.shape
    _, n_e, _, hid = w1.shape
    tile_e, n_used, tok, dst = _route_tables(e_idx, tm, n_e)
    n_tiles = tile_e.shape[0]
    n_chunks = _n_hidden_chunks(hid)
    smem_blk = lambda f: pl.BlockSpec((1, 1, tm), f, memory_space=pltpu.SMEM)
    wspec = lambda shape: pl.BlockSpec((1, 1) + shape, lambda i, te, nu: (layer, te[i], 0, 0),
                                       pipeline_mode=pl.Buffered(1))
    grid_spec = pltpu.PrefetchScalarGridSpec(
        num_scalar_prefetch=2,
        grid=(n_tiles,),
        in_specs=[smem_blk(lambda i, te, nu: (0, 0, 0)),
                  smem_blk(lambda i, te, nu: (i + 1, 0, 0)),
                  smem_blk(lambda i, te, nu: (i, 0, 0)),
                  smem_blk(lambda i, te, nu: (nu[0], 0, 0)),
                  pl.BlockSpec(memory_space=pl.ANY),
                  wspec((d, hid)), wspec((d, hid)), wspec((hid, d))],
        out_specs=pl.BlockSpec(memory_space=pl.ANY),
        scratch_shapes=[pltpu.VMEM((2, tm, d), F32), pltpu.VMEM((2, tm, d), F32),
                        pltpu.SemaphoreType.DMA((2,)), pltpu.SemaphoreType.DMA((2,))],
    )
    return pl.pallas_call(
        functools.partial(_moe_group_kernel, tm=tm, n_chunks=n_chunks),
        grid_spec=grid_spec,
        out_shape=jax.ShapeDtypeStruct((TOP_K * t + 2 * tm, d), F32),
        compiler_params=_cparams(("arbitrary",)),
        name="moe_group",
    )(tile_e, n_used, tok, tok, dst, dst, h2, w1, w3, w2)


def _moe_combine_kernel(x1_ref, y0_ref, y1_ref, route_ref, g2_ref, out_ref):
    out_ref[0] = _moe_combined(x1_ref, y0_ref, y1_ref, route_ref, g2_ref)


def _moe_combine(x1, y, route, gate2, tm):
    b, n, d = x1.shape
    nt = n // tm
    return pl.pallas_call(
        _moe_combine_kernel,
        grid=(b, nt),
        in_specs=_moe_pending_specs(b, n, d, tm),
        out_specs=pl.BlockSpec((1, tm, d), lambda bi, i: (bi, i, 0)),
        out_shape=jax.ShapeDtypeStruct((b, n, d), F32),
        compiler_params=_cparams(("arbitrary", "arbitrary")),
        name="moe_combine",
    )(x1, y, y, route, gate2)


def _moe(h2, route, x1, gate2, w1, w3, w2, layer, tm, defer=False):
    b, n, d = x1.shape
    e_idx = route.reshape(b * n, LANES)[:, :TOP_K].astype(jnp.int32)
    y = _moe_group(h2.reshape(b * n, d), e_idx, w1, w3, w2, layer, tm)
    if defer:
        return (x1, y, route, gate2)
    return _moe_combine(x1, y, route, gate2, tm)


def _prep_mla_weights(w_dq, g_q, w_uq, w_dkv, g_kv, w_ukv, g_qn, g_kn):
    ql = w_uq.shape[0]
    half = D_ROPE // 2
    pad = LANES - D_ROPE

    def ext_cols(w_rope):
        z = jnp.zeros(w_rope.shape[:-1] + (pad,), w_rope.dtype)
        r1, r2 = w_rope[..., :half], w_rope[..., half:]
        return jnp.concatenate([w_rope, z], -1), jnp.concatenate([-r2, r1, z], -1)

    def ext_gain(g):
        z = jnp.zeros((pad,), g.dtype)
        gr1, gr2 = g[D_NOPE:D_NOPE + half], g[D_NOPE + half:]
        return jnp.concatenate([g[:D_NOPE], gr1, gr2, z, gr2, gr1, z])[None, :].astype(F32)

    wq = w_uq.reshape(ql, N_HEADS, D_QK)
    qa, qb = ext_cols(wq[:, :, D_NOPE:])
    w_uq_ext = jnp.concatenate([wq[:, :, :D_NOPE], qa, qb], -1).reshape(ql, N_HEADS * Q_EXT)
    kl = w_ukv.shape[0]
    ka, kb = ext_cols(w_dkv[:, kl:])
    w_dkv_ext = jnp.concatenate([w_dkv[:, :kl], ka, kb], -1)
    wkv = w_ukv.reshape(kl, N_HEADS, D_NOPE + D_V)
    w_uk = wkv[:, :, :D_NOPE].reshape(kl, N_HEADS * D_NOPE)
    w_uvt = wkv[:, :, D_NOPE:].reshape(kl, N_HEADS * D_V).T
    return dict(w_dq=w_dq.astype(BF16), g_q=g_q[None, :].astype(F32), w_uq=w_uq_ext.astype(BF16),
                g_qn=ext_gain(g_qn), w_dkv=w_dkv_ext.astype(BF16), g_kv=g_kv[None, :].astype(F32),
                w_uk=w_uk.astype(BF16), w_uvt=w_uvt.astype(BF16), g_kn=ext_gain(g_kn))


def _rope_tables(n_tok):
    rows = n_tok // GRID_W
    row = jnp.repeat(jnp.arange(rows, dtype=F32), GRID_W)
    col = jnp.tile(jnp.arange(GRID_W, dtype=F32), rows)
    n_freq = D_ROPE // 4
    inv = ROPE_BASE ** (-jnp.arange(n_freq, dtype=F32) / n_freq)
    ang = jnp.concatenate([row[:, None] * inv, col[:, None] * inv], axis=-1)
    z = jnp.zeros((n_tok, LANES - D_ROPE), F32)
    cos, sin = jnp.cos(ang), jnp.sin(ang)
    return jnp.concatenate([cos, cos, z], -1), jnp.concatenate([sin, sin, z], -1)


def _no_rope_tables(n_tok):
    one = jnp.ones((n_tok, D_ROPE), F32)
    z = jnp.zeros((n_tok, LANES - D_ROPE), F32)
    return jnp.concatenate([one, z], -1), jnp.zeros((n_tok, LANES), F32)


def _n_hidden_chunks(hid):
    for c in range(1, hid // MXU_COLS + 1):
        if hid % (c * MXU_COLS) == 0 and hid // c <= MAX_HIDDEN_CHUNK:
            return c
    return 1


def _tile(n, target):
    t = min(n, target)
    assert n % t == 0, (n, t)
    return t


def _mla_layer(x, ctx, mods_l, mods_c, norm1_g, norm2_g, w, w_o, ffn, need_ctx):
    b, n, d = x[0].shape if isinstance(x, tuple) else x.shape
    nc = ctx.shape[1]
    tm = _tile(n, 512)
    tmc = _tile(nc, 512)
    sh1, sc1, g1, sh2, sc2, g2 = mods_l
    csh1, csc1, cg1, csh2, csc2, cg2 = mods_c
    cos_l, sin_l = _rope_tables(n)
    cos_c, sin_c = _no_rope_tables(nc)
    ql, kl, vtl, *x_comb = _mla_pre(x, sh1, sc1, norm1_g, cos_l, sin_l, w, tm)
    if x_comb:
        x = x_comb[0]
    qc, kc, vtc = _mla_pre(ctx, csh1, csc1, norm1_g, cos_c, sin_c, w, tmc)
    k5 = lambda k, t: k.reshape(k.shape[0], k.shape[1], k.shape[2] // t, t, k.shape[3])
    src_l = (k5(kl, 2 * tm if (n // tm) % 4 == 0 else tm), vtl)
    src_c = (k5(kc, tmc), vtc)
    ol = _attention(ql, [src_l, src_c], _tile(n, 2048))
    x_new = _mla_ffn(x, ol, g1, sh2, sc2, g2, norm2_g, w_o, *ffn, tm)
    ctx_new = ctx
    if need_ctx:
        oc = _attention(qc, [src_c], _tile(nc, 1024))
        ctx_new = _mla_ffn(ctx, oc, cg1, csh2, csc2, cg2, norm2_g, w_o, *ffn, tmc)
    return x_new, ctx_new


def _rg_layer(x, ctx, mods_l, mods_c, norm1_g, norm2_g, rg, moe, need_ctx, defer):
    b, n, d = x.shape
    nc = ctx.shape[1]
    tm = _tile(n, 512)
    tmc = _tile(nc, 512)
    tn = _tile(n, 256)
    tnc = _tile(nc, 256)
    sh1, sc1, g1, sh2, sc2, g2 = mods_l
    csh1, csc1, cg1, csh2, csc2, cg2 = mods_c
    w_in, w_gate, conv_w, conv_b, w_a, b_a, w_x, b_x, lam, w_out = rg
    w_router_pad, mw1, mw3, mw2, layer, n_experts = moe
    ul, gll = _rg_in(x, sh1, sc1, norm1_g, w_in, w_gate, conv_w, conv_b, tm)
    uc, glc = _rg_in(ctx, csh1, csc1, norm1_g, w_in, w_gate, conv_w, conv_b, tmc)
    zero = jnp.zeros((b, w_in.shape[1] // LANES, LANES), F32)
    hl, hc = [], []
    for dr in range(2):
        args = (w_a[dr], b_a[dr], w_x[dr], b_x[dr], lam[dr])
        hcd, h_end = _rg_scan(uc, zero, *args, reverse=bool(dr), tn=tnc)
        hld, _ = _rg_scan(ul, h_end, *args, reverse=bool(dr), tn=tn)
        hl.append(hld)
        hc.append(hcd)
    x1, h2, route = _rg_out(x, hl[0], hl[1], gll, g1, sh2, sc2, norm2_g, w_out, w_router_pad, n_experts, tm)
    x_new = _moe(h2, route, x1, g2, mw1, mw3, mw2, layer, tm, defer=defer)
    ctx_new = ctx
    if need_ctx:
        c1, ch2, croute = _rg_out(ctx, hc[0], hc[1], glc, cg1, csh2, csc2, norm2_g, w_out, w_router_pad,
                                  n_experts, tmc)
        ctx_new = _moe(ch2, croute, c1, cg2, mw1, mw3, mw2, layer, tmc)
    return x_new, ctx_new


def kernel(x, c, ctx, c_ctx, ada_w, ada_b, norm1_g, norm2_g, mla_w_dq, mla_g_q, mla_w_uq, mla_w_dkv, mla_g_kv, mla_w_ukv, mla_g_qn, mla_g_kn, mla_w_o, rg_w_in, rg_w_gate, rg_conv_w, rg_conv_b, rg_w_a, rg_b_a, rg_w_x, rg_b_x, rg_lam, rg_w_out, ffn_w1, ffn_w3, ffn_w2, moe_w_router, moe_w1, moe_w3, moe_w2):
    b, n, d = x.shape
    depth = ada_w.shape[0]
    n_experts = moe_w_router.shape[2]

    rows = 2 * SUBLANES
    cvec = jnp.concatenate([c, c_ctx[None, :], jnp.zeros((rows - b - 1, d), F32)], axis=0)
    mods = _ada_all(cvec, ada_w, ada_b)
    moe_w = (moe_w1.astype(BF16), moe_w3.astype(BF16), moe_w2.astype(BF16))

    for i in range(depth):
        need_ctx = i < depth - 1
        j = i // 2
        chunks = [mods[i, :, k * d:(k + 1) * d] for k in range(6)]
        mods_l = [m[:b, None, :] for m in chunks]
        mods_c = [jnp.broadcast_to(m[b:b + 1, None, :], (b, 1, d)) for m in chunks]
        n1 = norm1_g[i][None, :]
        n2 = norm2_g[i][None, :]
        if i % 2 == 0:
            w = _prep_mla_weights(mla_w_dq[j], mla_g_q[j], mla_w_uq[j], mla_w_dkv[j], mla_g_kv[j],
                                  mla_w_ukv[j], mla_g_qn[j], mla_g_kn[j])
            ffn = (ffn_w1[j].astype(BF16), ffn_w3[j].astype(BF16), ffn_w2[j].astype(BF16))
            x, ctx = _mla_layer(x, ctx, mods_l, mods_c, n1, n2, w, mla_w_o[j].astype(BF16), ffn, need_ctx)
        else:
            rg = (rg_w_in[j].astype(BF16), rg_w_gate[j].astype(BF16), rg_conv_w[j], rg_conv_b[j][None, :],
                  (0.5 * rg_w_a[j]).astype(BF16), 0.5 * rg_b_a[j][:, None, :],
                  (0.5 * rg_w_x[j]).astype(BF16), 0.5 * rg_b_x[j][:, None, :],
                  rg_lam[j][:, None, :], rg_w_out[j].astype(BF16))
            w_router_pad = jnp.concatenate(
                [moe_w_router[j], jnp.zeros((d, LANES - n_experts), F32)], axis=1)
            moe = (w_router_pad, *moe_w, j, n_experts)
            defer = i + 1 < depth
            x, ctx = _rg_layer(x, ctx, mods_l, mods_c, n1, n2, rg, moe, need_ctx, defer)
    return x
```
